```python
import jax
import jax.numpy as jnp
from jax import lax
import numpy as np

D_MODEL = 1024
BATCH = 8
SEQ = 4096
DEPTH = 1

HEAD_DIM = 64
SWA_HEADS = 8
SWA_KV_HEADS = 2
SWA_GROUP = SWA_HEADS // SWA_KV_HEADS
WINDOW = 128
SB_HEADS = 8
BLOCK = 128
SWA_WIDTH = SWA_HEADS * HEAD_DIM
SWA_KV_WIDTH = SWA_KV_HEADS * HEAD_DIM
SB_WIDTH = SB_HEADS * HEAD_DIM
IN_WIDTH = SWA_WIDTH + 2 * SWA_KV_WIDTH + 3 * SB_WIDTH + 2 * D_MODEL
N_EXPERTS = 32
TOP_K = 4
D_FF = D_MODEL
SWIGLU_LIMIT = 7.0
SWIGLU_ALPHA = 1.702
MOE_BLOCK = 128
NORM_EPS = 1e-5

kernel_name = "hybrid_swa_stickbreak_moe_block"


def rms_norm(x, w):
    xf = x.astype(jnp.float32)
    y = xf * lax.rsqrt(jnp.mean(xf * xf, axis=-1, keepdims=True) + NORM_EPS)
    return (y * w.astype(jnp.float32)).astype(x.dtype)


def alibi_slopes(n_heads):
    return jnp.asarray([2.0 ** (-8.0 * (i + 1) / n_heads) for i in range(n_heads)], dtype=jnp.float32)


def in_split_points():
    widths = (SWA_WIDTH, SWA_KV_WIDTH, SWA_KV_WIDTH, SB_WIDTH, SB_WIDTH, SB_WIDTH, D_MODEL, D_MODEL)
    points, acc = [], 0
    for w in widths[:-1]:
        acc += w
        points.append(acc)
    return points


def sliding_window_attention(q, k, v, q_norm_w, k_norm_w, sinks):
    B, S = q.shape[0], q.shape[1]
    nb = S // BLOCK
    q = rms_norm(q, q_norm_w)
    k = rms_norm(k, k_norm_w)
    qb = q.reshape(B, nb, BLOCK, SWA_KV_HEADS, SWA_GROUP, HEAD_DIM)

    def band(t):
        tb = t.reshape(B, nb, BLOCK, SWA_KV_HEADS, HEAD_DIM)
        prev = jnp.pad(tb[:, :-1], ((0, 0), (1, 0), (0, 0), (0, 0), (0, 0)))
        return jnp.concatenate([prev, tb], axis=2)

    kb, vb = band(k), band(v)
    scores = jnp.einsum('bnqhgd,bnkhd->bnhgqk', qb, kb).astype(jnp.float32) * (HEAD_DIM ** -0.5)
    q_pos = jnp.arange(BLOCK)[:, None] + BLOCK
    k_pos = jnp.arange(2 * BLOCK)[None, :]
    rel = q_pos - k_pos
    k_abs = jnp.arange(nb)[:, None, None] * BLOCK - BLOCK + k_pos
    valid = (rel >= 0) & (rel < WINDOW) & (k_abs >= 0)
    slopes = alibi_slopes(SWA_HEADS).reshape(SWA_KV_HEADS, SWA_GROUP)
    scores = scores - slopes[:, :, None, None] * rel.astype(jnp.float32)
    scores = jnp.where(valid[None, :, None, None], scores, -jnp.inf)
    sink = jnp.broadcast_to(sinks.astype(jnp.float32).reshape(1, 1, SWA_KV_HEADS, SWA_GROUP, 1, 1),
                            scores.shape[:-1] + (1,))
    probs = jax.nn.softmax(jnp.concatenate([scores, sink], axis=-1), axis=-1)[..., :-1]
    out = jnp.einsum('bnhgqk,bnkhd->bnqhgd', probs.astype(v.dtype), vb)
    return out.reshape(B, S, SWA_WIDTH)


def stick_breaking_attention(q, k, v):
    B, S = q.shape[0], q.shape[1]
    nb = S // BLOCK
    qb = q.reshape(B, nb, BLOCK, SB_HEADS, HEAD_DIM).transpose(1, 0, 2, 3, 4)
    k_pos = jnp.arange(S)
    scale = HEAD_DIM ** -0.5

    def one_block(args):
        q_blk, start = args
        z = jnp.einsum('bqhd,bkhd->bhqk', q_blk, k).astype(jnp.float32) * scale
        q_pos = start + jnp.arange(BLOCK)
        strict = k_pos[None, :] < q_pos[:, None]
        log_beta = jax.nn.log_sigmoid(z)
        log_keep = jnp.where(strict, jax.nn.log_sigmoid(-z), 0.0)
        between = lax.cumsum(log_keep, axis=3, reverse=True) - log_keep
        weights = jnp.where(strict, jnp.exp(log_beta + between), 0.0)
        return jnp.einsum('bhqk,bkhd->bqhd', weights.astype(v.dtype), v)

    out = lax.map(one_block, (qb, jnp.arange(nb) * BLOCK))
    return out.transpose(1, 0, 2, 3, 4).reshape(B, S, SB_WIDTH)


def moe_ffn(h, router_w, router_b, w_gate_up, b_gate_up, w_down, b_down):
    B, S, D = h.shape
    N = B * S
    xf = h.reshape(N, D)
    logits = (xf @ router_w + router_b).astype(jnp.float32)
    top_logit, top_idx = lax.top_k(logits, TOP_K)
    gate = jax.nn.softmax(top_logit, axis=-1)
    A = N * TOP_K
    flat_e = top_idx.reshape(A)
    order = jnp.argsort(flat_e)
    sorted_e = flat_e[order]
    slot_token = (order // TOP_K).astype(jnp.int32)
    slot_gate = gate.reshape(A)[order]
    counts = jnp.bincount(flat_e, length=N_EXPERTS)
    padded = (counts + MOE_BLOCK - 1) // MOE_BLOCK * MOE_BLOCK
    start = jnp.cumsum(counts) - counts
    pad_end = jnp.cumsum(padded)
    pad_start = pad_end - padded
    dest = pad_start[sorted_e] + jnp.arange(A) - start[sorted_e]
    cap = A + N_EXPERTS * MOE_BLOCK
    n_blocks = cap // MOE_BLOCK
    buf_token = jnp.zeros((cap,), jnp.int32).at[dest].set(slot_token)
    buf_gate = jnp.zeros((cap,), jnp.float32).at[dest].set(slot_gate)
    block_expert = jnp.minimum(
        jnp.searchsorted(pad_end, jnp.arange(n_blocks) * MOE_BLOCK, side='right'), N_EXPERTS - 1)
    xb = xf[buf_token].reshape(n_blocks, MOE_BLOCK, D)

    def expert_block(args):
        x_blk, e = args
        gu = x_blk @ w_gate_up[e] + b_gate_up[e]
        g = jnp.minimum(gu[:, :D_FF], SWIGLU_LIMIT)
        u = jnp.clip(gu[:, D_FF:], -SWIGLU_LIMIT, SWIGLU_LIMIT)
        act = (u + 1.0) * (g * jax.nn.sigmoid(SWIGLU_ALPHA * g))
        return act @ w_down[e] + b_down[e]

    yb = lax.map(expert_block, (xb, block_expert)).reshape(cap, D)
    out = jnp.zeros((N, D), h.dtype).at[buf_token].add(yb * buf_gate[:, None].astype(yb.dtype))
    return out.reshape(B, S, D)


def setup_inputs(seed: int = 0) -> dict:
    key = jax.random.key(seed)
    ks = jax.random.split(key, 16)
    L = DEPTH

    def nrm(k, shape, scale):
        return jax.random.normal(k, shape, jnp.float32) * scale

    return {
        "x": nrm(ks[0], (BATCH, SEQ, D_MODEL), 1.0),
        "attn_norm_w": 1.0 + nrm(ks[1], (L, D_MODEL), 0.02),
        "w_in": nrm(ks[2], (L, D_MODEL, IN_WIDTH), D_MODEL ** -0.5),
        "q_norm_w": 1.0 + nrm(ks[3], (L, HEAD_DIM), 0.02),
        "k_norm_w": 1.0 + nrm(ks[4], (L, HEAD_DIM), 0.02),
        "sinks": nrm(ks[5], (L, SWA_HEADS), 0.5),
        "w_proj_swa": nrm(ks[6], (L, SWA_WIDTH, D_MODEL), SWA_WIDTH ** -0.5),
        "w_proj_sb": nrm(ks[7], (L, SB_WIDTH, D_MODEL), SB_WIDTH ** -0.5),
        "w_out": nrm(ks[8], (L, D_MODEL, D_MODEL), D_MODEL ** -0.5),
        "ffn_norm_w": 1.0 + nrm(ks[9], (L, D_MODEL), 0.02),
        "router_w": nrm(ks[10], (L, D_MODEL, N_EXPERTS), D_MODEL ** -0.5),
        "router_b": nrm(ks[11], (L, N_EXPERTS), 0.01),
        "w_gate_up": nrm(ks[12], (L, N_EXPERTS, D_MODEL, 2 * D_FF), D_MODEL ** -0.5),
        "b_gate_up": nrm(ks[13], (L, N_EXPERTS, 2 * D_FF), 0.01),
        "w_down": nrm(ks[14], (L, N_EXPERTS, D_FF, D_MODEL), D_FF ** -0.5),
        "b_down": nrm(ks[15], (L, N_EXPERTS, D_MODEL), 0.01),
    }


def reference(x, attn_norm_w, w_in, q_norm_w, k_norm_w, sinks, w_proj_swa, w_proj_sb, w_out,
              ffn_norm_w, router_w, router_b, w_gate_up, b_gate_up, w_down, b_down):
    B, S = x.shape[0], x.shape[1]
    h = x
    for layer in range(DEPTH):
        xn = rms_norm(h, attn_norm_w[layer])
        proj = xn @ w_in[layer]
        q_a, k_a, v_a, q_b, k_b, v_b, g_a, g_b = jnp.split(proj, in_split_points(), axis=-1)
        y_a = sliding_window_attention(
            q_a.reshape(B, S, SWA_HEADS, HEAD_DIM),
            k_a.reshape(B, S, SWA_KV_HEADS, HEAD_DIM),
            v_a.reshape(B, S, SWA_KV_HEADS, HEAD_DIM),
            q_norm_w[layer], k_norm_w[layer], sinks[layer])
        y_b = stick_breaking_attention(
            q_b.reshape(B, S, SB_HEADS, HEAD_DIM),
            k_b.reshape(B, S, SB_HEADS, HEAD_DIM),
            v_b.reshape(B, S, SB_HEADS, HEAD_DIM))
        merged = (jax.nn.sigmoid(g_a) * (y_a @ w_proj_swa[layer])
                  + jax.nn.sigmoid(g_b) * (y_b @ w_proj_sb[layer]))
        h = h + merged @ w_out[layer]
        h = h + moe_ffn(rms_norm(h, ffn_norm_w[layer]), router_w[layer], router_b[layer],
                        w_gate_up[layer], b_gate_up[layer], w_down[layer], b_down[layer])
    return h
```

```python
import functools

import jax
import jax.numpy as jnp
from jax import lax
from jax.experimental import pallas as pl
from jax.experimental.pallas import tpu as pltpu

HEAD_DIM = 64
SWA_HEADS = 8
SWA_KV_HEADS = 2
SWA_GROUP = SWA_HEADS // SWA_KV_HEADS
BLOCK = 128
SB_HEADS = 8
SWA_WIDTH = SWA_HEADS * HEAD_DIM
SWA_KV_WIDTH = SWA_KV_HEADS * HEAD_DIM
SB_WIDTH = SB_HEADS * HEAD_DIM
N_EXPERTS = 32
TOP_K = 4
SWIGLU_LIMIT = 7.0
SWIGLU_ALPHA = 1.702
NORM_EPS = 1e-5
ATTN_SCALE = HEAD_DIM ** -0.5

LANES = 128
TOKEN_TILE = 512
FFN_BLOCK = 256
GATHER_ROWS = 512
COMBINE_TILE = 256
VMEM_LIMIT = 48 * 1024 * 1024

F32 = jnp.float32
BF16 = jnp.bfloat16
NEG_BIG = -1e30


def _dot(a, b):
    return jnp.dot(a, b, preferred_element_type=F32)


def _dot_nt(a, b):
    return lax.dot_general(a, b, (((1,), (1,)), ((), ())), preferred_element_type=F32)


def _sigmoid(x):
    return 1.0 / (1.0 + jnp.exp(-x))


def _rms(x, w):
    return x * lax.rsqrt(jnp.mean(x * x, axis=-1, keepdims=True) + NORM_EPS) * w


def _params(*sem):
    return pltpu.CompilerParams(dimension_semantics=sem, vmem_limit_bytes=VMEM_LIMIT)


def _in_proj_kernel(x_ref, nw_ref, w_ref, qa_ref, kva_ref, qb_ref, kb_ref, vb_ref, ga_ref, gb_ref):
    xn = _rms(x_ref[...], nw_ref[...]).astype(BF16)
    off = 0
    for ref, gate in ((qa_ref, False), (kva_ref, False), (qb_ref, False), (kb_ref, False),
                      (vb_ref, False), (ga_ref, True), (gb_ref, True)):
        width = ref.shape[1]
        y = _dot(xn, w_ref[:, off:off + width])
        if gate:
            y = _sigmoid(y)
        ref[...] = y.astype(ref.dtype)
        off += width


def _in_proj(x2, norm_w, w_in):
    n, d = x2.shape
    widths = (SWA_WIDTH, 2 * SWA_KV_WIDTH, SB_WIDTH, SB_WIDTH, SB_WIDTH, d, d)
    tm = TOKEN_TILE
    return pl.pallas_call(
        _in_proj_kernel,
        grid=(n // tm,),
        in_specs=[pl.BlockSpec((tm, d), lambda i: (i, 0)),
                  pl.BlockSpec((1, d), lambda i: (0, 0)),
                  pl.BlockSpec(w_in.shape, lambda i: (0, 0))],
        out_specs=[pl.BlockSpec((tm, w), lambda i: (i, 0)) for w in widths],
        out_shape=[jax.ShapeDtypeStruct((n, w), BF16) for w in widths],
        compiler_params=_params("parallel"),
        name="in_proj",
    )(x2, norm_w.reshape(1, d), w_in)


def _swa_kernel(q_ref, kvc_ref, kvp_ref, qnw_ref, knw_ref, sink_ref, o_ref):
    i = pl.program_id(1)
    q = q_ref[...].astype(F32)
    kvc = kvc_ref[...].astype(F32)
    kvp = kvp_ref[...].astype(F32)
    qnw = qnw_ref[...]
    knw = knw_ref[...]
    row = lax.broadcasted_iota(jnp.int32, (BLOCK, BLOCK), 0)
    col = lax.broadcasted_iota(jnp.int32, (BLOCK, BLOCK), 1)
    rel_c = (row - col).astype(F32)
    rel_p = rel_c + float(BLOCK)
    valid_c = row >= col
    valid_p = jnp.logical_and(col > row, i > 0)
    outs = []
    for g in range(SWA_KV_HEADS):
        ks = slice(g * HEAD_DIM, (g + 1) * HEAD_DIM)
        vs = slice(SWA_KV_WIDTH + g * HEAD_DIM, SWA_KV_WIDTH + (g + 1) * HEAD_DIM)
        kc = _rms(kvc[:, ks], knw).astype(BF16)
        kp = _rms(kvp[:, ks], knw).astype(BF16)
        vc = kvc_ref[:, vs]
        vp = kvp_ref[:, vs]
        for hh in range(SWA_GROUP):
            h = g * SWA_GROUP + hh
            slope = 2.0 ** (-8.0 * (h + 1) / SWA_HEADS)
            qh = (_rms(q[:, h * HEAD_DIM:(h + 1) * HEAD_DIM], qnw) * ATTN_SCALE).astype(BF16)
            sc = jnp.where(valid_c, _dot_nt(qh, kc) - slope * rel_c, NEG_BIG)
            sp = jnp.where(valid_p, _dot_nt(qh, kp) - slope * rel_p, NEG_BIG)
            sink = sink_ref[h]
            m = jnp.maximum(jnp.maximum(jnp.max(sc, axis=-1, keepdims=True),
                                        jnp.max(sp, axis=-1, keepdims=True)), sink)
            pc = jnp.exp(sc - m)
            pp = jnp.exp(sp - m)
            den = (jnp.sum(pc, axis=-1, keepdims=True) + jnp.sum(pp, axis=-1, keepdims=True)
                   + jnp.exp(sink - m))
            o = _dot(pc.astype(BF16), vc) + _dot(pp.astype(BF16), vp)
            outs.append(o / den)
    o_ref[...] = jnp.concatenate(outs, axis=-1).astype(o_ref.dtype)


def _swa(q_a, kv_a, q_norm_w, k_norm_w, sinks, batch, seq):
    n = q_a.shape[0]
    nb = seq // BLOCK
    return pl.pallas_call(
        _swa_kernel,
        grid=(batch, nb),
        in_specs=[pl.BlockSpec((BLOCK, SWA_WIDTH), lambda b, i: (b * nb + i, 0)),
                  pl.BlockSpec((BLOCK, 2 * SWA_KV_WIDTH), lambda b, i: (b * nb + i, 0)),
                  pl.BlockSpec((BLOCK, 2 * SWA_KV_WIDTH), lambda b, i: (b * nb + jnp.maximum(i - 1, 0), 0)),
                  pl.BlockSpec((1, HEAD_DIM), lambda b, i: (0, 0)),
                  pl.BlockSpec((1, HEAD_DIM), lambda b, i: (0, 0)),
                  pl.BlockSpec(memory_space=pltpu.SMEM)],
        out_specs=pl.BlockSpec((BLOCK, SWA_WIDTH), lambda b, i: (b * nb + i, 0)),
        out_shape=jax.ShapeDtypeStruct((n, SWA_WIDTH), BF16),
        compiler_params=_params("parallel", "parallel"),
        name="swa",
    )(q_a, kv_a, kv_a, q_norm_w.reshape(1, HEAD_DIM), k_norm_w.reshape(1, HEAD_DIM), sinks)


def _split_bf16(x):
    hi = x.astype(BF16)
    lo = (x - hi.astype(F32)).astype(BF16)
    return hi, lo


def _sb_kernel(q_ref, k_ref, v_ref, o_ref):
    seq = q_ref.shape[0]
    nq = seq // BLOCK
    lane = lax.broadcasted_iota(jnp.int32, (BLOCK, LANES), 1)
    row = lax.broadcasted_iota(jnp.int32, (BLOCK, BLOCK), 0)
    col = lax.broadcasted_iota(jnp.int32, (BLOCK, BLOCK), 1)
    strict = col < row
    neg_incl = jnp.where(row >= col, -1.0, 0.0).astype(BF16)

    def softplus(z):
        return jnp.maximum(z, 0.0) + jnp.log(1.0 + jnp.exp(-jnp.abs(z)))

    def neg_suffix_sum(sp):
        hi, lo = _split_bf16(sp)
        return _dot(hi, neg_incl) + _dot(lo, neg_incl)

    def q_block(qi, carry):
        q0 = pl.multiple_of(qi * BLOCK, BLOCK)
        q2 = q_ref[pl.ds(q0, BLOCK), :] * ATTN_SCALE
        accs = []
        for par in range(2):
            head_lanes = (lane < HEAD_DIM) if par == 0 else (lane >= HEAD_DIM)
            qm = jnp.where(head_lanes, q2, 0.0).astype(BF16)
            kd = k_ref[pl.ds(q0, BLOCK), :]
            vd = v_ref[pl.ds(q0, BLOCK), :]
            z = _dot_nt(qm, kd)
            c = neg_suffix_sum(jnp.where(strict, softplus(z), 0.0))
            w = jnp.where(strict, jnp.exp(z + c), 0.0)
            acc = _dot(w.astype(BF16), vd)
            run = c[:, 0:1]

            def k_block(jj, state):
                acc, run = state
                k0 = pl.multiple_of((qi - 1 - jj) * BLOCK, BLOCK)
                kb = k_ref[pl.ds(k0, BLOCK), :]
                vb = v_ref[pl.ds(k0, BLOCK), :]
                z = _dot_nt(qm, kb)
                c = neg_suffix_sum(softplus(z))
                w = jnp.exp(z + c + run)
                return acc + _dot(w.astype(BF16), vb), run + c[:, 0:1]

            acc, run = lax.fori_loop(0, qi, k_block, (acc, run))
            accs.append(acc)
        o_ref[pl.ds(q0, BLOCK), :] = jnp.where(lane < HEAD_DIM, accs[0], accs[1]).astype(o_ref.dtype)
        return carry

    lax.fori_loop(0, nq, q_block, 0)


def _sb(q_b, k_b, v_b, batch, seq):
    n = q_b.shape[0]
    spec = pl.BlockSpec((seq, LANES), lambda b, p: (b, p))
    return pl.pallas_call(
        _sb_kernel,
        grid=(batch, SB_WIDTH // LANES),
        in_specs=[spec, spec, spec],
        out_specs=spec,
        out_shape=jax.ShapeDtypeStruct((n, SB_WIDTH), BF16),
        compiler_params=_params("parallel", "parallel"),
        name="stick_breaking",
    )(q_b, k_b, v_b)


def _merge_kernel(ya_ref, yb_ref, ga_ref, gb_ref, x_ref, pa_ref, pb_ref, wo_ref, h_ref):
    m = (ga_ref[...].astype(F32) * _dot(ya_ref[...], pa_ref[...])
         + gb_ref[...].astype(F32) * _dot(yb_ref[...], pb_ref[...]))
    h_ref[...] = x_ref[...] + _dot(m.astype(BF16), wo_ref[...])


def _merge(y_a, y_b, g_a, g_b, x2, p_a, p_b, w_o):
    n, d = x2.shape
    tm = TOKEN_TILE
    row = lambda w: pl.BlockSpec((tm, w), lambda i: (i, 0))
    full = lambda a: pl.BlockSpec(a.shape, lambda i: (0, 0))
    return pl.pallas_call(
        _merge_kernel,
        grid=(n // tm,),
        in_specs=[row(SWA_WIDTH), row(SB_WIDTH), row(d), row(d), row(d), full(p_a), full(p_b), full(w_o)],
        out_specs=row(d),
        out_shape=jax.ShapeDtypeStruct((n, d), F32),
        compiler_params=_params("parallel"),
        name="merge",
    )(y_a, y_b, g_a, g_b, x2, p_a, p_b, w_o)


META_EXPERT = 0
META_GATE = TOP_K
META_RANK = 2 * TOP_K


def _router_kernel(h_ref, nw_ref, rw_ref, rb_ref, hn_ref, meta_ref, cnt_ref, run_ref):
    tm = h_ref.shape[0]

    @pl.when(pl.program_id(0) == 0)
    def _():
        run_ref[...] = jnp.zeros_like(run_ref)

    hn = _rms(h_ref[...], nw_ref[...])
    hn_ref[...] = hn
    logits = jnp.dot(hn, rw_ref[...], preferred_element_type=F32,
                     precision=lax.Precision.HIGHEST) + rb_ref[...]
    lane = lax.broadcasted_iota(jnp.int32, (tm, N_EXPERTS), 1).astype(F32)
    work = logits
    tops, idxs, hots = [], [], []
    for _ in range(TOP_K):
        m = jnp.max(work, axis=-1, keepdims=True)
        idx = jnp.min(jnp.where(work == m, lane, float(N_EXPERTS)), axis=-1, keepdims=True)
        hot = lane == idx
        work = jnp.where(hot, -jnp.inf, work)
        tops.append(m)
        idxs.append(idx)
        hots.append(hot)
    exps = [jnp.exp(t - tops[0]) for t in tops]
    den = exps[0] + exps[1] + exps[2] + exps[3]
    chosen = jnp.where(hots[0] | hots[1] | hots[2] | hots[3], 1.0, 0.0)
    r = lax.broadcasted_iota(jnp.int32, (tm, tm), 0)
    c = lax.broadcasted_iota(jnp.int32, (tm, tm), 1)
    before = jnp.where(c < r, 1.0, 0.0).astype(BF16)
    prefix = _dot(before, chosen.astype(BF16)) + run_ref[...]
    mlane = lax.broadcasted_iota(jnp.int32, (tm, LANES), 1)
    meta = jnp.zeros((tm, LANES), F32)
    for k in range(TOP_K):
        rank = jnp.sum(jnp.where(hots[k], prefix, 0.0), axis=-1, keepdims=True)
        meta = jnp.where(mlane == META_EXPERT + k, idxs[k], meta)
        meta = jnp.where(mlane == META_GATE + k, exps[k] / den, meta)
        meta = jnp.where(mlane == META_RANK + k, rank, meta)
    meta_ref[...] = meta
    run_ref[...] += jnp.sum(chosen, axis=0, keepdims=True)
    cnt_ref[...] = run_ref[...]


def _router(h1, norm_w, router_w, router_b):
    n, d = h1.shape
    tm = TOKEN_TILE
    return pl.pallas_call(
        _router_kernel,
        grid=(n // tm,),
        in_specs=[pl.BlockSpec((tm, d), lambda i: (i, 0)),
                  pl.BlockSpec((1, d), lambda i: (0, 0)),
                  pl.BlockSpec((d, N_EXPERTS), lambda i: (0, 0)),
                  pl.BlockSpec((1, N_EXPERTS), lambda i: (0, 0))],
        out_specs=[pl.BlockSpec((tm, d), lambda i: (i, 0)),
                   pl.BlockSpec((tm, LANES), lambda i: (i, 0)),
                   pl.BlockSpec((1, N_EXPERTS), lambda i: (0, 0))],
        out_shape=[jax.ShapeDtypeStruct((n, d), F32),
                   jax.ShapeDtypeStruct((n, LANES), F32),
                   jax.ShapeDtypeStruct((1, N_EXPERTS), F32)],
        scratch_shapes=[pltpu.VMEM((1, N_EXPERTS), F32)],
        compiler_params=_params("arbitrary"),
        name="router",
    )(h1, norm_w.reshape(1, d), router_w, router_b.reshape(1, N_EXPERTS))


def _dispatch_kernel(tok_ref, src_ref, dst_ref, sem):
    rows = tok_ref.shape[2]
    base = pl.program_id(0) * rows

    def row_copy(i):
        return pltpu.make_async_copy(src_ref.at[pl.ds(tok_ref[0, 0, i], 1)],
                                     dst_ref.at[pl.ds(base + i, 1)], sem)

    def start(i, carry):
        row_copy(i).start()
        return carry

    def wait(i, carry):
        row_copy(i).wait()
        return carry

    lax.fori_loop(0, rows, start, 0)
    lax.fori_loop(0, rows, wait, 0)


def _dispatch(hn, slot_token):
    cap = slot_token.shape[0]
    d = hn.shape[1]
    steps = cap // GATHER_ROWS
    return pl.pallas_call(
        _dispatch_kernel,
        grid=(steps,),
        in_specs=[pl.BlockSpec((1, 1, GATHER_ROWS), lambda i: (i, 0, 0), memory_space=pltpu.SMEM),
                  pl.BlockSpec(memory_space=pl.ANY)],
        out_specs=pl.BlockSpec(memory_space=pl.ANY),
        out_shape=jax.ShapeDtypeStruct((cap, d), hn.dtype),
        scratch_shapes=[pltpu.SemaphoreType.DMA(())],
        compiler_params=_params("arbitrary"),
        name="dispatch",
    )(slot_token.reshape(steps, 1, GATHER_ROWS), hn)


def _ffn_kernel(be_ref, x_ref, wgu_ref, bgu_ref, wd_ref, bd_ref, y_ref):
    del be_ref
    d_ff = wd_ref.shape[1]
    gu = _dot(x_ref[...].astype(BF16), wgu_ref[0]) + bgu_ref[0]
    g = jnp.minimum(gu[:, :d_ff], SWIGLU_LIMIT)
    u = jnp.clip(gu[:, d_ff:], -SWIGLU_LIMIT, SWIGLU_LIMIT)
    act = (u + 1.0) * (g * _sigmoid(SWIGLU_ALPHA * g))
    y_ref[...] = _dot(act.astype(BF16), wd_ref[0]) + bd_ref[0]


def _ffn(xb, block_expert, w_gate_up, b_gate_up, w_down, b_down):
    cap, d = xb.shape
    e, _, two_ff = w_gate_up.shape
    d_ff = two_ff // 2
    grid_spec = pltpu.PrefetchScalarGridSpec(
        num_scalar_prefetch=1,
        grid=(cap // FFN_BLOCK,),
        in_specs=[pl.BlockSpec((FFN_BLOCK, d), lambda b, be: (b, 0)),
                  pl.BlockSpec((1, d, two_ff), lambda b, be: (be[b], 0, 0)),
                  pl.BlockSpec((1, 1, two_ff), lambda b, be: (be[b], 0, 0)),
                  pl.BlockSpec((1, d_ff, d), lambda b, be: (be[b], 0, 0)),
                  pl.BlockSpec((1, 1, d), lambda b, be: (be[b], 0, 0))],
        out_specs=pl.BlockSpec((FFN_BLOCK, d), lambda b, be: (b, 0)),
    )
    return pl.pallas_call(
        _ffn_kernel,
        grid_spec=grid_spec,
        out_shape=jax.ShapeDtypeStruct((cap, d), F32),
        compiler_params=_params("arbitrary"),
        name="expert_ffn",
    )(block_expert, xb, w_gate_up, b_gate_up.reshape(e, 1, two_ff), w_down, b_down.reshape(e, 1, d))


def _combine_kernel(dest_ref, gate_ref, h_ref, y_ref, o_ref, buf_ref, sem):
    tm = h_ref.shape[0]

    def row_copy(i, k):
        return pltpu.make_async_copy(y_ref.at[pl.ds(dest_ref[0, 0, i * TOP_K + k], 1)],
                                     buf_ref.at[k, pl.ds(i, 1)], sem)

    def start(i, carry):
        for k in range(TOP_K):
            row_copy(i, k).start()
        return carry

    def wait(i, carry):
        for k in range(TOP_K):
            row_copy(i, k).wait()
        return carry

    lax.fori_loop(0, tm, start, 0)
    lax.fori_loop(0, tm, wait, 0)
    gate = gate_ref[...]
    out = h_ref[...]
    for k in range(TOP_K):
        out = out + gate[:, k:k + 1] * buf_ref[k]
    o_ref[...] = out


def _combine(dest, gate, h1, yb):
    n, d = h1.shape
    tm = COMBINE_TILE
    steps = n // tm
    return pl.pallas_call(
        _combine_kernel,
        grid=(steps,),
        in_specs=[pl.BlockSpec((1, 1, tm * TOP_K), lambda i: (i, 0, 0), memory_space=pltpu.SMEM),
                  pl.BlockSpec((tm, TOP_K), lambda i: (i, 0)),
                  pl.BlockSpec((tm, d), lambda i: (i, 0)),
                  pl.BlockSpec(memory_space=pl.ANY)],
        out_specs=pl.BlockSpec((tm, d), lambda i: (i, 0)),
        out_shape=jax.ShapeDtypeStruct((n, d), F32),
        scratch_shapes=[pltpu.VMEM((TOP_K, tm, d), F32), pltpu.SemaphoreType.DMA(())],
        compiler_params=_params("arbitrary"),
        name="combine",
    )(dest.reshape(steps, 1, tm * TOP_K), gate, h1, yb)


def _routing_tables(meta, counts, n):
    eidx = meta[:, META_EXPERT:META_EXPERT + TOP_K].astype(jnp.int32)
    gate = meta[:, META_GATE:META_GATE + TOP_K]
    rank = meta[:, META_RANK:META_RANK + TOP_K].astype(jnp.int32)
    counts = counts.reshape(N_EXPERTS).astype(jnp.int32)
    padded = (counts + FFN_BLOCK - 1) // FFN_BLOCK * FFN_BLOCK
    pad_end = jnp.cumsum(padded)
    pad_start = pad_end - padded
    dest = pad_start[eidx] + rank
    cap = n * TOP_K + N_EXPERTS * FFN_BLOCK
    token = jnp.broadcast_to(jnp.arange(n, dtype=jnp.int32)[:, None], (n, TOP_K))
    slot_token = jnp.zeros((cap,), jnp.int32).at[dest.reshape(-1)].set(token.reshape(-1))
    block_start = jnp.arange(cap // FFN_BLOCK, dtype=jnp.int32) * FFN_BLOCK
    block_expert = jnp.minimum(jnp.searchsorted(pad_end, block_start, side='right'),
                               N_EXPERTS - 1).astype(jnp.int32)
    return dest, gate, slot_token, block_expert


def kernel(x, attn_norm_w, w_in, q_norm_w, k_norm_w, sinks, w_proj_swa, w_proj_sb, w_out,
           ffn_norm_w, router_w, router_b, w_gate_up, b_gate_up, w_down, b_down):
    batch, seq, d = x.shape
    n = batch * seq
    h = x.reshape(n, d)
    for layer in range(attn_norm_w.shape[0]):
        q_a, kv_a, q_b, k_b, v_b, g_a, g_b = _in_proj(h, attn_norm_w[layer], w_in[layer].astype(BF16))
        y_a = _swa(q_a, kv_a, q_norm_w[layer], k_norm_w[layer], sinks[layer], batch, seq)
        y_b = _sb(q_b, k_b, v_b, batch, seq)
        h1 = _merge(y_a, y_b, g_a, g_b, h, w_proj_swa[layer].astype(BF16),
                    w_proj_sb[layer].astype(BF16), w_out[layer].astype(BF16))
        hn, meta, counts = _router(h1, ffn_norm_w[layer], router_w[layer], router_b[layer])
        dest, gate, slot_token, block_expert = _routing_tables(meta, counts, n)
        xb = _dispatch(hn, slot_token)
        yb = _ffn(xb, block_expert, w_gate_up[layer].astype(BF16), b_gate_up[layer],
                  w_down[layer].astype(BF16), b_down[layer])
        h = _combine(dest, gate, h1, yb)
    return h.reshape(batch, seq, d)
```

```python
import functools

import jax
import jax.numpy as jnp
from jax import lax
from jax.experimental import pallas as pl
from jax.experimental.pallas import tpu as pltpu

HEAD_DIM = 64
SWA_HEADS = 8
SWA_KV_HEADS = 2
SWA_GROUP = SWA_HEADS // SWA_KV_HEADS
BLOCK = 128
SB_HEADS = 8
SWA_WIDTH = SWA_HEADS * HEAD_DIM
SWA_KV_WIDTH = SWA_KV_HEADS * HEAD_DIM
SB_WIDTH = SB_HEADS * HEAD_DIM
N_EXPERTS = 32
TOP_K = 4
SWIGLU_LIMIT = 7.0
SWIGLU_ALPHA = 1.702
NORM_EPS = 1e-5
ATTN_SCALE = HEAD_DIM ** -0.5

LANES = 128
TOKEN_TILE = 512
FFN_BLOCK = 256
GATHER_ROWS = 512
COMBINE_TILE = 256
SB_Q_ROWS = 512
SB_K_ROWS = 256
VMEM_LIMIT = 48 * 1024 * 1024

F32 = jnp.float32
BF16 = jnp.bfloat16
NEG_BIG = -1e30


def _dot(a, b):
    return jnp.dot(a, b, preferred_element_type=F32)


def _dot_nt(a, b):
    return lax.dot_general(a, b, (((1,), (1,)), ((), ())), preferred_element_type=F32)


def _sigmoid(x):
    return 1.0 / (1.0 + jnp.exp(-x))


def _rms(x, w):
    return x * lax.rsqrt(jnp.mean(x * x, axis=-1, keepdims=True) + NORM_EPS) * w


def _params(*sem):
    return pltpu.CompilerParams(dimension_semantics=sem, vmem_limit_bytes=VMEM_LIMIT)


def _in_proj_kernel(x_ref, nw_ref, w_ref, qa_ref, kva_ref, qb_ref, kb_ref, vb_ref, ga_ref, gb_ref):
    xn = _rms(x_ref[...], nw_ref[...]).astype(BF16)
    off = 0
    for ref, gate in ((qa_ref, False), (kva_ref, False), (qb_ref, False), (kb_ref, False),
                      (vb_ref, False), (ga_ref, True), (gb_ref, True)):
        width = ref.shape[1]
        y = _dot(xn, w_ref[:, off:off + width])
        if gate:
            y = _sigmoid(y)
        ref[...] = y.astype(ref.dtype)
        off += width


def _in_proj(x2, norm_w, w_in):
    n, d = x2.shape
    widths = (SWA_WIDTH, 2 * SWA_KV_WIDTH, SB_WIDTH, SB_WIDTH, SB_WIDTH, d, d)
    tm = TOKEN_TILE
    return pl.pallas_call(
        _in_proj_kernel,
        grid=(n // tm,),
        in_specs=[pl.BlockSpec((tm, d), lambda i: (i, 0)),
                  pl.BlockSpec((1, d), lambda i: (0, 0)),
                  pl.BlockSpec(w_in.shape, lambda i: (0, 0))],
        out_specs=[pl.BlockSpec((tm, w), lambda i: (i, 0)) for w in widths],
        out_shape=[jax.ShapeDtypeStruct((n, w), BF16) for w in widths],
        compiler_params=_params("parallel"),
        name="in_proj",
    )(x2, norm_w.reshape(1, d), w_in)


def _swa_kernel(q_ref, kvc_ref, kvp_ref, qnw_ref, knw_ref, sink_ref, o_ref):
    i = pl.program_id(1)
    q = q_ref[...].astype(F32)
    kvc = kvc_ref[...].astype(F32)
    kvp = kvp_ref[...].astype(F32)
    qnw = qnw_ref[...]
    knw = knw_ref[...]
    row = lax.broadcasted_iota(jnp.int32, (BLOCK, BLOCK), 0)
    col = lax.broadcasted_iota(jnp.int32, (BLOCK, BLOCK), 1)
    rel_c = (row - col).astype(F32)
    rel_p = rel_c + float(BLOCK)
    valid_c = row >= col
    valid_p = jnp.logical_and(col > row, i > 0)
    outs = []
    for g in range(SWA_KV_HEADS):
        ks = slice(g * HEAD_DIM, (g + 1) * HEAD_DIM)
        vs = slice(SWA_KV_WIDTH + g * HEAD_DIM, SWA_KV_WIDTH + (g + 1) * HEAD_DIM)
        kc = _rms(kvc[:, ks], knw).astype(BF16)
        kp = _rms(kvp[:, ks], knw).astype(BF16)
        vc = kvc_ref[:, vs]
        vp = kvp_ref[:, vs]
        for hh in range(SWA_GROUP):
            h = g * SWA_GROUP + hh
            slope = 2.0 ** (-8.0 * (h + 1) / SWA_HEADS)
            qh = (_rms(q[:, h * HEAD_DIM:(h + 1) * HEAD_DIM], qnw) * ATTN_SCALE).astype(BF16)
            sc = jnp.where(valid_c, _dot_nt(qh, kc) - slope * rel_c, NEG_BIG)
            sp = jnp.where(valid_p, _dot_nt(qh, kp) - slope * rel_p, NEG_BIG)
            sink = sink_ref[h]
            m = jnp.maximum(jnp.maximum(jnp.max(sc, axis=-1, keepdims=True),
                                        jnp.max(sp, axis=-1, keepdims=True)), sink)
            pc = jnp.exp(sc - m)
            pp = jnp.exp(sp - m)
            den = (jnp.sum(pc, axis=-1, keepdims=True) + jnp.sum(pp, axis=-1, keepdims=True)
                   + jnp.exp(sink - m))
            o = _dot(pc.astype(BF16), vc) + _dot(pp.astype(BF16), vp)
            outs.append(o / den)
    o_ref[...] = jnp.concatenate(outs, axis=-1).astype(o_ref.dtype)


def _swa(q_a, kv_a, q_norm_w, k_norm_w, sinks, batch, seq):
    n = q_a.shape[0]
    nb = seq // BLOCK
    return pl.pallas_call(
        _swa_kernel,
        grid=(batch, nb),
        in_specs=[pl.BlockSpec((BLOCK, SWA_WIDTH), lambda b, i: (b * nb + i, 0)),
                  pl.BlockSpec((BLOCK, 2 * SWA_KV_WIDTH), lambda b, i: (b * nb + i, 0)),
                  pl.BlockSpec((BLOCK, 2 * SWA_KV_WIDTH), lambda b, i: (b * nb + jnp.maximum(i - 1, 0), 0)),
                  pl.BlockSpec((1, HEAD_DIM), lambda b, i: (0, 0)),
                  pl.BlockSpec((1, HEAD_DIM), lambda b, i: (0, 0)),
                  pl.BlockSpec(memory_space=pltpu.SMEM)],
        out_specs=pl.BlockSpec((BLOCK, SWA_WIDTH), lambda b, i: (b * nb + i, 0)),
        out_shape=jax.ShapeDtypeStruct((n, SWA_WIDTH), BF16),
        compiler_params=_params("parallel", "parallel"),
        name="swa",
    )(q_a, kv_a, kv_a, q_norm_w.reshape(1, HEAD_DIM), k_norm_w.reshape(1, HEAD_DIM), sinks)


def _split_bf16(x):
    hi = x.astype(BF16)
    lo = (x - hi.astype(F32)).astype(BF16)
    return hi, lo


def _sb_kernel(q_ref, k_ref, v_ref, o_ref):
    seq = q_ref.shape[0]
    tq, tk = SB_Q_ROWS, SB_K_ROWS
    nq = seq // tq
    k_per_q = tq // tk
    lane = lax.broadcasted_iota(jnp.int32, (tq, LANES), 1)
    row = lax.broadcasted_iota(jnp.int32, (tq, tk), 0)
    col = lax.broadcasted_iota(jnp.int32, (tq, tk), 1)
    neg_incl = jnp.where(lax.broadcasted_iota(jnp.int32, (tk, tk), 0)
                         >= lax.broadcasted_iota(jnp.int32, (tk, tk), 1), -1.0, 0.0).astype(BF16)

    def softplus(z):
        return jnp.maximum(z, 0.0) + jnp.log(1.0 + jnp.exp(-jnp.abs(z)))

    def neg_suffix_sum(sp):
        hi, lo = _split_bf16(sp)
        return _dot(hi, neg_incl) + _dot(lo, neg_incl)

    def k_block(k0, qms, state, mask):
        kb = k_ref[pl.ds(k0, tk), :]
        vb = v_ref[pl.ds(k0, tk), :]
        accs, runs = [], []
        for par in range(2):
            z = _dot_nt(qms[par], kb)
            sp = softplus(z)
            if mask is not None:
                sp = jnp.where(mask, sp, 0.0)
            c = neg_suffix_sum(sp)
            w = jnp.exp(z + c + state[2 + par])
            if mask is not None:
                w = jnp.where(mask, w, 0.0)
            accs.append(state[par] + _dot(w.astype(BF16), vb))
            runs.append(state[2 + par] + c[:, 0:1])
        return (*accs, *runs)

    def q_block(qi, carry):
        q0 = pl.multiple_of(qi * tq, tq)
        q2 = q_ref[pl.ds(q0, tq), :] * ATTN_SCALE
        qms = (jnp.where(lane < HEAD_DIM, q2, 0.0).astype(BF16),
               jnp.where(lane >= HEAD_DIM, q2, 0.0).astype(BF16))
        state = (jnp.zeros((tq, LANES), F32), jnp.zeros((tq, LANES), F32),
                 jnp.zeros((tq, 1), F32), jnp.zeros((tq, 1), F32))
        for d in reversed(range(k_per_q)):
            state = k_block(q0 + d * tk, qms, state, (col + d * tk) < row)

        def earlier(jj, state):
            for d in reversed(range(k_per_q)):
                k0 = pl.multiple_of((qi - 1 - jj) * tq + d * tk, tk)
                state = k_block(k0, qms, state, None)
            return state

        state = lax.fori_loop(0, qi, earlier, state)
        o_ref[pl.ds(q0, tq), :] = jnp.where(lane < HEAD_DIM, state[0], state[1]).astype(o_ref.dtype)
        return carry

    lax.fori_loop(0, nq, q_block, 0)


def _sb(q_b, k_b, v_b, batch, seq):
    n = q_b.shape[0]
    spec = pl.BlockSpec((seq, LANES), lambda b, p: (b, p))
    return pl.pallas_call(
        _sb_kernel,
        grid=(batch, SB_WIDTH // LANES),
        in_specs=[spec, spec, spec],
        out_specs=spec,
        out_shape=jax.ShapeDtypeStruct((n, SB_WIDTH), BF16),
        compiler_params=_params("parallel", "parallel"),
        name="stick_breaking",
    )(q_b, k_b, v_b)


def _merge_kernel(ya_ref, yb_ref, ga_ref, gb_ref, x_ref, pa_ref, pb_ref, wo_ref, h_ref):
    m = (ga_ref[...].astype(F32) * _dot(ya_ref[...], pa_ref[...])
         + gb_ref[...].astype(F32) * _dot(yb_ref[...], pb_ref[...]))
    h_ref[...] = x_ref[...] + _dot(m.astype(BF16), wo_ref[...])


def _merge(y_a, y_b, g_a, g_b, x2, p_a, p_b, w_o):
    n, d = x2.shape
    tm = TOKEN_TILE
    row = lambda w: pl.BlockSpec((tm, w), lambda i: (i, 0))
    full = lambda a: pl.BlockSpec(a.shape, lambda i: (0, 0))
    return pl.pallas_call(
        _merge_kernel,
        grid=(n // tm,),
        in_specs=[row(SWA_WIDTH), row(SB_WIDTH), row(d), row(d), row(d), full(p_a), full(p_b), full(w_o)],
        out_specs=row(d),
        out_shape=jax.ShapeDtypeStruct((n, d), F32),
        compiler_params=_params("parallel"),
        name="merge",
    )(y_a, y_b, g_a, g_b, x2, p_a, p_b, w_o)


META_EXPERT = 0
META_GATE = TOP_K
META_RANK = 2 * TOP_K


def _router_kernel(h_ref, nw_ref, rw_ref, rb_ref, hn_ref, meta_ref, cnt_ref, run_ref):
    tm = h_ref.shape[0]

    @pl.when(pl.program_id(0) == 0)
    def _():
        run_ref[...] = jnp.zeros_like(run_ref)

    hn = _rms(h_ref[...], nw_ref[...])
    hn_ref[...] = hn
    logits = jnp.dot(hn, rw_ref[...], preferred_element_type=F32,
                     precision=lax.Precision.HIGHEST) + rb_ref[...]
    lane = lax.broadcasted_iota(jnp.int32, (tm, N_EXPERTS), 1).astype(F32)
    work = logits
    tops, idxs, hots = [], [], []
    for _ in range(TOP_K):
        m = jnp.max(work, axis=-1, keepdims=True)
        idx = jnp.min(jnp.where(work == m, lane, float(N_EXPERTS)), axis=-1, keepdims=True)
        hot = lane == idx
        work = jnp.where(hot, -jnp.inf, work)
        tops.append(m)
        idxs.append(idx)
        hots.append(hot)
    exps = [jnp.exp(t - tops[0]) for t in tops]
    den = exps[0] + exps[1] + exps[2] + exps[3]
    chosen = jnp.where(hots[0] | hots[1] | hots[2] | hots[3], 1.0, 0.0)
    r = lax.broadcasted_iota(jnp.int32, (tm, tm), 0)
    c = lax.broadcasted_iota(jnp.int32, (tm, tm), 1)
    before = jnp.where(c < r, 1.0, 0.0).astype(BF16)
    prefix = _dot(before, chosen.astype(BF16)) + run_ref[...]
    mlane = lax.broadcasted_iota(jnp.int32, (tm, LANES), 1)
    meta = jnp.zeros((tm, LANES), F32)
    for k in range(TOP_K):
        rank = jnp.sum(jnp.where(hots[k], prefix, 0.0), axis=-1, keepdims=True)
        meta = jnp.where(mlane == META_EXPERT + k, idxs[k], meta)
        meta = jnp.where(mlane == META_GATE + k, exps[k] / den, meta)
        meta = jnp.where(mlane == META_RANK + k, rank, meta)
    meta_ref[...] = meta
    run_ref[...] += jnp.sum(chosen, axis=0, keepdims=True)
    cnt_ref[...] = run_ref[...]


def _router(h1, norm_w, router_w, router_b):
    n, d = h1.shape
    tm = TOKEN_TILE
    return pl.pallas_call(
        _router_kernel,
        grid=(n // tm,),
        in_specs=[pl.BlockSpec((tm, d), lambda i: (i, 0)),
                  pl.BlockSpec((1, d), lambda i: (0, 0)),
                  pl.BlockSpec((d, N_EXPERTS), lambda i: (0, 0)),
                  pl.BlockSpec((1, N_EXPERTS), lambda i: (0, 0))],
        out_specs=[pl.BlockSpec((tm, d), lambda i: (i, 0)),
                   pl.BlockSpec((tm, LANES), lambda i: (i, 0)),
                   pl.BlockSpec((1, N_EXPERTS), lambda i: (0, 0))],
        out_shape=[jax.ShapeDtypeStruct((n, d), F32),
                   jax.ShapeDtypeStruct((n, LANES), F32),
                   jax.ShapeDtypeStruct((1, N_EXPERTS), F32)],
        scratch_shapes=[pltpu.VMEM((1, N_EXPERTS), F32)],
        compiler_params=_params("arbitrary"),
        name="router",
    )(h1, norm_w.reshape(1, d), router_w, router_b.reshape(1, N_EXPERTS))


def _dispatch_kernel(tok_ref, src_ref, dst_ref, sem):
    rows = tok_ref.shape[2]

    def row_copy(i):
        return pltpu.make_async_copy(src_ref.at[pl.ds(tok_ref[0, 0, i], 1)],
                                     dst_ref.at[pl.ds(i, 1)], sem)

    def start(i, carry):
        row_copy(i).start()
        return carry

    def wait(i, carry):
        row_copy(i).wait()
        return carry

    lax.fori_loop(0, rows, start, 0)
    lax.fori_loop(0, rows, wait, 0)


def _dispatch(hn, slot_token):
    cap = slot_token.shape[0]
    d = hn.shape[1]
    steps = cap // GATHER_ROWS
    return pl.pallas_call(
        _dispatch_kernel,
        grid=(steps,),
        in_specs=[pl.BlockSpec((1, 1, GATHER_ROWS), lambda i: (i, 0, 0), memory_space=pltpu.SMEM),
                  pl.BlockSpec(memory_space=pl.ANY)],
        out_specs=pl.BlockSpec((GATHER_ROWS, d), lambda i: (i, 0)),
        out_shape=jax.ShapeDtypeStruct((cap, d), hn.dtype),
        scratch_shapes=[pltpu.SemaphoreType.DMA(())],
        compiler_params=_params("arbitrary"),
        name="dispatch",
    )(slot_token.reshape(steps, 1, GATHER_ROWS), hn)


def _ffn_kernel(be_ref, x_ref, wgu_ref, bgu_ref, wd_ref, bd_ref, y_ref):
    del be_ref
    d_ff = wd_ref.shape[1]
    gu = _dot(x_ref[...].astype(BF16), wgu_ref[0]) + bgu_ref[0]
    g = jnp.minimum(gu[:, :d_ff], SWIGLU_LIMIT)
    u = jnp.clip(gu[:, d_ff:], -SWIGLU_LIMIT, SWIGLU_LIMIT)
    act = (u + 1.0) * (g * _sigmoid(SWIGLU_ALPHA * g))
    y_ref[...] = _dot(act.astype(BF16), wd_ref[0]) + bd_ref[0]


def _ffn(xb, block_expert, w_gate_up, b_gate_up, w_down, b_down):
    cap, d = xb.shape
    e, _, two_ff = w_gate_up.shape
    d_ff = two_ff // 2
    grid_spec = pltpu.PrefetchScalarGridSpec(
        num_scalar_prefetch=1,
        grid=(cap // FFN_BLOCK,),
        in_specs=[pl.BlockSpec((FFN_BLOCK, d), lambda b, be: (b, 0)),
                  pl.BlockSpec((1, d, two_ff), lambda b, be: (be[b], 0, 0)),
                  pl.BlockSpec((1, 1, two_ff), lambda b, be: (be[b], 0, 0)),
                  pl.BlockSpec((1, d_ff, d), lambda b, be: (be[b], 0, 0)),
                  pl.BlockSpec((1, 1, d), lambda b, be: (be[b], 0, 0))],
        out_specs=pl.BlockSpec((FFN_BLOCK, d), lambda b, be: (b, 0)),
    )
    return pl.pallas_call(
        _ffn_kernel,
        grid_spec=grid_spec,
        out_shape=jax.ShapeDtypeStruct((cap, d), F32),
        compiler_params=_params("arbitrary"),
        name="expert_ffn",
    )(block_expert, xb, w_gate_up, b_gate_up.reshape(e, 1, two_ff), w_down, b_down.reshape(e, 1, d))


def _combine_kernel(dest_ref, gate_ref, h_ref, y_ref, o_ref, buf_ref, sem):
    tm = h_ref.shape[0]

    def row_copy(i, k):
        return pltpu.make_async_copy(y_ref.at[pl.ds(dest_ref[0, 0, i * TOP_K + k], 1)],
                                     buf_ref.at[k, pl.ds(i, 1)], sem)

    def start(i, carry):
        for k in range(TOP_K):
            row_copy(i, k).start()
        return carry

    def wait(i, carry):
        for k in range(TOP_K):
            row_copy(i, k).wait()
        return carry

    lax.fori_loop(0, tm, start, 0)
    lax.fori_loop(0, tm, wait, 0)
    gate = gate_ref[...]
    out = h_ref[...]
    for k in range(TOP_K):
        out = out + gate[:, k:k + 1] * buf_ref[k]
    o_ref[...] = out


def _combine(dest, gate, h1, yb):
    n, d = h1.shape
    tm = COMBINE_TILE
    steps = n // tm
    return pl.pallas_call(
        _combine_kernel,
        grid=(steps,),
        in_specs=[pl.BlockSpec((1, 1, tm * TOP_K), lambda i: (i, 0, 0), memory_space=pltpu.SMEM),
                  pl.BlockSpec((tm, TOP_K), lambda i: (i, 0)),
                  pl.BlockSpec((tm, d), lambda i: (i, 0)),
                  pl.BlockSpec(memory_space=pl.ANY)],
        out_specs=pl.BlockSpec((tm, d), lambda i: (i, 0)),
        out_shape=jax.ShapeDtypeStruct((n, d), F32),
        scratch_shapes=[pltpu.VMEM((TOP_K, tm, d), F32), pltpu.SemaphoreType.DMA(())],
        compiler_params=_params("arbitrary"),
        name="combine",
    )(dest.reshape(steps, 1, tm * TOP_K), gate, h1, yb)


def _routing_tables(meta, counts, n):
    eidx = meta[:, META_EXPERT:META_EXPERT + TOP_K].astype(jnp.int32)
    gate = meta[:, META_GATE:META_GATE + TOP_K]
    rank = meta[:, META_RANK:META_RANK + TOP_K].astype(jnp.int32)
    counts = counts.reshape(N_EXPERTS).astype(jnp.int32)
    padded = (counts + FFN_BLOCK - 1) // FFN_BLOCK * FFN_BLOCK
    pad_end = jnp.cumsum(padded)
    pad_start = pad_end - padded
    dest = pad_start[eidx] + rank
    cap = n * TOP_K + N_EXPERTS * FFN_BLOCK
    token = jnp.broadcast_to(jnp.arange(n, dtype=jnp.int32)[:, None], (n, TOP_K))
    slot_token = jnp.zeros((cap,), jnp.int32).at[dest.reshape(-1)].set(token.reshape(-1))
    block_start = jnp.arange(cap // FFN_BLOCK, dtype=jnp.int32) * FFN_BLOCK
    block_expert = jnp.minimum(jnp.searchsorted(pad_end, block_start, side='right'),
                               N_EXPERTS - 1).astype(jnp.int32)
    return dest, gate, slot_token, block_expert


def kernel(x, attn_norm_w, w_in, q_norm_w, k_norm_w, sinks, w_proj_swa, w_proj_sb, w_out,
           ffn_norm_w, router_w, router_b, w_gate_up, b_gate_up, w_down, b_down):
    batch, seq, d = x.shape
    n = batch * seq
    h = x.reshape(n, d)
    for layer in range(attn_norm_w.shape[0]):
        q_a, kv_a, q_b, k_b, v_b, g_a, g_b = _in_proj(h, attn_norm_w[layer], w_in[layer].astype(BF16))
        y_a = _swa(q_a, kv_a, q_norm_w[layer], k_norm_w[layer], sinks[layer], batch, seq)
        y_b = _sb(q_b, k_b, v_b, batch, seq)
        h1 = _merge(y_a, y_b, g_a, g_b, h, w_proj_swa[layer].astype(BF16),
                    w_proj_sb[layer].astype(BF16), w_out[layer].astype(BF16))
        hn, meta, counts = _router(h1, ffn_norm_w[layer], router_w[layer], router_b[layer])
        dest, gate, slot_token, block_expert = _routing_tables(meta, counts, n)
        xb = _dispatch(hn, slot_token)
        yb = _ffn(xb, block_expert, w_gate_up[layer].astype(BF16), b_gate_up[layer],
                  w_down[layer].astype(BF16), b_down[layer])
        h = _combine(dest, gate, h1, yb)
    return h.reshape(batch, seq, d)
```

```python
import jax
import jax.numpy as jnp
from jax import lax
from jax.experimental import pallas as pl
from jax.experimental.pallas import tpu as pltpu

HEAD_DIM = 64
SWA_HEADS = 8
SWA_KV_HEADS = 2
SWA_GROUP = SWA_HEADS // SWA_KV_HEADS
BLOCK = 128
SB_HEADS = 8
SWA_WIDTH = SWA_HEADS * HEAD_DIM
SWA_KV_WIDTH = SWA_KV_HEADS * HEAD_DIM
SB_WIDTH = SB_HEADS * HEAD_DIM
N_EXPERTS = 32
TOP_K = 4
SWIGLU_LIMIT = 7.0
SWIGLU_ALPHA = 1.702
NORM_EPS = 1e-5
ATTN_SCALE = HEAD_DIM ** -0.5

LANES = 128
SUBLANES = 8
TOKEN_TILE = 512
FFN_BLOCK = 256
CHUNK = 32
SORTED_ROWS = TOKEN_TILE * TOP_K + N_EXPERTS * CHUNK
SB_Q_ROWS = 512
SB_K_ROWS = 256
VMEM_LIMIT = 56 * 1024 * 1024

F32 = jnp.float32
BF16 = jnp.bfloat16
NEG_BIG = -1e30


def _dot(a, b):
    return jnp.dot(a, b, preferred_element_type=F32)


def _dot_nt(a, b):
    return lax.dot_general(a, b, (((1,), (1,)), ((), ())), preferred_element_type=F32)


def _sigmoid(x):
    return 1.0 / (1.0 + jnp.exp(-x))


def _rms(x, w):
    return x * lax.rsqrt(jnp.mean(x * x, axis=-1, keepdims=True) + NORM_EPS) * w


def _params(*sem):
    return pltpu.CompilerParams(dimension_semantics=sem, vmem_limit_bytes=VMEM_LIMIT)


def _in_proj_kernel(x_ref, nw_ref, w_ref, qa_ref, kva_ref, qb_ref, kb_ref, vb_ref, ga_ref, gb_ref):
    xn = _rms(x_ref[...], nw_ref[...]).astype(BF16)
    off = 0
    for ref, gate in ((qa_ref, False), (kva_ref, False), (qb_ref, False), (kb_ref, False),
                      (vb_ref, False), (ga_ref, True), (gb_ref, True)):
        width = ref.shape[1]
        y = _dot(xn, w_ref[:, off:off + width])
        if gate:
            y = _sigmoid(y)
        ref[...] = y.astype(ref.dtype)
        off += width


def _in_proj(x2, norm_w, w_in):
    n, d = x2.shape
    widths = (SWA_WIDTH, 2 * SWA_KV_WIDTH, SB_WIDTH, SB_WIDTH, SB_WIDTH, d, d)
    tm = TOKEN_TILE
    return pl.pallas_call(
        _in_proj_kernel,
        grid=(n // tm,),
        in_specs=[pl.BlockSpec((tm, d), lambda i: (i, 0)),
                  pl.BlockSpec((1, d), lambda i: (0, 0)),
                  pl.BlockSpec(w_in.shape, lambda i: (0, 0))],
        out_specs=[pl.BlockSpec((tm, w), lambda i: (i, 0)) for w in widths],
        out_shape=[jax.ShapeDtypeStruct((n, w), BF16) for w in widths],
        compiler_params=_params("parallel"),
        name="in_proj",
    )(x2, norm_w.reshape(1, d), w_in)


def _swa_kernel(q_ref, kvc_ref, kvp_ref, qnw_ref, knw_ref, sink_ref, o_ref):
    i = pl.program_id(1)
    q = q_ref[...].astype(F32)
    kvc = kvc_ref[...].astype(F32)
    kvp = kvp_ref[...].astype(F32)
    qnw = qnw_ref[...]
    knw = knw_ref[...]
    row = lax.broadcasted_iota(jnp.int32, (BLOCK, BLOCK), 0)
    col = lax.broadcasted_iota(jnp.int32, (BLOCK, BLOCK), 1)
    rel_c = (row - col).astype(F32)
    rel_p = rel_c + float(BLOCK)
    valid_c = row >= col
    valid_p = jnp.logical_and(col > row, i > 0)
    outs = []
    for g in range(SWA_KV_HEADS):
        ks = slice(g * HEAD_DIM, (g + 1) * HEAD_DIM)
        vs = slice(SWA_KV_WIDTH + g * HEAD_DIM, SWA_KV_WIDTH + (g + 1) * HEAD_DIM)
        kc = _rms(kvc[:, ks], knw).astype(BF16)
        kp = _rms(kvp[:, ks], knw).astype(BF16)
        vc = kvc_ref[:, vs]
        vp = kvp_ref[:, vs]
        for hh in range(SWA_GROUP):
            h = g * SWA_GROUP + hh
            slope = 2.0 ** (-8.0 * (h + 1) / SWA_HEADS)
            qh = (_rms(q[:, h * HEAD_DIM:(h + 1) * HEAD_DIM], qnw) * ATTN_SCALE).astype(BF16)
            sc = jnp.where(valid_c, _dot_nt(qh, kc) - slope * rel_c, NEG_BIG)
            sp = jnp.where(valid_p, _dot_nt(qh, kp) - slope * rel_p, NEG_BIG)
            sink = sink_ref[h]
            m = jnp.maximum(jnp.maximum(jnp.max(sc, axis=-1, keepdims=True),
                                        jnp.max(sp, axis=-1, keepdims=True)), sink)
            pc = jnp.exp(sc - m)
            pp = jnp.exp(sp - m)
            den = (jnp.sum(pc, axis=-1, keepdims=True) + jnp.sum(pp, axis=-1, keepdims=True)
                   + jnp.exp(sink - m))
            o = _dot(pc.astype(BF16), vc) + _dot(pp.astype(BF16), vp)
            outs.append(o / den)
    o_ref[...] = jnp.concatenate(outs, axis=-1).astype(o_ref.dtype)


def _swa(q_a, kv_a, q_norm_w, k_norm_w, sinks, batch, seq):
    n = q_a.shape[0]
    nb = seq // BLOCK
    return pl.pallas_call(
        _swa_kernel,
        grid=(batch, nb),
        in_specs=[pl.BlockSpec((BLOCK, SWA_WIDTH), lambda b, i: (b * nb + i, 0)),
                  pl.BlockSpec((BLOCK, 2 * SWA_KV_WIDTH), lambda b, i: (b * nb + i, 0)),
                  pl.BlockSpec((BLOCK, 2 * SWA_KV_WIDTH), lambda b, i: (b * nb + jnp.maximum(i - 1, 0), 0)),
                  pl.BlockSpec((1, HEAD_DIM), lambda b, i: (0, 0)),
                  pl.BlockSpec((1, HEAD_DIM), lambda b, i: (0, 0)),
                  pl.BlockSpec(memory_space=pltpu.SMEM)],
        out_specs=pl.BlockSpec((BLOCK, SWA_WIDTH), lambda b, i: (b * nb + i, 0)),
        out_shape=jax.ShapeDtypeStruct((n, SWA_WIDTH), BF16),
        compiler_params=_params("parallel", "parallel"),
        name="swa",
    )(q_a, kv_a, kv_a, q_norm_w.reshape(1, HEAD_DIM), k_norm_w.reshape(1, HEAD_DIM), sinks)


def _split_bf16(x):
    hi = x.astype(BF16)
    lo = (x - hi.astype(F32)).astype(BF16)
    return hi, lo


def _sb_kernel(q_ref, k_ref, v_ref, o_ref):
    seq = q_ref.shape[0]
    tq, tk = SB_Q_ROWS, SB_K_ROWS
    nq = seq // tq
    k_per_q = tq // tk
    lane = lax.broadcasted_iota(jnp.int32, (tq, LANES), 1)
    row = lax.broadcasted_iota(jnp.int32, (tq, tk), 0)
    col = lax.broadcasted_iota(jnp.int32, (tq, tk), 1)
    neg_incl = jnp.where(lax.broadcasted_iota(jnp.int32, (tk, tk), 0)
                         >= lax.broadcasted_iota(jnp.int32, (tk, tk), 1), -1.0, 0.0).astype(BF16)

    def softplus(z):
        return jnp.maximum(z, 0.0) + jnp.log(1.0 + jnp.exp(-jnp.abs(z)))

    def neg_suffix_sum(sp):
        hi, lo = _split_bf16(sp)
        return _dot(hi, neg_incl) + _dot(lo, neg_incl)

    def k_block(k0, qms, state, mask):
        kb = k_ref[pl.ds(k0, tk), :]
        vb = v_ref[pl.ds(k0, tk), :]
        accs, runs = [], []
        for par in range(2):
            z = _dot_nt(qms[par], kb)
            sp = softplus(z)
            if mask is not None:
                sp = jnp.where(mask, sp, 0.0)
            c = neg_suffix_sum(sp)
            w = jnp.exp(z + c + state[2 + par])
            if mask is not None:
                w = jnp.where(mask, w, 0.0)
            accs.append(state[par] + _dot(w.astype(BF16), vb))
            runs.append(state[2 + par] + c[:, 0:1])
        return (*accs, *runs)

    def q_block(qi, carry):
        q0 = pl.multiple_of(qi * tq, tq)
        q2 = q_ref[pl.ds(q0, tq), :] * ATTN_SCALE
        qms = (jnp.where(lane < HEAD_DIM, q2, 0.0).astype(BF16),
               jnp.where(lane >= HEAD_DIM, q2, 0.0).astype(BF16))
        state = (jnp.zeros((tq, LANES), F32), jnp.zeros((tq, LANES), F32),
                 jnp.zeros((tq, 1), F32), jnp.zeros((tq, 1), F32))
        for d in reversed(range(k_per_q)):
            state = k_block(q0 + d * tk, qms, state, (col + d * tk) < row)

        def earlier(jj, state):
            for d in reversed(range(k_per_q)):
                k0 = pl.multiple_of((qi - 1 - jj) * tq + d * tk, tk)
                state = k_block(k0, qms, state, None)
            return state

        state = lax.fori_loop(0, qi, earlier, state)
        o_ref[pl.ds(q0, tq), :] = jnp.where(lane < HEAD_DIM, state[0], state[1]).astype(o_ref.dtype)
        return carry

    lax.fori_loop(0, nq, q_block, 0)


def _sb(q_b, k_b, v_b, batch, seq):
    n = q_b.shape[0]
    spec = pl.BlockSpec((seq, LANES), lambda b, p: (b, p))
    return pl.pallas_call(
        _sb_kernel,
        grid=(batch, SB_WIDTH // LANES),
        in_specs=[spec, spec, spec],
        out_specs=spec,
        out_shape=jax.ShapeDtypeStruct((n, SB_WIDTH), BF16),
        compiler_params=_params("parallel", "parallel"),
        name="stick_breaking",
    )(q_b, k_b, v_b)


def _merge_kernel(ya_ref, yb_ref, ga_ref, gb_ref, x_ref, pa_ref, pb_ref, wo_ref, h_ref):
    m = (ga_ref[...].astype(F32) * _dot(ya_ref[...], pa_ref[...])
         + gb_ref[...].astype(F32) * _dot(yb_ref[...], pb_ref[...]))
    h_ref[...] = x_ref[...] + _dot(m.astype(BF16), wo_ref[...])


def _merge(y_a, y_b, g_a, g_b, x2, p_a, p_b, w_o):
    n, d = x2.shape
    tm = TOKEN_TILE
    row = lambda w: pl.BlockSpec((tm, w), lambda i: (i, 0))
    full = lambda a: pl.BlockSpec(a.shape, lambda i: (0, 0))
    return pl.pallas_call(
        _merge_kernel,
        grid=(n // tm,),
        in_specs=[row(SWA_WIDTH), row(SB_WIDTH), row(d), row(d), row(d), full(p_a), full(p_b), full(w_o)],
        out_specs=row(d),
        out_shape=jax.ShapeDtypeStruct((n, d), F32),
        compiler_params=_params("parallel"),
        name="merge",
    )(y_a, y_b, g_a, g_b, x2, p_a, p_b, w_o)


META_EXPERT = 0
META_RANK = TOP_K
EXT_EXPERT = 0
EXT_GATE = TOP_K
GATE_PARTS = 3


def _router_kernel(h_ref, nw_ref, rw_ref, rb_ref, hx_ref, meta_ref, trun_ref, cnt_ref, run_ref):
    tm, d = h_ref.shape

    @pl.when(pl.program_id(0) == 0)
    def _():
        run_ref[...] = jnp.zeros_like(run_ref)

    hn = _rms(h_ref[...], nw_ref[...])
    logits = jnp.dot(hn, rw_ref[...], preferred_element_type=F32,
                     precision=lax.Precision.HIGHEST) + rb_ref[...]
    lane = lax.broadcasted_iota(jnp.int32, (tm, N_EXPERTS), 1).astype(F32)
    work = logits
    tops, idxs, hots = [], [], []
    for _ in range(TOP_K):
        m = jnp.max(work, axis=-1, keepdims=True)
        idx = jnp.min(jnp.where(work == m, lane, float(N_EXPERTS)), axis=-1, keepdims=True)
        hot = lane == idx
        work = jnp.where(hot, -jnp.inf, work)
        tops.append(m)
        idxs.append(idx)
        hots.append(hot)
    exps = [jnp.exp(t - tops[0]) for t in tops]
    den = exps[0] + exps[1] + exps[2] + exps[3]
    chosen = jnp.where(hots[0] | hots[1] | hots[2] | hots[3], 1.0, 0.0)
    r = lax.broadcasted_iota(jnp.int32, (tm, tm), 0)
    c = lax.broadcasted_iota(jnp.int32, (tm, tm), 1)
    before = jnp.where(c < r, 1.0, 0.0).astype(BF16)
    run = run_ref[...]
    prefix = _dot(before, chosen.astype(BF16)) + run
    mlane = lax.broadcasted_iota(jnp.int32, (tm, LANES), 1)
    meta = jnp.zeros((tm, LANES), F32)
    ext = jnp.zeros((tm, LANES), F32)
    for k in range(TOP_K):
        rank = jnp.sum(jnp.where(hots[k], prefix, 0.0), axis=-1, keepdims=True)
        meta = jnp.where(mlane == META_EXPERT + k, idxs[k], meta)
        meta = jnp.where(mlane == META_RANK + k, rank, meta)
        ext = jnp.where(mlane == EXT_EXPERT + k, idxs[k], ext)
        rest = exps[k] / den
        for part in range(GATE_PARTS):
            piece = rest.astype(BF16).astype(F32)
            ext = jnp.where(mlane == EXT_GATE + GATE_PARTS * k + part, piece, ext)
            rest = rest - piece
    meta_ref[...] = meta
    hx_ref[:, :d] = hn.astype(BF16)
    hx_ref[:, d:] = ext.astype(BF16)
    trun_ref[0] = run
    tile_cnt = jnp.sum(chosen, axis=0, keepdims=True)
    run = run + jnp.ceil(tile_cnt / SUBLANES) * SUBLANES
    run_ref[...] = run
    cnt_ref[...] = run


def _router(h1, norm_w, router_w, router_b):
    n, d = h1.shape
    tm = TOKEN_TILE
    tiles = n // tm
    return pl.pallas_call(
        _router_kernel,
        grid=(tiles,),
        in_specs=[pl.BlockSpec((tm, d), lambda i: (i, 0)),
                  pl.BlockSpec((1, d), lambda i: (0, 0)),
                  pl.BlockSpec((d, N_EXPERTS), lambda i: (0, 0)),
                  pl.BlockSpec((1, N_EXPERTS), lambda i: (0, 0))],
        out_specs=[pl.BlockSpec((tm, d + LANES), lambda i: (i, 0)),
                   pl.BlockSpec((tm, LANES), lambda i: (i, 0)),
                   pl.BlockSpec((1, 1, N_EXPERTS), lambda i: (i, 0, 0)),
                   pl.BlockSpec((1, N_EXPERTS), lambda i: (0, 0))],
        out_shape=[jax.ShapeDtypeStruct((n, d + LANES), BF16),
                   jax.ShapeDtypeStruct((n, LANES), F32),
                   jax.ShapeDtypeStruct((tiles, 1, N_EXPERTS), F32),
                   jax.ShapeDtypeStruct((1, N_EXPERTS), F32)],
        scratch_shapes=[pltpu.VMEM((1, N_EXPERTS), F32)],
        compiler_params=_params("arbitrary"),
        name="router",
    )(h1, norm_w.reshape(1, d), router_w, router_b.reshape(1, N_EXPERTS))


def _routing_tables(tile_run, counts, n):
    tiles = tile_run.shape[0]
    counts = counts.reshape(N_EXPERTS).astype(jnp.int32)
    run = tile_run.reshape(tiles, N_EXPERTS).astype(jnp.int32)
    tile_len = jnp.concatenate([run[1:], counts[None]], axis=0) - run
    nch = (tile_len + CHUNK - 1) // CHUNK
    loc = (jnp.cumsum(nch, axis=1) - nch) * CHUNK
    padded = (counts + CHUNK - 1 + FFN_BLOCK - 1) // FFN_BLOCK * FFN_BLOCK
    seg_end = jnp.cumsum(padded)
    seg_start = seg_end - padded
    dst0 = seg_start[None, :] + run
    delta = (loc - run).astype(F32).reshape(tiles, 1, N_EXPERTS)
    zero0 = seg_start + counts
    nzero = (padded - counts) // SUBLANES
    n_real = seg_end[-1] // FFN_BLOCK
    max_blocks = (n * TOP_K + tiles * N_EXPERTS * (SUBLANES - 1)
                  + N_EXPERTS * (CHUNK - 1 + FFN_BLOCK - 1)) // FFN_BLOCK
    blk_exp = jnp.minimum(jnp.sum(jnp.arange(max_blocks, dtype=jnp.int32)[:, None] * FFN_BLOCK
                                  >= seg_end[None, :], axis=1), N_EXPERTS - 1).astype(jnp.int32)
    return dict(dst0=dst0.reshape(-1), nch=nch.reshape(-1).astype(jnp.int32), loc=loc.reshape(-1),
                delta=delta, zero0=zero0, nzero=nzero, blk_exp=blk_exp,
                n_real=n_real.reshape(1).astype(jnp.int32), cap=max_blocks * FFN_BLOCK)


def _tile_rows(meta_ref, delta_ref):
    meta = meta_ref[...]
    tm = meta.shape[0]
    delta = delta_ref[0]
    lane = lax.broadcasted_iota(jnp.int32, (tm, N_EXPERTS), 1).astype(F32)
    rows = []
    for k in range(TOP_K):
        expert = meta[:, META_EXPERT + k:META_EXPERT + k + 1]
        rank = meta[:, META_RANK + k:META_RANK + k + 1]
        rows.append(rank + jnp.sum(jnp.where(lane == expert, delta, 0.0), axis=-1, keepdims=True))
    return rows


def _chunk_copies(t, tbl_ref, nch_ref, loc_ref, make_copy, op):
    for e in range(N_EXPERTS):
        idx = t * N_EXPERTS + e
        seg_row = tbl_ref[idx]
        loc_row = loc_ref[idx]

        def body(c, carry):
            cp = make_copy(pl.multiple_of(seg_row + c * CHUNK, SUBLANES),
                           pl.multiple_of(loc_row + c * CHUNK, CHUNK))
            cp.start() if op == "start" else cp.wait()
            return carry

        lax.fori_loop(0, nch_ref[idx], body, 0)


def _dispatch_kernel(dst0_ref, nch_ref, loc_ref, zero0_ref, nzero_ref, nreal_ref,
                     hx_ref, meta_ref, delta_ref, xb_ref, srt_ref, zero_ref, sem):
    t = pl.program_id(0)
    tm = hx_ref.shape[0]
    rows = _tile_rows(meta_ref, delta_ref)
    mlane = lax.broadcasted_iota(jnp.int32, (tm, LANES), 1)
    packed = jnp.full((tm, LANES), -1.0, F32)
    for k in range(TOP_K):
        packed = jnp.where(mlane == k, rows[k], packed)
    rows_t = packed.T
    prow = lax.broadcasted_iota(jnp.int32, (SORTED_ROWS, tm), 0).astype(F32)
    hit = prow == rows_t[0:1, :]
    for k in range(1, TOP_K):
        hit = hit | (prow == rows_t[k:k + 1, :])
    onehot = jnp.where(hit, 1.0, 0.0).astype(BF16)
    srt_ref[...] = _dot(onehot, hx_ref[...])

    def copy(seg_row, loc_row):
        return pltpu.make_async_copy(srt_ref.at[pl.ds(loc_row, CHUNK)], xb_ref.at[pl.ds(seg_row, CHUNK)], sem)

    _chunk_copies(t, dst0_ref, nch_ref, loc_ref, copy, "start")
    _chunk_copies(t, dst0_ref, nch_ref, loc_ref, copy, "wait")

    @pl.when(t == pl.num_programs(0) - 1)
    def _():
        zero_ref[...] = jnp.zeros_like(zero_ref)
        for op in ("start", "wait"):
            for e in range(N_EXPERTS):
                def body(c, carry):
                    row0 = pl.multiple_of(zero0_ref[e] + c * SUBLANES, SUBLANES)
                    cp = pltpu.make_async_copy(zero_ref.at[pl.ds(0, SUBLANES)],
                                               xb_ref.at[pl.ds(row0, SUBLANES)], sem)
                    cp.start() if op == "start" else cp.wait()
                    return carry

                lax.fori_loop(0, nzero_ref[e], body, 0)

            def tail(b, carry):
                row0 = pl.multiple_of(b * FFN_BLOCK, FFN_BLOCK)
                cp = pltpu.make_async_copy(zero_ref, xb_ref.at[pl.ds(row0, FFN_BLOCK)], sem)
                cp.start() if op == "start" else cp.wait()
                return carry

            lax.fori_loop(nreal_ref[0], xb_ref.shape[0] // FFN_BLOCK, tail, 0)


def _dispatch(hx, meta, tbl):
    n, width = hx.shape
    tm = TOKEN_TILE
    grid_spec = pltpu.PrefetchScalarGridSpec(
        num_scalar_prefetch=6,
        grid=(n // tm,),
        in_specs=[pl.BlockSpec((tm, width), lambda i, *_: (i, 0)),
                  pl.BlockSpec((tm, LANES), lambda i, *_: (i, 0)),
                  pl.BlockSpec((1, 1, N_EXPERTS), lambda i, *_: (i, 0, 0))],
        out_specs=pl.BlockSpec(memory_space=pl.ANY),
        scratch_shapes=[pltpu.VMEM((SORTED_ROWS, width), F32),
                        pltpu.VMEM((FFN_BLOCK, width), F32),
                        pltpu.SemaphoreType.DMA(())],
    )
    return pl.pallas_call(
        _dispatch_kernel,
        grid_spec=grid_spec,
        out_shape=jax.ShapeDtypeStruct((tbl["cap"], width), F32),
        compiler_params=_params("arbitrary"),
        name="dispatch",
    )(tbl["dst0"], tbl["nch"], tbl["loc"], tbl["zero0"], tbl["nzero"], tbl["n_real"],
      hx, meta, tbl["delta"])


def _ffn_kernel(exp_ref, nreal_ref, x_ref, wgu_ref, bgu_ref, wd_ref, bd_ref, y_ref):
    b = pl.program_id(0)
    d_ff, d = wd_ref.shape[1], wd_ref.shape[2]

    @pl.when(b >= nreal_ref[0])
    def _():
        y_ref[...] = jnp.zeros_like(y_ref)

    @pl.when(b < nreal_ref[0])
    def _():
        x = x_ref[...]
        ext = x[:, d:]
        expert = exp_ref[b].astype(F32)
        gate = jnp.zeros((x.shape[0], 1), F32)
        for k in range(TOP_K):
            g0 = EXT_GATE + GATE_PARTS * k
            g = ext[:, g0:g0 + 1] + ext[:, g0 + 1:g0 + 2] + ext[:, g0 + 2:g0 + 3]
            gate = gate + jnp.where(ext[:, EXT_EXPERT + k:EXT_EXPERT + k + 1] == expert, g, 0.0)
        gu = _dot(x[:, :d].astype(BF16), wgu_ref[0]) + bgu_ref[0]
        g = jnp.minimum(gu[:, :d_ff], SWIGLU_LIMIT)
        u = jnp.clip(gu[:, d_ff:], -SWIGLU_LIMIT, SWIGLU_LIMIT)
        act = (u + 1.0) * (g * _sigmoid(SWIGLU_ALPHA * g))
        y_ref[...] = gate * (_dot(act.astype(BF16), wd_ref[0]) + bd_ref[0])


def _ffn(xb, tbl, w_gate_up, b_gate_up, w_down, b_down):
    cap, width = xb.shape
    e, d, two_ff = w_gate_up.shape
    d_ff = two_ff // 2
    grid_spec = pltpu.PrefetchScalarGridSpec(
        num_scalar_prefetch=2,
        grid=(cap // FFN_BLOCK,),
        in_specs=[pl.BlockSpec((FFN_BLOCK, width), lambda b, ex, nr: (b, 0)),
                  pl.BlockSpec((1, d, two_ff), lambda b, ex, nr: (ex[b], 0, 0)),
                  pl.BlockSpec((1, 1, two_ff), lambda b, ex, nr: (ex[b], 0, 0)),
                  pl.BlockSpec((1, d_ff, d), lambda b, ex, nr: (ex[b], 0, 0)),
                  pl.BlockSpec((1, 1, d), lambda b, ex, nr: (ex[b], 0, 0))],
        out_specs=pl.BlockSpec((FFN_BLOCK, d), lambda b, ex, nr: (b, 0)),
    )
    return pl.pallas_call(
        _ffn_kernel,
        grid_spec=grid_spec,
        out_shape=jax.ShapeDtypeStruct((cap, d), F32),
        compiler_params=_params("arbitrary"),
        name="expert_ffn",
    )(tbl["blk_exp"], tbl["n_real"], xb, w_gate_up,
      b_gate_up.reshape(e, 1, two_ff), w_down, b_down.reshape(e, 1, d))


def _combine_kernel(dst0_ref, nch_ref, loc_ref, meta_ref, delta_ref, h_ref, y_ref, o_ref, srt_ref, sem):
    t = pl.program_id(0)
    tm = h_ref.shape[0]

    @pl.when(t == 0)
    def _():
        srt_ref[...] = jnp.zeros_like(srt_ref)

    def copy(seg_row, loc_row):
        return pltpu.make_async_copy(y_ref.at[pl.ds(seg_row, CHUNK)], srt_ref.at[pl.ds(loc_row, CHUNK)], sem)

    _chunk_copies(t, dst0_ref, nch_ref, loc_ref, copy, "start")
    rows = _tile_rows(meta_ref, delta_ref)
    pcol = lax.broadcasted_iota(jnp.int32, (tm, SORTED_ROWS), 1).astype(F32)
    hit = pcol == rows[0]
    for k in range(1, TOP_K):
        hit = hit | (pcol == rows[k])
    onehot = jnp.where(hit, 1.0, 0.0).astype(BF16)
    _chunk_copies(t, dst0_ref, nch_ref, loc_ref, copy, "wait")
    o_ref[...] = h_ref[...] + _dot(onehot, srt_ref[...].astype(BF16))


def _combine(meta, tbl, h1, yb):
    n, d = h1.shape
    tm = TOKEN_TILE
    grid_spec = pltpu.PrefetchScalarGridSpec(
        num_scalar_prefetch=3,
        grid=(n // tm,),
        in_specs=[pl.BlockSpec((tm, LANES), lambda i, *_: (i, 0)),
                  pl.BlockSpec((1, 1, N_EXPERTS), lambda i, *_: (i, 0, 0)),
                  pl.BlockSpec((tm, d), lambda i, *_: (i, 0)),
                  pl.BlockSpec(memory_space=pl.ANY)],
        out_specs=pl.BlockSpec((tm, d), lambda i, *_: (i, 0)),
        scratch_shapes=[pltpu.VMEM((SORTED_ROWS, d), F32), pltpu.SemaphoreType.DMA(())],
    )
    return pl.pallas_call(
        _combine_kernel,
        grid_spec=grid_spec,
        out_shape=jax.ShapeDtypeStruct((n, d), F32),
        compiler_params=_params("arbitrary"),
        name="combine",
    )(tbl["dst0"], tbl["nch"], tbl["loc"], meta, tbl["delta"], h1, yb)


def kernel(x, attn_norm_w, w_in, q_norm_w, k_norm_w, sinks, w_proj_swa, w_proj_sb, w_out,
           ffn_norm_w, router_w, router_b, w_gate_up, b_gate_up, w_down, b_down):
    batch, seq, d = x.shape
    n = batch * seq
    h = x.reshape(n, d)
    for layer in range(attn_norm_w.shape[0]):
        q_a, kv_a, q_b, k_b, v_b, g_a, g_b = _in_proj(h, attn_norm_w[layer], w_in[layer].astype(BF16))
        y_a = _swa(q_a, kv_a, q_norm_w[layer], k_norm_w[layer], sinks[layer], batch, seq)
        y_b = _sb(q_b, k_b, v_b, batch, seq)
        h1 = _merge(y_a, y_b, g_a, g_b, h, w_proj_swa[layer].astype(BF16),
                    w_proj_sb[layer].astype(BF16), w_out[layer].astype(BF16))
        hx, meta, tile_run, counts = _router(h1, ffn_norm_w[layer], router_w[layer], router_b[layer])
        tbl = _routing_tables(tile_run, counts, n)
        xb = _dispatch(hx, meta, tbl)
        yb = _ffn(xb, tbl, w_gate_up[layer].astype(BF16), b_gate_up[layer],
                  w_down[layer].astype(BF16), b_down[layer])
        h = _combine(meta, tbl, h1, yb)
    return h.reshape(batch, seq, d)
```

```python
import jax
import jax.numpy as jnp
from jax import lax
from jax.experimental import pallas as pl
from jax.experimental.pallas import tpu as pltpu

HEAD_DIM = 64
SWA_HEADS = 8
SWA_KV_HEADS = 2
SWA_GROUP = SWA_HEADS // SWA_KV_HEADS
BLOCK = 128
SB_HEADS = 8
SWA_WIDTH = SWA_HEADS * HEAD_DIM
SWA_KV_WIDTH = SWA_KV_HEADS * HEAD_DIM
SB_WIDTH = SB_HEADS * HEAD_DIM
N_EXPERTS = 32
TOP_K = 4
SWIGLU_LIMIT = 7.0
SWIGLU_ALPHA = 1.702
NORM_EPS = 1e-5
ATTN_SCALE = HEAD_DIM ** -0.5
SIGN_BIT = -2 ** 31

LANES = 128
SUBLANES = 8
TOKEN_TILE = 512
FFN_BLOCK = 256
CHUNK = 32
SORTED_ROWS = TOKEN_TILE * TOP_K + N_EXPERTS * CHUNK
SB_Q_ROWS = 512
SB_K_ROWS = 256
VMEM_LIMIT = 56 * 1024 * 1024

F32 = jnp.float32
BF16 = jnp.bfloat16
NEG_BIG = -1e30


def _dot(a, b):
    return jnp.dot(a, b, preferred_element_type=F32)


def _dot_nt(a, b):
    return lax.dot_general(a, b, (((1,), (1,)), ((), ())), preferred_element_type=F32)


def _sigmoid(x):
    return jax.nn.sigmoid(x)


def _rms(x, w):
    return x * lax.rsqrt(jnp.mean(x * x, axis=-1, keepdims=True) + NORM_EPS) * w


def _params(*sem):
    return pltpu.CompilerParams(dimension_semantics=sem, vmem_limit_bytes=VMEM_LIMIT)


def _in_proj_kernel(x_ref, nw_ref, w_ref, qa_ref, kva_ref, qb_ref, kb_ref, vb_ref, ga_ref, gb_ref):
    xn = _rms(x_ref[...], nw_ref[...]).astype(BF16)
    off = 0
    for ref, gate in ((qa_ref, False), (kva_ref, False), (qb_ref, False), (kb_ref, False),
                      (vb_ref, False), (ga_ref, True), (gb_ref, True)):
        width = ref.shape[1]
        y = _dot(xn, w_ref[:, off:off + width])
        if gate:
            y = _sigmoid(y)
        ref[...] = y.astype(ref.dtype)
        off += width


def _in_proj(x2, norm_w, w_in):
    n, d = x2.shape
    widths = (SWA_WIDTH, 2 * SWA_KV_WIDTH, SB_WIDTH, SB_WIDTH, SB_WIDTH, d, d)
    tm = TOKEN_TILE
    return pl.pallas_call(
        _in_proj_kernel,
        grid=(n // tm,),
        in_specs=[pl.BlockSpec((tm, d), lambda i: (i, 0)),
                  pl.BlockSpec((1, d), lambda i: (0, 0)),
                  pl.BlockSpec(w_in.shape, lambda i: (0, 0))],
        out_specs=[pl.BlockSpec((tm, w), lambda i: (i, 0)) for w in widths],
        out_shape=[jax.ShapeDtypeStruct((n, w), BF16) for w in widths],
        compiler_params=_params("parallel"),
        name="in_proj",
    )(x2, norm_w.reshape(1, d), w_in)


def _swa_kernel(q_ref, kvc_ref, kvp_ref, qnw_ref, knw_ref, sink_ref, o_ref):
    i = pl.program_id(1)
    half = HEAD_DIM

    def group_mean_sq(x):
        w = x.shape[1]
        same_head = (lax.broadcasted_iota(jnp.int32, (w, w), 0) // half
                     == lax.broadcasted_iota(jnp.int32, (w, w), 1) // half)
        avg = jnp.where(same_head, 1.0 / half, 0.0).astype(BF16)
        hi, lo = _split_bf16(x * x)
        return _dot(hi, avg) + _dot(lo, avg)

    q = q_ref[...].astype(F32)
    kv = jnp.concatenate([kvp_ref[...], kvc_ref[...]], axis=0)
    k = kv[:, :SWA_KV_WIDTH].astype(F32)
    v = kv[:, SWA_KV_WIDTH:].astype(F32)
    qn = (q * lax.rsqrt(group_mean_sq(q) + NORM_EPS) * qnw_ref[...] * ATTN_SCALE).astype(BF16)
    kn = k * lax.rsqrt(group_mean_sq(k) + NORM_EPS) * knw_ref[...]
    lane = lax.broadcasted_iota(jnp.int32, (2 * BLOCK, LANES), 1)
    low = lane < half
    k_swap = pltpu.roll(kn, half, 1)
    v_swap = pltpu.roll(v, half, 1).astype(BF16)
    v_same = v.astype(BF16)
    keys = [[jnp.where(low, kn, 0.0).astype(BF16), jnp.where(low, 0.0, k_swap).astype(BF16)],
            [jnp.where(low, k_swap, 0.0).astype(BF16), jnp.where(low, 0.0, kn).astype(BF16)]]
    vals = [[v_same, v_swap], [v_swap, v_same]]

    row = lax.broadcasted_iota(jnp.int32, (BLOCK, 2 * BLOCK), 0)
    col = lax.broadcasted_iota(jnp.int32, (BLOCK, 2 * BLOCK), 1)
    rel = row + BLOCK - col
    valid = (rel >= 0) & (rel < BLOCK) & ((col >= BLOCK) | (i > 0))
    rel = rel.astype(F32)
    out_lane_low = lax.broadcasted_iota(jnp.int32, (BLOCK, LANES), 1) < half
    for blk in range(SWA_HEADS // 2):
        qb = qn[:, blk * LANES:(blk + 1) * LANES]
        halves = []
        for par in range(2):
            h = 2 * blk + par
            g = h // SWA_GROUP
            slope = 2.0 ** (-8.0 * (h + 1) / SWA_HEADS)
            s = jnp.where(valid, _dot_nt(qb, keys[g][par]) - slope * rel, NEG_BIG)
            sink = sink_ref[h]
            m = jnp.maximum(jnp.max(s, axis=-1, keepdims=True), sink)
            p = jnp.exp(s - m)
            den = jnp.sum(p, axis=-1, keepdims=True) + jnp.exp(sink - m)
            halves.append(_dot(p.astype(BF16), vals[g][par]) / den)
        o_ref[:, blk * LANES:(blk + 1) * LANES] = jnp.where(out_lane_low, halves[0], halves[1]).astype(o_ref.dtype)


def _swa(q_a, kv_a, q_norm_w, k_norm_w, sinks, batch, seq):
    n = q_a.shape[0]
    nb = seq // BLOCK
    return pl.pallas_call(
        _swa_kernel,
        grid=(batch, nb),
        in_specs=[pl.BlockSpec((BLOCK, SWA_WIDTH), lambda b, i: (b * nb + i, 0)),
                  pl.BlockSpec((BLOCK, 2 * SWA_KV_WIDTH), lambda b, i: (b * nb + i, 0)),
                  pl.BlockSpec((BLOCK, 2 * SWA_KV_WIDTH), lambda b, i: (b * nb + jnp.maximum(i - 1, 0), 0)),
                  pl.BlockSpec((1, SWA_WIDTH), lambda b, i: (0, 0)),
                  pl.BlockSpec((1, SWA_KV_WIDTH), lambda b, i: (0, 0)),
                  pl.BlockSpec(memory_space=pltpu.SMEM)],
        out_specs=pl.BlockSpec((BLOCK, SWA_WIDTH), lambda b, i: (b * nb + i, 0)),
        out_shape=jax.ShapeDtypeStruct((n, SWA_WIDTH), BF16),
        compiler_params=_params("parallel", "parallel"),
        name="swa",
    )(q_a, kv_a, kv_a, jnp.tile(q_norm_w, SWA_HEADS).reshape(1, SWA_WIDTH),
      jnp.tile(k_norm_w, SWA_KV_HEADS).reshape(1, SWA_KV_WIDTH), sinks)


def _split_bf16(x):
    hi = x.astype(BF16)
    lo = (x - hi.astype(F32)).astype(BF16)
    return hi, lo


def _sb_kernel(q_ref, k_ref, v_ref, o_ref):
    seq = q_ref.shape[0]
    tq, tk = SB_Q_ROWS, SB_K_ROWS
    nq = seq // tq
    k_per_q = tq // tk
    lane = lax.broadcasted_iota(jnp.int32, (tq, LANES), 1)
    row = lax.broadcasted_iota(jnp.int32, (tq, tk), 0)
    col = lax.broadcasted_iota(jnp.int32, (tq, tk), 1)
    neg_incl = jnp.where(lax.broadcasted_iota(jnp.int32, (tk, tk), 0)
                         >= lax.broadcasted_iota(jnp.int32, (tk, tk), 1), -1.0, 0.0).astype(BF16)

    def softplus(z):
        neg_abs = lax.bitcast_convert_type(lax.bitcast_convert_type(z, jnp.int32) | SIGN_BIT, F32)
        return jnp.maximum(z, 0.0) + jnp.log(1.0 + jnp.exp(neg_abs))

    def neg_suffix_sum(sp):
        return _dot(sp.astype(BF16), neg_incl)

    def k_span(k0s, qms, state, masks):
        pairs = [(i, p) for i in range(len(k0s)) for p in range(2)]
        kbs = [k_ref[pl.ds(k0, tk), :] for k0 in k0s]
        vbs = [v_ref[pl.ds(k0, tk), :] for k0 in k0s]
        zs = {(i, p): _dot_nt(qms[p], kbs[i]) for i, p in pairs}
        sps = {ip: softplus(zs[ip]) for ip in pairs}
        for (i, p) in pairs:
            if masks[i] is not None:
                sps[(i, p)] = jnp.where(masks[i], sps[(i, p)], 0.0)
        cs = {ip: neg_suffix_sum(sps[ip]) for ip in pairs}
        accs, runs = list(state[:2]), list(state[2:])
        ws = {}
        for (i, p) in pairs:
            w = jnp.exp(zs[(i, p)] + cs[(i, p)] + runs[p])
            ws[(i, p)] = w if masks[i] is None else jnp.where(masks[i], w, 0.0)
            runs[p] = runs[p] + cs[(i, p)][:, 0:1]
        for (i, p) in pairs:
            accs[p] = accs[p] + _dot(ws[(i, p)].astype(BF16), vbs[i])
        return (*accs, *runs)

    def q_block(qi, carry):
        q0 = pl.multiple_of(qi * tq, tq)
        q2 = q_ref[pl.ds(q0, tq), :] * ATTN_SCALE
        qms = (jnp.where(lane < HEAD_DIM, q2, 0.0).astype(BF16),
               jnp.where(lane >= HEAD_DIM, q2, 0.0).astype(BF16))
        state = (jnp.zeros((tq, LANES), F32), jnp.zeros((tq, LANES), F32),
                 jnp.zeros((tq, 1), F32), jnp.zeros((tq, 1), F32))
        order = list(reversed(range(k_per_q)))
        state = k_span([q0 + d * tk for d in order], qms, state, [(col + d * tk) < row for d in order])

        def earlier(jj, state):
            k0s = [pl.multiple_of((qi - 1 - jj) * tq + d * tk, tk) for d in order]
            return k_span(k0s, qms, state, [None] * k_per_q)

        state = lax.fori_loop(0, qi, earlier, state)
        o_ref[pl.ds(q0, tq), :] = jnp.where(lane < HEAD_DIM, state[0], state[1]).astype(o_ref.dtype)
        return carry

    lax.fori_loop(0, nq, q_block, 0)


def _sb(q_b, k_b, v_b, batch, seq):
    n = q_b.shape[0]
    spec = pl.BlockSpec((seq, LANES), lambda b, p: (b, p))
    return pl.pallas_call(
        _sb_kernel,
        grid=(batch, SB_WIDTH // LANES),
        in_specs=[spec, spec, spec],
        out_specs=spec,
        out_shape=jax.ShapeDtypeStruct((n, SB_WIDTH), BF16),
        compiler_params=_params("parallel", "parallel"),
        name="stick_breaking",
    )(q_b, k_b, v_b)


def _merge_kernel(ya_ref, yb_ref, ga_ref, gb_ref, x_ref, pa_ref, pb_ref, wo_ref, h_ref):
    m = (ga_ref[...].astype(F32) * _dot(ya_ref[...], pa_ref[...])
         + gb_ref[...].astype(F32) * _dot(yb_ref[...], pb_ref[...]))
    h_ref[...] = x_ref[...] + _dot(m.astype(BF16), wo_ref[...])


def _merge(y_a, y_b, g_a, g_b, x2, p_a, p_b, w_o):
    n, d = x2.shape
    tm = TOKEN_TILE
    row = lambda w: pl.BlockSpec((tm, w), lambda i: (i, 0))
    full = lambda a: pl.BlockSpec(a.shape, lambda i: (0, 0))
    return pl.pallas_call(
        _merge_kernel,
        grid=(n // tm,),
        in_specs=[row(SWA_WIDTH), row(SB_WIDTH), row(d), row(d), row(d), full(p_a), full(p_b), full(w_o)],
        out_specs=row(d),
        out_shape=jax.ShapeDtypeStruct((n, d), F32),
        compiler_params=_params("parallel"),
        name="merge",
    )(y_a, y_b, g_a, g_b, x2, p_a, p_b, w_o)


META_EXPERT = 0
META_RANK = TOP_K
EXT_EXPERT = 0
EXT_GATE = TOP_K
GATE_PARTS = 3


def _router_kernel(h_ref, nw_ref, rw_ref, rb_ref, hx_ref, meta_ref, trun_ref, cnt_ref, run_ref):
    tm, d = h_ref.shape

    @pl.when(pl.program_id(0) == 0)
    def _():
        run_ref[...] = jnp.zeros_like(run_ref)

    hn = _rms(h_ref[...], nw_ref[...])
    logits = jnp.dot(hn, rw_ref[...], preferred_element_type=F32,
                     precision=lax.Precision.HIGHEST) + rb_ref[...]
    lane = lax.broadcasted_iota(jnp.int32, (tm, N_EXPERTS), 1).astype(F32)
    work = logits
    tops, idxs, hots = [], [], []
    for _ in range(TOP_K):
        m = jnp.max(work, axis=-1, keepdims=True)
        idx = jnp.min(jnp.where(work == m, lane, float(N_EXPERTS)), axis=-1, keepdims=True)
        hot = lane == idx
        work = jnp.where(hot, -jnp.inf, work)
        tops.append(m)
        idxs.append(idx)
        hots.append(hot)
    exps = [jnp.exp(t - tops[0]) for t in tops]
    den = exps[0] + exps[1] + exps[2] + exps[3]
    chosen = jnp.where(hots[0] | hots[1] | hots[2] | hots[3], 1.0, 0.0)
    r = lax.broadcasted_iota(jnp.int32, (tm, tm), 0)
    c = lax.broadcasted_iota(jnp.int32, (tm, tm), 1)
    before = jnp.where(c < r, 1.0, 0.0).astype(BF16)
    run = run_ref[...]
    prefix = _dot(before, chosen.astype(BF16)) + run
    mlane = lax.broadcasted_iota(jnp.int32, (tm, LANES), 1)
    meta = jnp.zeros((tm, LANES), F32)
    ext = jnp.zeros((tm, LANES), F32)
    for k in range(TOP_K):
        rank = jnp.sum(jnp.where(hots[k], prefix, 0.0), axis=-1, keepdims=True)
        meta = jnp.where(mlane == META_EXPERT + k, idxs[k], meta)
        meta = jnp.where(mlane == META_RANK + k, rank, meta)
        ext = jnp.where(mlane == EXT_EXPERT + k, idxs[k], ext)
        rest = exps[k] / den
        for part in range(GATE_PARTS):
            piece = rest.astype(BF16).astype(F32)
            ext = jnp.where(mlane == EXT_GATE + GATE_PARTS * k + part, piece, ext)
            rest = rest - piece
    meta_ref[...] = meta
    hx_ref[:, :d] = hn.astype(BF16)
    hx_ref[:, d:] = ext.astype(BF16)
    trun_ref[0] = run
    tile_cnt = jnp.sum(chosen, axis=0, keepdims=True)
    run = run + jnp.ceil(tile_cnt / SUBLANES) * SUBLANES
    run_ref[...] = run
    cnt_ref[...] = run


def _router(h1, norm_w, router_w, router_b):
    n, d = h1.shape
    tm = TOKEN_TILE
    tiles = n // tm
    return pl.pallas_call(
        _router_kernel,
        grid=(tiles,),
        in_specs=[pl.BlockSpec((tm, d), lambda i: (i, 0)),
                  pl.BlockSpec((1, d), lambda i: (0, 0)),
                  pl.BlockSpec((d, N_EXPERTS), lambda i: (0, 0)),
                  pl.BlockSpec((1, N_EXPERTS), lambda i: (0, 0))],
        out_specs=[pl.BlockSpec((tm, d + LANES), lambda i: (i, 0)),
                   pl.BlockSpec((tm, LANES), lambda i: (i, 0)),
                   pl.BlockSpec((1, 1, N_EXPERTS), lambda i: (i, 0, 0)),
                   pl.BlockSpec((1, N_EXPERTS), lambda i: (0, 0))],
        out_shape=[jax.ShapeDtypeStruct((n, d + LANES), BF16),
                   jax.ShapeDtypeStruct((n, LANES), F32),
                   jax.ShapeDtypeStruct((tiles, 1, N_EXPERTS), F32),
                   jax.ShapeDtypeStruct((1, N_EXPERTS), F32)],
        scratch_shapes=[pltpu.VMEM((1, N_EXPERTS), F32)],
        compiler_params=_params("arbitrary"),
        name="router",
    )(h1, norm_w.reshape(1, d), router_w, router_b.reshape(1, N_EXPERTS))


def _routing_tables(tile_run, counts, n):
    tiles = tile_run.shape[0]
    counts = counts.reshape(N_EXPERTS).astype(jnp.int32)
    run = tile_run.reshape(tiles, N_EXPERTS).astype(jnp.int32)
    tile_len = jnp.concatenate([run[1:], counts[None]], axis=0) - run
    nch = (tile_len + CHUNK - 1) // CHUNK
    loc = (jnp.cumsum(nch, axis=1) - nch) * CHUNK
    padded = (counts + CHUNK - 1 + FFN_BLOCK - 1) // FFN_BLOCK * FFN_BLOCK
    seg_end = jnp.cumsum(padded)
    seg_start = seg_end - padded
    dst0 = seg_start[None, :] + run
    delta = (loc - run).astype(F32).reshape(tiles, 1, N_EXPERTS)
    zero0 = seg_start + counts
    nzero = (padded - counts) // SUBLANES
    n_real = seg_end[-1] // FFN_BLOCK
    max_blocks = (n * TOP_K + tiles * N_EXPERTS * (SUBLANES - 1)
                  + N_EXPERTS * (CHUNK - 1 + FFN_BLOCK - 1)) // FFN_BLOCK
    blk_exp = jnp.minimum(jnp.sum(jnp.arange(max_blocks, dtype=jnp.int32)[:, None] * FFN_BLOCK
                                  >= seg_end[None, :], axis=1), N_EXPERTS - 1).astype(jnp.int32)
    return dict(dst0=dst0.reshape(-1), nch=nch.reshape(-1).astype(jnp.int32), loc=loc.reshape(-1),
                delta=delta, zero0=zero0, nzero=nzero, blk_exp=blk_exp,
                n_real=n_real.reshape(1).astype(jnp.int32), cap=max_blocks * FFN_BLOCK)


def _tile_rows(meta_ref, delta_ref):
    meta = meta_ref[...]
    tm = meta.shape[0]
    delta = delta_ref[0]
    lane = lax.broadcasted_iota(jnp.int32, (tm, N_EXPERTS), 1).astype(F32)
    rows = []
    for k in range(TOP_K):
        expert = meta[:, META_EXPERT + k:META_EXPERT + k + 1]
        rank = meta[:, META_RANK + k:META_RANK + k + 1]
        rows.append(rank + jnp.sum(jnp.where(lane == expert, delta, 0.0), axis=-1, keepdims=True))
    return rows


def _chunk_copies(t, tbl_ref, nch_ref, loc_ref, make_copy, op):
    for e in range(N_EXPERTS):
        idx = t * N_EXPERTS + e
        seg_row = tbl_ref[idx]
        loc_row = loc_ref[idx]

        def body(c, carry):
            cp = make_copy(pl.multiple_of(seg_row + c * CHUNK, SUBLANES),
                           pl.multiple_of(loc_row + c * CHUNK, CHUNK))
            cp.start() if op == "start" else cp.wait()
            return carry

        lax.fori_loop(0, nch_ref[idx], body, 0)


def _dispatch_kernel(dst0_ref, nch_ref, loc_ref, zero0_ref, nzero_ref, nreal_ref,
                     hx_ref, meta_ref, delta_ref, xb_ref, srt_ref, zero_ref, sem):
    t = pl.program_id(0)
    tm = hx_ref.shape[0]
    rows = _tile_rows(meta_ref, delta_ref)
    mlane = lax.broadcasted_iota(jnp.int32, (tm, LANES), 1)
    packed = jnp.full((tm, LANES), -1.0, F32)
    for k in range(TOP_K):
        packed = jnp.where(mlane == k, rows[k], packed)
    rows_t = packed.T
    prow = lax.broadcasted_iota(jnp.int32, (SORTED_ROWS, tm), 0).astype(F32)
    hit = prow == rows_t[0:1, :]
    for k in range(1, TOP_K):
        hit = hit | (prow == rows_t[k:k + 1, :])
    onehot = jnp.where(hit, 1.0, 0.0).astype(BF16)
    srt_ref[...] = _dot(onehot, hx_ref[...])

    def copy(seg_row, loc_row):
        return pltpu.make_async_copy(srt_ref.at[pl.ds(loc_row, CHUNK)], xb_ref.at[pl.ds(seg_row, CHUNK)], sem)

    _chunk_copies(t, dst0_ref, nch_ref, loc_ref, copy, "start")
    _chunk_copies(t, dst0_ref, nch_ref, loc_ref, copy, "wait")

    @pl.when(t == pl.num_programs(0) - 1)
    def _():
        zero_ref[...] = jnp.zeros_like(zero_ref)
        for op in ("start", "wait"):
            for e in range(N_EXPERTS):
                def body(c, carry):
                    row0 = pl.multiple_of(zero0_ref[e] + c * SUBLANES, SUBLANES)
                    cp = pltpu.make_async_copy(zero_ref.at[pl.ds(0, SUBLANES)],
                                               xb_ref.at[pl.ds(row0, SUBLANES)], sem)
                    cp.start() if op == "start" else cp.wait()
                    return carry

                lax.fori_loop(0, nzero_ref[e], body, 0)

            def tail(b, carry):
                row0 = pl.multiple_of(b * FFN_BLOCK, FFN_BLOCK)
                cp = pltpu.make_async_copy(zero_ref, xb_ref.at[pl.ds(row0, FFN_BLOCK)], sem)
                cp.start() if op == "start" else cp.wait()
                return carry

            lax.fori_loop(nreal_ref[0], xb_ref.shape[0] // FFN_BLOCK, tail, 0)


def _dispatch(hx, meta, tbl):
    n, width = hx.shape
    tm = TOKEN_TILE
    grid_spec = pltpu.PrefetchScalarGridSpec(
        num_scalar_prefetch=6,
        grid=(n // tm,),
        in_specs=[pl.BlockSpec((tm, width), lambda i, *_: (i, 0)),
                  pl.BlockSpec((tm, LANES), lambda i, *_: (i, 0)),
                  pl.BlockSpec((1, 1, N_EXPERTS), lambda i, *_: (i, 0, 0))],
        out_specs=pl.BlockSpec(memory_space=pl.ANY),
        scratch_shapes=[pltpu.VMEM((SORTED_ROWS, width), F32),
                        pltpu.VMEM((FFN_BLOCK, width), F32),
                        pltpu.SemaphoreType.DMA(())],
    )
    return pl.pallas_call(
        _dispatch_kernel,
        grid_spec=grid_spec,
        out_shape=jax.ShapeDtypeStruct((tbl["cap"], width), F32),
        compiler_params=_params("arbitrary"),
        name="dispatch",
    )(tbl["dst0"], tbl["nch"], tbl["loc"], tbl["zero0"], tbl["nzero"], tbl["n_real"],
      hx, meta, tbl["delta"])


def _ffn_kernel(exp_ref, nreal_ref, x_ref, wgu_ref, bgu_ref, wd_ref, bd_ref, y_ref):
    b = pl.program_id(0)
    d_ff, d = wd_ref.shape[1], wd_ref.shape[2]

    @pl.when(b >= nreal_ref[0])
    def _():
        y_ref[...] = jnp.zeros_like(y_ref)

    @pl.when(b < nreal_ref[0])
    def _():
        x = x_ref[...]
        ext = x[:, d:]
        expert = exp_ref[b].astype(F32)
        gate = jnp.zeros((x.shape[0], 1), F32)
        for k in range(TOP_K):
            g0 = EXT_GATE + GATE_PARTS * k
            g = ext[:, g0:g0 + 1] + ext[:, g0 + 1:g0 + 2] + ext[:, g0 + 2:g0 + 3]
            gate = gate + jnp.where(ext[:, EXT_EXPERT + k:EXT_EXPERT + k + 1] == expert, g, 0.0)
        gu = _dot(x[:, :d].astype(BF16), wgu_ref[0]) + bgu_ref[0]
        g = jnp.minimum(gu[:, :d_ff], SWIGLU_LIMIT)
        u = jnp.clip(gu[:, d_ff:], -SWIGLU_LIMIT, SWIGLU_LIMIT)
        act = (u + 1.0) * (g * jax.nn.sigmoid(SWIGLU_ALPHA * g))
        y_ref[...] = gate * (_dot(act.astype(BF16), wd_ref[0]) + bd_ref[0])


def _ffn(xb, tbl, w_gate_up, b_gate_up, w_down, b_down):
    cap, width = xb.shape
    e, d, two_ff = w_gate_up.shape
    d_ff = two_ff // 2
    grid_spec = pltpu.PrefetchScalarGridSpec(
        num_scalar_prefetch=2,
        grid=(cap // FFN_BLOCK,),
        in_specs=[pl.BlockSpec((FFN_BLOCK, width), lambda b, ex, nr: (b, 0)),
                  pl.BlockSpec((1, d, two_ff), lambda b, ex, nr: (ex[b], 0, 0)),
                  pl.BlockSpec((1, 1, two_ff), lambda b, ex, nr: (ex[b], 0, 0)),
                  pl.BlockSpec((1, d_ff, d), lambda b, ex, nr: (ex[b], 0, 0)),
                  pl.BlockSpec((1, 1, d), lambda b, ex, nr: (ex[b], 0, 0))],
        out_specs=pl.BlockSpec((FFN_BLOCK, d), lambda b, ex, nr: (b, 0)),
    )
    return pl.pallas_call(
        _ffn_kernel,
        grid_spec=grid_spec,
        out_shape=jax.ShapeDtypeStruct((cap, d), F32),
        compiler_params=_params("arbitrary"),
        name="expert_ffn",
    )(tbl["blk_exp"], tbl["n_real"], xb, w_gate_up,
      b_gate_up.reshape(e, 1, two_ff), w_down, b_down.reshape(e, 1, d))


def _combine_kernel(dst0_ref, nch_ref, loc_ref, meta_ref, delta_ref, h_ref, y_ref, o_ref, srt_ref, sem):
    t = pl.program_id(0)
    tm = h_ref.shape[0]

    @pl.when(t == 0)
    def _():
        srt_ref[...] = jnp.zeros_like(srt_ref)

    def copy(seg_row, loc_row):
        return pltpu.make_async_copy(y_ref.at[pl.ds(seg_row, CHUNK)], srt_ref.at[pl.ds(loc_row, CHUNK)], sem)

    _chunk_copies(t, dst0_ref, nch_ref, loc_ref, copy, "start")
    rows = _tile_rows(meta_ref, delta_ref)
    pcol = lax.broadcasted_iota(jnp.int32, (tm, SORTED_ROWS), 1).astype(F32)
    hit = pcol == rows[0]
    for k in range(1, TOP_K):
        hit = hit | (pcol == rows[k])
    onehot = jnp.where(hit, 1.0, 0.0).astype(BF16)
    _chunk_copies(t, dst0_ref, nch_ref, loc_ref, copy, "wait")
    o_ref[...] = h_ref[...] + _dot(onehot, srt_ref[...].astype(BF16))


def _combine(meta, tbl, h1, yb):
    n, d = h1.shape
    tm = TOKEN_TILE
    grid_spec = pltpu.PrefetchScalarGridSpec(
        num_scalar_prefetch=3,
        grid=(n // tm,),
        in_specs=[pl.BlockSpec((tm, LANES), lambda i, *_: (i, 0)),
                  pl.BlockSpec((1, 1, N_EXPERTS), lambda i, *_: (i, 0, 0)),
                  pl.BlockSpec((tm, d), lambda i, *_: (i, 0)),
                  pl.BlockSpec(memory_space=pl.ANY)],
        out_specs=pl.BlockSpec((tm, d), lambda i, *_: (i, 0)),
        scratch_shapes=[pltpu.VMEM((SORTED_ROWS, d), F32), pltpu.SemaphoreType.DMA(())],
    )
    return pl.pallas_call(
        _combine_kernel,
        grid_spec=grid_spec,
        out_shape=jax.ShapeDtypeStruct((n, d), F32),
        compiler_params=_params("arbitrary"),
        name="combine",
    )(tbl["dst0"], tbl["nch"], tbl["loc"], meta, tbl["delta"], h1, yb)


def kernel(x, attn_norm_w, w_in, q_norm_w, k_norm_w, sinks, w_proj_swa, w_proj_sb, w_out,
           ffn_norm_w, router_w, router_b, w_gate_up, b_gate_up, w_down, b_down):
    batch, seq, d = x.shape
    n = batch * seq
    h = x.reshape(n, d)
    for layer in range(attn_norm_w.shape[0]):
        q_a, kv_a, q_b, k_b, v_b, g_a, g_b = _in_proj(h, attn_norm_w[layer], w_in[layer].astype(BF16))
        y_a = _swa(q_a, kv_a, q_norm_w[layer], k_norm_w[layer], sinks[layer], batch, seq)
        y_b = _sb(q_b, k_b, v_b, batch, seq)
        h1 = _merge(y_a, y_b, g_a, g_b, h, w_proj_swa[layer].astype(BF16),
                    w_proj_sb[layer].astype(BF16), w_out[layer].astype(BF16))
        hx, meta, tile_run, counts = _router(h1, ffn_norm_w[layer], router_w[layer], router_b[layer])
        tbl = _routing_tables(tile_run, counts, n)
        xb = _dispatch(hx, meta, tbl)
        yb = _ffn(xb, tbl, w_gate_up[layer].astype(BF16), b_gate_up[layer],
                  w_down[layer].astype(BF16), b_down[layer])
        h = _combine(meta, tbl, h1, yb)
    return h.reshape(batch, seq, d)
```

```python
import jax
import jax.numpy as jnp
from jax import lax
from jax.experimental import pallas as pl
from jax.experimental.pallas import tpu as pltpu

HEAD_DIM = 64
SWA_HEADS = 8
SWA_KV_HEADS = 2
SWA_GROUP = SWA_HEADS // SWA_KV_HEADS
BLOCK = 128
SB_HEADS = 8
SWA_WIDTH = SWA_HEADS * HEAD_DIM
SWA_KV_WIDTH = SWA_KV_HEADS * HEAD_DIM
SB_WIDTH = SB_HEADS * HEAD_DIM
N_EXPERTS = 32
TOP_K = 4
SWIGLU_LIMIT = 7.0
SWIGLU_ALPHA = 1.702
NORM_EPS = 1e-5
ATTN_SCALE = HEAD_DIM ** -0.5
SIGN_BIT = -2 ** 31

LANES = 128
SUBLANES = 8
TOKEN_TILE = 512
FFN_BLOCK = 256
CHUNK = 32
SORTED_ROWS = TOKEN_TILE * TOP_K + N_EXPERTS * CHUNK
SB_Q_ROWS = 512
SB_K_ROWS = 256
VMEM_LIMIT = 56 * 1024 * 1024

F32 = jnp.float32
BF16 = jnp.bfloat16
NEG_BIG = -1e30


def _dot(a, b):
    return jnp.dot(a, b, preferred_element_type=F32)


def _dot_nt(a, b):
    return lax.dot_general(a, b, (((1,), (1,)), ((), ())), preferred_element_type=F32)


def _sigmoid(x):
    return jax.nn.sigmoid(x)


def _rms(x, w):
    return x * lax.rsqrt(jnp.mean(x * x, axis=-1, keepdims=True) + NORM_EPS) * w


def _params(*sem):
    return pltpu.CompilerParams(dimension_semantics=sem, vmem_limit_bytes=VMEM_LIMIT)


def _in_proj_kernel(x_ref, nw_ref, w_ref, qa_ref, kva_ref, qb_ref, kb_ref, vb_ref, ga_ref, gb_ref):
    xn = _rms(x_ref[...], nw_ref[...]).astype(BF16)
    off = 0
    for ref, gate in ((qa_ref, False), (kva_ref, False), (qb_ref, False), (kb_ref, False),
                      (vb_ref, False), (ga_ref, True), (gb_ref, True)):
        width = ref.shape[1]
        y = _dot(xn, w_ref[:, off:off + width])
        if gate:
            y = _sigmoid(y)
        ref[...] = y.astype(ref.dtype)
        off += width


def _in_proj(x2, norm_w, w_in):
    n, d = x2.shape
    widths = (SWA_WIDTH, 2 * SWA_KV_WIDTH, SB_WIDTH, SB_WIDTH, SB_WIDTH, d, d)
    tm = TOKEN_TILE
    return pl.pallas_call(
        _in_proj_kernel,
        grid=(n // tm,),
        in_specs=[pl.BlockSpec((tm, d), lambda i: (i, 0)),
                  pl.BlockSpec((1, d), lambda i: (0, 0)),
                  pl.BlockSpec(w_in.shape, lambda i: (0, 0))],
        out_specs=[pl.BlockSpec((tm, w), lambda i: (i, 0)) for w in widths],
        out_shape=[jax.ShapeDtypeStruct((n, w), BF16) for w in widths],
        compiler_params=_params("parallel"),
        name="in_proj",
    )(x2, norm_w.reshape(1, d), w_in)


def _swa_kernel(q_ref, kvc_ref, kvp_ref, qnw_ref, knw_ref, sink_ref, o_ref):
    i = pl.program_id(1)
    half = HEAD_DIM

    def group_mean_sq(x):
        w = x.shape[1]
        same_head = (lax.broadcasted_iota(jnp.int32, (w, w), 0) // half
                     == lax.broadcasted_iota(jnp.int32, (w, w), 1) // half)
        avg = jnp.where(same_head, 1.0 / half, 0.0).astype(BF16)
        hi, lo = _split_bf16(x * x)
        return _dot(hi, avg) + _dot(lo, avg)

    q = q_ref[...].astype(F32)
    kv = jnp.concatenate([kvp_ref[...], kvc_ref[...]], axis=0)
    k = kv[:, :SWA_KV_WIDTH].astype(F32)
    v = kv[:, SWA_KV_WIDTH:].astype(F32)
    qn = (q * lax.rsqrt(group_mean_sq(q) + NORM_EPS) * qnw_ref[...] * ATTN_SCALE).astype(BF16)
    kn = k * lax.rsqrt(group_mean_sq(k) + NORM_EPS) * knw_ref[...]
    lane = lax.broadcasted_iota(jnp.int32, (2 * BLOCK, LANES), 1)
    low = lane < half
    k_swap = pltpu.roll(kn, half, 1)
    v_swap = pltpu.roll(v, half, 1).astype(BF16)
    v_same = v.astype(BF16)
    keys = [[jnp.where(low, kn, 0.0).astype(BF16), jnp.where(low, 0.0, k_swap).astype(BF16)],
            [jnp.where(low, k_swap, 0.0).astype(BF16), jnp.where(low, 0.0, kn).astype(BF16)]]
    vals = [[v_same, v_swap], [v_swap, v_same]]

    row = lax.broadcasted_iota(jnp.int32, (BLOCK, 2 * BLOCK), 0)
    col = lax.broadcasted_iota(jnp.int32, (BLOCK, 2 * BLOCK), 1)
    rel = row + BLOCK - col
    valid = (rel >= 0) & (rel < BLOCK) & ((col >= BLOCK) | (i > 0))
    rel = rel.astype(F32)
    out_lane_low = lax.broadcasted_iota(jnp.int32, (BLOCK, LANES), 1) < half
    for blk in range(SWA_HEADS // 2):
        qb = qn[:, blk * LANES:(blk + 1) * LANES]
        halves = []
        for par in range(2):
            h = 2 * blk + par
            g = h // SWA_GROUP
            slope = 2.0 ** (-8.0 * (h + 1) / SWA_HEADS)
            s = jnp.where(valid, _dot_nt(qb, keys[g][par]) - slope * rel, NEG_BIG)
            sink = sink_ref[h]
            m = jnp.maximum(jnp.max(s, axis=-1, keepdims=True), sink)
            p = jnp.exp(s - m)
            den = jnp.sum(p, axis=-1, keepdims=True) + jnp.exp(sink - m)
            halves.append(_dot(p.astype(BF16), vals[g][par]) / den)
        o_ref[:, blk * LANES:(blk + 1) * LANES] = jnp.where(out_lane_low, halves[0], halves[1]).astype(o_ref.dtype)


def _swa(q_a, kv_a, q_norm_w, k_norm_w, sinks, batch, seq):
    n = q_a.shape[0]
    nb = seq // BLOCK
    return pl.pallas_call(
        _swa_kernel,
        grid=(batch, nb),
        in_specs=[pl.BlockSpec((BLOCK, SWA_WIDTH), lambda b, i: (b * nb + i, 0)),
                  pl.BlockSpec((BLOCK, 2 * SWA_KV_WIDTH), lambda b, i: (b * nb + i, 0)),
                  pl.BlockSpec((BLOCK, 2 * SWA_KV_WIDTH), lambda b, i: (b * nb + jnp.maximum(i - 1, 0), 0)),
                  pl.BlockSpec((1, SWA_WIDTH), lambda b, i: (0, 0)),
                  pl.BlockSpec((1, SWA_KV_WIDTH), lambda b, i: (0, 0)),
                  pl.BlockSpec(memory_space=pltpu.SMEM)],
        out_specs=pl.BlockSpec((BLOCK, SWA_WIDTH), lambda b, i: (b * nb + i, 0)),
        out_shape=jax.ShapeDtypeStruct((n, SWA_WIDTH), BF16),
        compiler_params=_params("parallel", "parallel"),
        name="swa",
    )(q_a, kv_a, kv_a, jnp.tile(q_norm_w, SWA_HEADS).reshape(1, SWA_WIDTH),
      jnp.tile(k_norm_w, SWA_KV_HEADS).reshape(1, SWA_KV_WIDTH), sinks)


def _split_bf16(x):
    hi = x.astype(BF16)
    lo = (x - hi.astype(F32)).astype(BF16)
    return hi, lo


def _sb_kernel(q_ref, k_ref, v_ref, o_ref):
    seq = q_ref.shape[0]
    tq, tk = SB_Q_ROWS, SB_K_ROWS
    nq = seq // tq
    k_per_q = tq // tk
    lane = lax.broadcasted_iota(jnp.int32, (tq, LANES), 1)
    row = lax.broadcasted_iota(jnp.int32, (tq, tk), 0)
    col = lax.broadcasted_iota(jnp.int32, (tq, tk), 1)
    neg_incl = jnp.where(lax.broadcasted_iota(jnp.int32, (tk, tk), 0)
                         >= lax.broadcasted_iota(jnp.int32, (tk, tk), 1), -1.0, 0.0).astype(BF16)

    def softplus(z):
        neg_abs = lax.bitcast_convert_type(lax.bitcast_convert_type(z, jnp.int32) | SIGN_BIT, F32)
        return jnp.maximum(z, 0.0) + jnp.log(1.0 + jnp.exp(neg_abs))

    def neg_suffix_sum(sp):
        return _dot(sp.astype(BF16), neg_incl)

    def k_span(k0s, qms, state, masks):
        pairs = [(i, p) for i in range(len(k0s)) for p in range(2)]
        kbs = [k_ref[pl.ds(k0, tk), :] for k0 in k0s]
        vbs = [v_ref[pl.ds(k0, tk), :] for k0 in k0s]
        zs = {(i, p): _dot_nt(qms[p], kbs[i]) for i, p in pairs}
        sps = {ip: softplus(zs[ip]) for ip in pairs}
        for (i, p) in pairs:
            if masks[i] is not None:
                sps[(i, p)] = jnp.where(masks[i], sps[(i, p)], 0.0)
        cs = {ip: neg_suffix_sum(sps[ip]) for ip in pairs}
        accs, runs = list(state[:2]), list(state[2:])
        ws = {}
        for (i, p) in pairs:
            w = jnp.exp(zs[(i, p)] + cs[(i, p)] + runs[p])
            ws[(i, p)] = w if masks[i] is None else jnp.where(masks[i], w, 0.0)
            runs[p] = runs[p] + cs[(i, p)][:, 0:1]
        for (i, p) in pairs:
            accs[p] = accs[p] + _dot(ws[(i, p)].astype(BF16), vbs[i])
        return (*accs, *runs)

    def q_block(qi, carry):
        q0 = pl.multiple_of(qi * tq, tq)
        q2 = q_ref[pl.ds(q0, tq), :] * ATTN_SCALE
        qms = (jnp.where(lane < HEAD_DIM, q2, 0.0).astype(BF16),
               jnp.where(lane >= HEAD_DIM, q2, 0.0).astype(BF16))
        state = (jnp.zeros((tq, LANES), F32), jnp.zeros((tq, LANES), F32),
                 jnp.zeros((tq, 1), F32), jnp.zeros((tq, 1), F32))
        order = list(reversed(range(k_per_q)))
        state = k_span([q0 + d * tk for d in order], qms, state, [(col + d * tk) < row for d in order])

        def earlier(jj, state):
            k0s = [pl.multiple_of((qi - 1 - jj) * tq + d * tk, tk) for d in order]
            return k_span(k0s, qms, state, [None] * k_per_q)

        state = lax.fori_loop(0, qi, earlier, state)
        o_ref[pl.ds(q0, tq), :] = jnp.where(lane < HEAD_DIM, state[0], state[1]).astype(o_ref.dtype)
        return carry

    lax.fori_loop(0, nq, q_block, 0)


def _sb(q_b, k_b, v_b, batch, seq):
    n = q_b.shape[0]
    spec = pl.BlockSpec((seq, LANES), lambda b, p: (b, p))
    return pl.pallas_call(
        _sb_kernel,
        grid=(batch, SB_WIDTH // LANES),
        in_specs=[spec, spec, spec],
        out_specs=spec,
        out_shape=jax.ShapeDtypeStruct((n, SB_WIDTH), BF16),
        compiler_params=_params("parallel", "parallel"),
        name="stick_breaking",
    )(q_b, k_b, v_b)


def _merge_kernel(ya_ref, yb_ref, ga_ref, gb_ref, x_ref, pa_ref, pb_ref, wo_ref, h_ref):
    m = (ga_ref[...].astype(F32) * _dot(ya_ref[...], pa_ref[...])
         + gb_ref[...].astype(F32) * _dot(yb_ref[...], pb_ref[...]))
    h_ref[...] = x_ref[...] + _dot(m.astype(BF16), wo_ref[...])


def _merge(y_a, y_b, g_a, g_b, x2, p_a, p_b, w_o):
    n, d = x2.shape
    tm = TOKEN_TILE
    row = lambda w: pl.BlockSpec((tm, w), lambda i: (i, 0))
    full = lambda a: pl.BlockSpec(a.shape, lambda i: (0, 0))
    return pl.pallas_call(
        _merge_kernel,
        grid=(n // tm,),
        in_specs=[row(SWA_WIDTH), row(SB_WIDTH), row(d), row(d), row(d), full(p_a), full(p_b), full(w_o)],
        out_specs=row(d),
        out_shape=jax.ShapeDtypeStruct((n, d), F32),
        compiler_params=_params("parallel"),
        name="merge",
    )(y_a, y_b, g_a, g_b, x2, p_a, p_b, w_o)


META_EXPERT = 0
META_RANK = TOP_K
EXT_EXPERT = 0
EXT_GATE = TOP_K
GATE_PARTS = 3


def _router_kernel(h_ref, nw_ref, rw_ref, rb_ref, hx_ref, meta_ref, trun_ref, cnt_ref, run_ref):
    tm, d = h_ref.shape

    @pl.when(pl.program_id(0) == 0)
    def _():
        run_ref[...] = jnp.zeros_like(run_ref)

    hn = _rms(h_ref[...], nw_ref[...])
    logits = jnp.dot(hn, rw_ref[...], preferred_element_type=F32,
                     precision=lax.Precision.HIGHEST) + rb_ref[...]
    lane = lax.broadcasted_iota(jnp.int32, (tm, N_EXPERTS), 1).astype(F32)
    work = logits
    tops, idxs, hots = [], [], []
    for _ in range(TOP_K):
        m = jnp.max(work, axis=-1, keepdims=True)
        idx = jnp.min(jnp.where(work == m, lane, float(N_EXPERTS)), axis=-1, keepdims=True)
        hot = lane == idx
        work = jnp.where(hot, -jnp.inf, work)
        tops.append(m)
        idxs.append(idx)
        hots.append(hot)
    exps = [jnp.exp(t - tops[0]) for t in tops]
    den = exps[0] + exps[1] + exps[2] + exps[3]
    chosen = jnp.where(hots[0] | hots[1] | hots[2] | hots[3], 1.0, 0.0)
    r = lax.broadcasted_iota(jnp.int32, (tm, tm), 0)
    c = lax.broadcasted_iota(jnp.int32, (tm, tm), 1)
    before = jnp.where(c < r, 1.0, 0.0).astype(BF16)
    run = run_ref[...]
    prefix = _dot(before, chosen.astype(BF16)) + run
    mlane = lax.broadcasted_iota(jnp.int32, (tm, LANES), 1)
    meta = jnp.zeros((tm, LANES), F32)
    ext = jnp.zeros((tm, LANES), F32)
    for k in range(TOP_K):
        rank = jnp.sum(jnp.where(hots[k], prefix, 0.0), axis=-1, keepdims=True)
        meta = jnp.where(mlane == META_EXPERT + k, idxs[k], meta)
        meta = jnp.where(mlane == META_RANK + k, rank, meta)
        ext = jnp.where(mlane == EXT_EXPERT + k, idxs[k], ext)
        rest = exps[k] / den
        for part in range(GATE_PARTS):
            piece = rest.astype(BF16).astype(F32)
            ext = jnp.where(mlane == EXT_GATE + GATE_PARTS * k + part, piece, ext)
            rest = rest - piece
    meta_ref[...] = meta
    hx_ref[:, :d] = hn.astype(BF16)
    hx_ref[:, d:] = ext.astype(BF16)
    trun_ref[0] = run
    tile_cnt = jnp.sum(chosen, axis=0, keepdims=True)
    run = run + jnp.ceil(tile_cnt / SUBLANES) * SUBLANES
    run_ref[...] = run
    cnt_ref[...] = run


def _router(h1, norm_w, router_w, router_b):
    n, d = h1.shape
    tm = TOKEN_TILE
    tiles = n // tm
    return pl.pallas_call(
        _router_kernel,
        grid=(tiles,),
        in_specs=[pl.BlockSpec((tm, d), lambda i: (i, 0)),
                  pl.BlockSpec((1, d), lambda i: (0, 0)),
                  pl.BlockSpec((d, N_EXPERTS), lambda i: (0, 0)),
                  pl.BlockSpec((1, N_EXPERTS), lambda i: (0, 0))],
        out_specs=[pl.BlockSpec((tm, d + LANES), lambda i: (i, 0)),
                   pl.BlockSpec((tm, LANES), lambda i: (i, 0)),
                   pl.BlockSpec((1, 1, N_EXPERTS), lambda i: (i, 0, 0)),
                   pl.BlockSpec((1, N_EXPERTS), lambda i: (0, 0))],
        out_shape=[jax.ShapeDtypeStruct((n, d + LANES), BF16),
                   jax.ShapeDtypeStruct((n, LANES), F32),
                   jax.ShapeDtypeStruct((tiles, 1, N_EXPERTS), F32),
                   jax.ShapeDtypeStruct((1, N_EXPERTS), F32)],
        scratch_shapes=[pltpu.VMEM((1, N_EXPERTS), F32)],
        compiler_params=_params("arbitrary"),
        name="router",
    )(h1, norm_w.reshape(1, d), router_w, router_b.reshape(1, N_EXPERTS))


def _routing_tables(tile_run, counts, n):
    tiles = tile_run.shape[0]
    counts = counts.reshape(N_EXPERTS).astype(jnp.int32)
    run = tile_run.reshape(tiles, N_EXPERTS).astype(jnp.int32)
    tile_len = jnp.concatenate([run[1:], counts[None]], axis=0) - run
    nch = (tile_len + CHUNK - 1) // CHUNK
    loc = (jnp.cumsum(nch, axis=1) - nch) * CHUNK
    padded = (counts + CHUNK - 1 + FFN_BLOCK - 1) // FFN_BLOCK * FFN_BLOCK
    seg_end = jnp.cumsum(padded)
    seg_start = seg_end - padded
    dst0 = seg_start[None, :] + run
    delta = (loc - run).astype(F32).reshape(tiles, 1, N_EXPERTS)
    zero0 = seg_start + counts
    nzero = (padded - counts) // SUBLANES
    n_real = seg_end[-1] // FFN_BLOCK
    max_blocks = (n * TOP_K + tiles * N_EXPERTS * (SUBLANES - 1)
                  + N_EXPERTS * (CHUNK - 1 + FFN_BLOCK - 1)) // FFN_BLOCK
    blk_exp = jnp.minimum(jnp.sum(jnp.arange(max_blocks, dtype=jnp.int32)[:, None] * FFN_BLOCK
                                  >= seg_end[None, :], axis=1), N_EXPERTS - 1).astype(jnp.int32)
    return dict(dst0=dst0.reshape(-1), nch=nch.reshape(-1).astype(jnp.int32), loc=loc.reshape(-1),
                delta=delta, zero0=zero0, nzero=nzero, blk_exp=blk_exp,
                n_real=n_real.reshape(1).astype(jnp.int32), cap=max_blocks * FFN_BLOCK)


def _tile_rows(meta_ref, delta_ref):
    meta = meta_ref[...]
    tm = meta.shape[0]
    delta = delta_ref[0]
    lane = lax.broadcasted_iota(jnp.int32, (tm, N_EXPERTS), 1).astype(F32)
    rows = []
    for k in range(TOP_K):
        expert = meta[:, META_EXPERT + k:META_EXPERT + k + 1]
        rank = meta[:, META_RANK + k:META_RANK + k + 1]
        rows.append(rank + jnp.sum(jnp.where(lane == expert, delta, 0.0), axis=-1, keepdims=True))
    return rows


def _chunk_copies(t, tbl_ref, nch_ref, loc_ref, make_copy, op):
    for e in range(N_EXPERTS):
        idx = t * N_EXPERTS + e
        seg_row = tbl_ref[idx]
        loc_row = loc_ref[idx]

        def body(c, carry):
            cp = make_copy(pl.multiple_of(seg_row + c * CHUNK, SUBLANES),
                           pl.multiple_of(loc_row + c * CHUNK, CHUNK))
            cp.start() if op == "start" else cp.wait()
            return carry

        lax.fori_loop(0, nch_ref[idx], body, 0)


def _dispatch_kernel(dst0_ref, nch_ref, loc_ref, zero0_ref, nzero_ref, nreal_ref,
                     hx_ref, meta_ref, delta_ref, xb_ref, srt_ref, zero_ref, sem):
    t = pl.program_id(0)
    tm = hx_ref.shape[0]
    rows = _tile_rows(meta_ref, delta_ref)
    mlane = lax.broadcasted_iota(jnp.int32, (tm, LANES), 1)
    packed = jnp.full((tm, LANES), -1.0, F32)
    for k in range(TOP_K):
        packed = jnp.where(mlane == k, rows[k], packed)
    rows_t = packed.T
    prow = lax.broadcasted_iota(jnp.int32, (SORTED_ROWS, tm), 0).astype(F32)
    onehot = jnp.zeros((SORTED_ROWS, tm), F32)
    for k in range(TOP_K):
        onehot = jnp.where(prow == rows_t[k:k + 1, :], 1.0, onehot)
    slot = t % 2
    srt_ref[slot] = _dot(onehot.astype(BF16), hx_ref[...])

    def copy_from(s):
        def copy(seg_row, loc_row):
            return pltpu.make_async_copy(srt_ref.at[s, pl.ds(loc_row, CHUNK)],
                                         xb_ref.at[pl.ds(seg_row, CHUNK)], sem.at[s])
        return copy

    @pl.when(t > 0)
    def _():
        _chunk_copies(t - 1, dst0_ref, nch_ref, loc_ref, copy_from(1 - slot), "wait")

    _chunk_copies(t, dst0_ref, nch_ref, loc_ref, copy_from(slot), "start")

    @pl.when(t == pl.num_programs(0) - 1)
    def _():
        _chunk_copies(t, dst0_ref, nch_ref, loc_ref, copy_from(slot), "wait")
        zero_ref[...] = jnp.zeros_like(zero_ref)
        for op in ("start", "wait"):
            for e in range(N_EXPERTS):
                def body(c, carry):
                    row0 = pl.multiple_of(zero0_ref[e] + c * SUBLANES, SUBLANES)
                    cp = pltpu.make_async_copy(zero_ref.at[pl.ds(0, SUBLANES)],
                                               xb_ref.at[pl.ds(row0, SUBLANES)], sem.at[0])
                    cp.start() if op == "start" else cp.wait()
                    return carry

                lax.fori_loop(0, nzero_ref[e], body, 0)

            def tail(b, carry):
                row0 = pl.multiple_of(b * FFN_BLOCK, FFN_BLOCK)
                cp = pltpu.make_async_copy(zero_ref, xb_ref.at[pl.ds(row0, FFN_BLOCK)], sem.at[0])
                cp.start() if op == "start" else cp.wait()
                return carry

            lax.fori_loop(nreal_ref[0], xb_ref.shape[0] // FFN_BLOCK, tail, 0)


def _dispatch(hx, meta, tbl):
    n, width = hx.shape
    tm = TOKEN_TILE
    grid_spec = pltpu.PrefetchScalarGridSpec(
        num_scalar_prefetch=6,
        grid=(n // tm,),
        in_specs=[pl.BlockSpec((tm, width), lambda i, *_: (i, 0)),
                  pl.BlockSpec((tm, LANES), lambda i, *_: (i, 0)),
                  pl.BlockSpec((1, 1, N_EXPERTS), lambda i, *_: (i, 0, 0))],
        out_specs=pl.BlockSpec(memory_space=pl.ANY),
        scratch_shapes=[pltpu.VMEM((2, SORTED_ROWS, width), F32),
                        pltpu.VMEM((FFN_BLOCK, width), F32),
                        pltpu.SemaphoreType.DMA((2,))],
    )
    return pl.pallas_call(
        _dispatch_kernel,
        grid_spec=grid_spec,
        out_shape=jax.ShapeDtypeStruct((tbl["cap"], width), F32),
        compiler_params=_params("arbitrary"),
        name="dispatch",
    )(tbl["dst0"], tbl["nch"], tbl["loc"], tbl["zero0"], tbl["nzero"], tbl["n_real"],
      hx, meta, tbl["delta"])


def _ffn_kernel(exp_ref, nreal_ref, x_ref, wgu_ref, bgu_ref, wd_ref, bd_ref, y_ref, wgu_bf, wd_bf):
    b = pl.program_id(0)
    d_ff, d = wd_ref.shape[1], wd_ref.shape[2]

    @pl.when(b >= nreal_ref[0])
    def _():
        y_ref[...] = jnp.zeros_like(y_ref)

    @pl.when((b == 0) | (exp_ref[b] != exp_ref[jnp.maximum(b - 1, 0)]))
    def _():
        wgu_bf[...] = wgu_ref[0].astype(BF16)
        wd_bf[...] = wd_ref[0].astype(BF16)

    @pl.when(b < nreal_ref[0])
    def _():
        x = x_ref[...]
        ext = x[:, d:]
        expert = exp_ref[b].astype(F32)
        gate = jnp.zeros((x.shape[0], 1), F32)
        for k in range(TOP_K):
            g0 = EXT_GATE + GATE_PARTS * k
            g = ext[:, g0:g0 + 1] + ext[:, g0 + 1:g0 + 2] + ext[:, g0 + 2:g0 + 3]
            gate = gate + jnp.where(ext[:, EXT_EXPERT + k:EXT_EXPERT + k + 1] == expert, g, 0.0)
        gu = _dot(x[:, :d].astype(BF16), wgu_bf[...]) + bgu_ref[0]
        g = jnp.minimum(gu[:, :d_ff], SWIGLU_LIMIT)
        u = jnp.clip(gu[:, d_ff:], -SWIGLU_LIMIT, SWIGLU_LIMIT)
        act = (u + 1.0) * (g * jax.nn.sigmoid(SWIGLU_ALPHA * g))
        y_ref[...] = gate * (_dot(act.astype(BF16), wd_bf[...]) + bd_ref[0])


def _ffn(xb, tbl, w_gate_up, b_gate_up, w_down, b_down):
    cap, width = xb.shape
    e, d, two_ff = w_gate_up.shape
    d_ff = two_ff // 2
    grid_spec = pltpu.PrefetchScalarGridSpec(
        num_scalar_prefetch=2,
        grid=(cap // FFN_BLOCK,),
        in_specs=[pl.BlockSpec((FFN_BLOCK, width), lambda b, ex, nr: (b, 0)),
                  pl.BlockSpec((1, d, two_ff), lambda b, ex, nr: (ex[b], 0, 0)),
                  pl.BlockSpec((1, 1, two_ff), lambda b, ex, nr: (ex[b], 0, 0)),
                  pl.BlockSpec((1, d_ff, d), lambda b, ex, nr: (ex[b], 0, 0)),
                  pl.BlockSpec((1, 1, d), lambda b, ex, nr: (ex[b], 0, 0))],
        out_specs=pl.BlockSpec((FFN_BLOCK, d), lambda b, ex, nr: (b, 0)),
        scratch_shapes=[pltpu.VMEM((d, two_ff), BF16), pltpu.VMEM((d_ff, d), BF16)],
    )
    return pl.pallas_call(
        _ffn_kernel,
        grid_spec=grid_spec,
        out_shape=jax.ShapeDtypeStruct((cap, d), F32),
        compiler_params=_params("arbitrary"),
        name="expert_ffn",
    )(tbl["blk_exp"], tbl["n_real"], xb, w_gate_up,
      b_gate_up.reshape(e, 1, two_ff), w_down, b_down.reshape(e, 1, d))


def _combine_kernel(dst0_ref, nch_ref, loc_ref, meta_ref, delta_ref, h_ref, y_ref, o_ref, srt_ref, sem):
    t = pl.program_id(0)
    tm = h_ref.shape[0]

    @pl.when(t == 0)
    def _():
        srt_ref[...] = jnp.zeros_like(srt_ref)

    slot = t % 2

    def copy_into(s):
        def copy(seg_row, loc_row):
            return pltpu.make_async_copy(y_ref.at[pl.ds(seg_row, CHUNK)],
                                         srt_ref.at[s, pl.ds(loc_row, CHUNK)], sem.at[s])
        return copy

    @pl.when(t == 0)
    def _():
        _chunk_copies(t, dst0_ref, nch_ref, loc_ref, copy_into(slot), "start")

    @pl.when(t + 1 < pl.num_programs(0))
    def _():
        _chunk_copies(t + 1, dst0_ref, nch_ref, loc_ref, copy_into(1 - slot), "start")

    rows = _tile_rows(meta_ref, delta_ref)
    pcol = lax.broadcasted_iota(jnp.int32, (tm, SORTED_ROWS), 1).astype(F32)
    onehot = jnp.zeros((tm, SORTED_ROWS), F32)
    for k in range(TOP_K):
        onehot = jnp.where(pcol == rows[k], 1.0, onehot)
    onehot = onehot.astype(BF16)
    _chunk_copies(t, dst0_ref, nch_ref, loc_ref, copy_into(slot), "wait")
    o_ref[...] = h_ref[...] + _dot(onehot, srt_ref[slot].astype(BF16))


def _combine(meta, tbl, h1, yb):
    n, d = h1.shape
    tm = TOKEN_TILE
    grid_spec = pltpu.PrefetchScalarGridSpec(
        num_scalar_prefetch=3,
        grid=(n // tm,),
        in_specs=[pl.BlockSpec((tm, LANES), lambda i, *_: (i, 0)),
                  pl.BlockSpec((1, 1, N_EXPERTS), lambda i, *_: (i, 0, 0)),
                  pl.BlockSpec((tm, d), lambda i, *_: (i, 0)),
                  pl.BlockSpec(memory_space=pl.ANY)],
        out_specs=pl.BlockSpec((tm, d), lambda i, *_: (i, 0)),
        scratch_shapes=[pltpu.VMEM((2, SORTED_ROWS, d), F32), pltpu.SemaphoreType.DMA((2,))],
    )
    return pl.pallas_call(
        _combine_kernel,
        grid_spec=grid_spec,
        out_shape=jax.ShapeDtypeStruct((n, d), F32),
        compiler_params=_params("arbitrary"),
        name="combine",
    )(tbl["dst0"], tbl["nch"], tbl["loc"], meta, tbl["delta"], h1, yb)


def kernel(x, attn_norm_w, w_in, q_norm_w, k_norm_w, sinks, w_proj_swa, w_proj_sb, w_out,
           ffn_norm_w, router_w, router_b, w_gate_up, b_gate_up, w_down, b_down):
    batch, seq, d = x.shape
    n = batch * seq
    h = x.reshape(n, d)
    for layer in range(attn_norm_w.shape[0]):
        q_a, kv_a, q_b, k_b, v_b, g_a, g_b = _in_proj(h, attn_norm_w[layer], w_in[layer].astype(BF16))
        y_a = _swa(q_a, kv_a, q_norm_w[layer], k_norm_w[layer], sinks[layer], batch, seq)
        y_b = _sb(q_b, k_b, v_b, batch, seq)
        h1 = _merge(y_a, y_b, g_a, g_b, h, w_proj_swa[layer].astype(BF16),
                    w_proj_sb[layer].astype(BF16), w_out[layer].astype(BF16))
        hx, meta, tile_run, counts = _router(h1, ffn_norm_w[layer], router_w[layer], router_b[layer])
        tbl = _routing_tables(tile_run, counts, n)
        xb = _dispatch(hx, meta, tbl)
        yb = _ffn(xb, tbl, w_gate_up[layer], b_gate_up[layer], w_down[layer], b_down[layer])
        h = _combine(meta, tbl, h1, yb)
    return h.reshape(batch, seq, d)
```

```python
import jax
import jax.numpy as jnp
from jax import lax
from jax.experimental import pallas as pl
from jax.experimental.pallas import tpu as pltpu

HEAD_DIM = 64
SWA_HEADS = 8
SWA_KV_HEADS = 2
SWA_GROUP = SWA_HEADS // SWA_KV_HEADS
BLOCK = 128
SB_HEADS = 8
SWA_WIDTH = SWA_HEADS * HEAD_DIM
SWA_KV_WIDTH = SWA_KV_HEADS * HEAD_DIM
SB_WIDTH = SB_HEADS * HEAD_DIM
N_EXPERTS = 32
TOP_K = 4
SWIGLU_LIMIT = 7.0
SWIGLU_ALPHA = 1.702
NORM_EPS = 1e-5
ATTN_SCALE = HEAD_DIM ** -0.5
SIGN_BIT = -2 ** 31

LANES = 128
SUBLANES = 8
TOKEN_TILE = 512
FFN_BLOCK = 256
CHUNK = 32
SORTED_ROWS = -(-(TOKEN_TILE * TOP_K + N_EXPERTS * (SUBLANES - 1)) // LANES) * LANES
SB_Q_ROWS = 512
SB_K_ROWS = 256
VMEM_LIMIT = 56 * 1024 * 1024

F32 = jnp.float32
BF16 = jnp.bfloat16
NEG_BIG = -1e30


def _dot(a, b):
    return jnp.dot(a, b, preferred_element_type=F32)


def _dot_nt(a, b):
    return lax.dot_general(a, b, (((1,), (1,)), ((), ())), preferred_element_type=F32)


def _sigmoid(x):
    return jax.nn.sigmoid(x)


def _rms(x, w):
    return x * lax.rsqrt(jnp.mean(x * x, axis=-1, keepdims=True) + NORM_EPS) * w


def _params(*sem):
    return pltpu.CompilerParams(dimension_semantics=sem, vmem_limit_bytes=VMEM_LIMIT)


def _in_proj_kernel(x_ref, nw_ref, w_ref, qa_ref, kva_ref, qb_ref, kb_ref, vb_ref, ga_ref, gb_ref):
    xn = _rms(x_ref[...], nw_ref[...]).astype(BF16)
    off = 0
    for ref, gate in ((qa_ref, False), (kva_ref, False), (qb_ref, False), (kb_ref, False),
                      (vb_ref, False), (ga_ref, True), (gb_ref, True)):
        width = ref.shape[1]
        y = _dot(xn, w_ref[:, off:off + width])
        if gate:
            y = _sigmoid(y)
        ref[...] = y.astype(ref.dtype)
        off += width


def _in_proj(x2, norm_w, w_in):
    n, d = x2.shape
    widths = (SWA_WIDTH, 2 * SWA_KV_WIDTH, SB_WIDTH, SB_WIDTH, SB_WIDTH, d, d)
    tm = TOKEN_TILE
    return pl.pallas_call(
        _in_proj_kernel,
        grid=(n // tm,),
        in_specs=[pl.BlockSpec((tm, d), lambda i: (i, 0)),
                  pl.BlockSpec((1, d), lambda i: (0, 0)),
                  pl.BlockSpec(w_in.shape, lambda i: (0, 0))],
        out_specs=[pl.BlockSpec((tm, w), lambda i: (i, 0)) for w in widths],
        out_shape=[jax.ShapeDtypeStruct((n, w), BF16) for w in widths],
        compiler_params=_params("parallel"),
        name="in_proj",
    )(x2, norm_w.reshape(1, d), w_in)


def _swa_kernel(q_ref, kvc_ref, kvp_ref, qnw_ref, knw_ref, sink_ref, o_ref):
    i = pl.program_id(1)
    half = HEAD_DIM

    def group_mean_sq(x):
        w = x.shape[1]
        same_head = (lax.broadcasted_iota(jnp.int32, (w, w), 0) // half
                     == lax.broadcasted_iota(jnp.int32, (w, w), 1) // half)
        avg = jnp.where(same_head, 1.0 / half, 0.0).astype(BF16)
        hi, lo = _split_bf16(x * x)
        return _dot(hi, avg) + _dot(lo, avg)

    q = q_ref[...].astype(F32)
    kv = jnp.concatenate([kvp_ref[...], kvc_ref[...]], axis=0)
    k = kv[:, :SWA_KV_WIDTH].astype(F32)
    v = kv[:, SWA_KV_WIDTH:].astype(F32)
    qn = (q * lax.rsqrt(group_mean_sq(q) + NORM_EPS) * qnw_ref[...] * ATTN_SCALE).astype(BF16)
    kn = k * lax.rsqrt(group_mean_sq(k) + NORM_EPS) * knw_ref[...]
    lane = lax.broadcasted_iota(jnp.int32, (2 * BLOCK, LANES), 1)
    low = lane < half
    k_swap = pltpu.roll(kn, half, 1)
    v_swap = pltpu.roll(v, half, 1).astype(BF16)
    v_same = v.astype(BF16)
    keys = [[jnp.where(low, kn, 0.0).astype(BF16), jnp.where(low, 0.0, k_swap).astype(BF16)],
            [jnp.where(low, k_swap, 0.0).astype(BF16), jnp.where(low, 0.0, kn).astype(BF16)]]
    vals = [[v_same, v_swap], [v_swap, v_same]]

    row = lax.broadcasted_iota(jnp.int32, (BLOCK, 2 * BLOCK), 0)
    col = lax.broadcasted_iota(jnp.int32, (BLOCK, 2 * BLOCK), 1)
    rel = row + BLOCK - col
    valid = (rel >= 0) & (rel < BLOCK) & ((col >= BLOCK) | (i > 0))
    rel = rel.astype(F32)
    out_lane_low = lax.broadcasted_iota(jnp.int32, (BLOCK, LANES), 1) < half
    for blk in range(SWA_HEADS // 2):
        qb = qn[:, blk * LANES:(blk + 1) * LANES]
        halves = []
        for par in range(2):
            h = 2 * blk + par
            g = h // SWA_GROUP
            slope = 2.0 ** (-8.0 * (h + 1) / SWA_HEADS)
            s = jnp.where(valid, _dot_nt(qb, keys[g][par]) - slope * rel, NEG_BIG)
            sink = sink_ref[h]
            m = jnp.maximum(jnp.max(s, axis=-1, keepdims=True), sink)
            p = jnp.exp(s - m)
            den = jnp.sum(p, axis=-1, keepdims=True) + jnp.exp(sink - m)
            halves.append(_dot(p.astype(BF16), vals[g][par]) / den)
        o_ref[:, blk * LANES:(blk + 1) * LANES] = jnp.where(out_lane_low, halves[0], halves[1]).astype(o_ref.dtype)


def _swa(q_a, kv_a, q_norm_w, k_norm_w, sinks, batch, seq):
    n = q_a.shape[0]
    nb = seq // BLOCK
    return pl.pallas_call(
        _swa_kernel,
        grid=(batch, nb),
        in_specs=[pl.BlockSpec((BLOCK, SWA_WIDTH), lambda b, i: (b * nb + i, 0)),
                  pl.BlockSpec((BLOCK, 2 * SWA_KV_WIDTH), lambda b, i: (b * nb + i, 0)),
                  pl.BlockSpec((BLOCK, 2 * SWA_KV_WIDTH), lambda b, i: (b * nb + jnp.maximum(i - 1, 0), 0)),
                  pl.BlockSpec((1, SWA_WIDTH), lambda b, i: (0, 0)),
                  pl.BlockSpec((1, SWA_KV_WIDTH), lambda b, i: (0, 0)),
                  pl.BlockSpec(memory_space=pltpu.SMEM)],
        out_specs=pl.BlockSpec((BLOCK, SWA_WIDTH), lambda b, i: (b * nb + i, 0)),
        out_shape=jax.ShapeDtypeStruct((n, SWA_WIDTH), BF16),
        compiler_params=_params("parallel", "parallel"),
        name="swa",
    )(q_a, kv_a, kv_a, jnp.tile(q_norm_w, SWA_HEADS).reshape(1, SWA_WIDTH),
      jnp.tile(k_norm_w, SWA_KV_HEADS).reshape(1, SWA_KV_WIDTH), sinks)


def _split_bf16(x):
    hi = x.astype(BF16)
    lo = (x - hi.astype(F32)).astype(BF16)
    return hi, lo


def _sb_kernel(q_ref, k_ref, v_ref, o_ref):
    seq = q_ref.shape[0]
    tq, tk = SB_Q_ROWS, SB_K_ROWS
    nq = seq // tq
    k_per_q = tq // tk
    lane = lax.broadcasted_iota(jnp.int32, (tq, LANES), 1)
    row = lax.broadcasted_iota(jnp.int32, (tq, tk), 0)
    col = lax.broadcasted_iota(jnp.int32, (tq, tk), 1)
    neg_incl = jnp.where(lax.broadcasted_iota(jnp.int32, (tk, tk), 0)
                         >= lax.broadcasted_iota(jnp.int32, (tk, tk), 1), -1.0, 0.0).astype(BF16)

    def softplus(z):
        neg_abs = lax.bitcast_convert_type(lax.bitcast_convert_type(z, jnp.int32) | SIGN_BIT, F32)
        return jnp.maximum(z, 0.0) + jnp.log(1.0 + jnp.exp(neg_abs))

    def neg_suffix_sum(sp):
        return _dot(sp.astype(BF16), neg_incl)

    def k_span(k0s, qms, state, masks):
        pairs = [(i, p) for i in range(len(k0s)) for p in range(2)]
        kbs = [k_ref[pl.ds(k0, tk), :] for k0 in k0s]
        vbs = [v_ref[pl.ds(k0, tk), :] for k0 in k0s]
        zs = {(i, p): _dot_nt(qms[p], kbs[i]) for i, p in pairs}
        sps = {ip: softplus(zs[ip]) for ip in pairs}
        for (i, p) in pairs:
            if masks[i] is not None:
                sps[(i, p)] = jnp.where(masks[i], sps[(i, p)], 0.0)
        cs = {ip: neg_suffix_sum(sps[ip]) for ip in pairs}
        accs, runs = list(state[:2]), list(state[2:])
        ws = {}
        for (i, p) in pairs:
            w = jnp.exp(zs[(i, p)] + cs[(i, p)] + runs[p])
            ws[(i, p)] = w if masks[i] is None else jnp.where(masks[i], w, 0.0)
            runs[p] = runs[p] + cs[(i, p)][:, 0:1]
        for (i, p) in pairs:
            accs[p] = accs[p] + _dot(ws[(i, p)].astype(BF16), vbs[i])
        return (*accs, *runs)

    def q_block(qi, carry):
        q0 = pl.multiple_of(qi * tq, tq)
        q2 = q_ref[pl.ds(q0, tq), :] * ATTN_SCALE
        qms = (jnp.where(lane < HEAD_DIM, q2, 0.0).astype(BF16),
               jnp.where(lane >= HEAD_DIM, q2, 0.0).astype(BF16))
        state = (jnp.zeros((tq, LANES), F32), jnp.zeros((tq, LANES), F32),
                 jnp.zeros((tq, 1), F32), jnp.zeros((tq, 1), F32))
        order = list(reversed(range(k_per_q)))
        state = k_span([q0 + d * tk for d in order], qms, state, [(col + d * tk) < row for d in order])

        def earlier(jj, state):
            k0s = [pl.multiple_of((qi - 1 - jj) * tq + d * tk, tk) for d in order]
            return k_span(k0s, qms, state, [None] * k_per_q)

        state = lax.fori_loop(0, qi, earlier, state)
        o_ref[pl.ds(q0, tq), :] = jnp.where(lane < HEAD_DIM, state[0], state[1]).astype(o_ref.dtype)
        return carry

    lax.fori_loop(0, nq, q_block, 0)


def _sb(q_b, k_b, v_b, batch, seq):
    n = q_b.shape[0]
    spec = pl.BlockSpec((seq, LANES), lambda b, p: (b, p))
    return pl.pallas_call(
        _sb_kernel,
        grid=(batch, SB_WIDTH // LANES),
        in_specs=[spec, spec, spec],
        out_specs=spec,
        out_shape=jax.ShapeDtypeStruct((n, SB_WIDTH), BF16),
        compiler_params=_params("parallel", "parallel"),
        name="stick_breaking",
    )(q_b, k_b, v_b)


def _merge_kernel(ya_ref, yb_ref, ga_ref, gb_ref, x_ref, pa_ref, pb_ref, wo_ref, h_ref):
    m = (ga_ref[...].astype(F32) * _dot(ya_ref[...], pa_ref[...])
         + gb_ref[...].astype(F32) * _dot(yb_ref[...], pb_ref[...]))
    h_ref[...] = x_ref[...] + _dot(m.astype(BF16), wo_ref[...])


def _merge(y_a, y_b, g_a, g_b, x2, p_a, p_b, w_o):
    n, d = x2.shape
    tm = TOKEN_TILE
    row = lambda w: pl.BlockSpec((tm, w), lambda i: (i, 0))
    full = lambda a: pl.BlockSpec(a.shape, lambda i: (0, 0))
    return pl.pallas_call(
        _merge_kernel,
        grid=(n // tm,),
        in_specs=[row(SWA_WIDTH), row(SB_WIDTH), row(d), row(d), row(d), full(p_a), full(p_b), full(w_o)],
        out_specs=row(d),
        out_shape=jax.ShapeDtypeStruct((n, d), F32),
        compiler_params=_params("parallel"),
        name="merge",
    )(y_a, y_b, g_a, g_b, x2, p_a, p_b, w_o)


META_EXPERT = 0
META_RANK = TOP_K
EXT_EXPERT = 0
EXT_GATE = TOP_K
GATE_PARTS = 3


def _router_kernel(h_ref, nw_ref, rw_ref, rb_ref, hx_ref, meta_ref, trun_ref, cnt_ref, run_ref):
    tm, d = h_ref.shape

    @pl.when(pl.program_id(0) == 0)
    def _():
        run_ref[...] = jnp.zeros_like(run_ref)

    hn = _rms(h_ref[...], nw_ref[...])
    hn_hi, hn_lo = _split_bf16(hn)
    both = _dot(hn_hi, rw_ref[...])
    logits = (both[:, :N_EXPERTS] + both[:, N_EXPERTS:] + _dot(hn_lo, rw_ref[:, :N_EXPERTS])
              + rb_ref[...])
    lane = lax.broadcasted_iota(jnp.int32, (tm, N_EXPERTS), 1).astype(F32)
    work = logits
    tops, idxs, hots = [], [], []
    for _ in range(TOP_K):
        m = jnp.max(work, axis=-1, keepdims=True)
        idx = jnp.min(jnp.where(work == m, lane, float(N_EXPERTS)), axis=-1, keepdims=True)
        hot = lane == idx
        work = jnp.where(hot, -jnp.inf, work)
        tops.append(m)
        idxs.append(idx)
        hots.append(hot)
    exps = [jnp.exp(t - tops[0]) for t in tops]
    den = exps[0] + exps[1] + exps[2] + exps[3]
    chosen = jnp.where(hots[0] | hots[1] | hots[2] | hots[3], 1.0, 0.0)
    r = lax.broadcasted_iota(jnp.int32, (tm, tm), 0)
    c = lax.broadcasted_iota(jnp.int32, (tm, tm), 1)
    before = jnp.where(c < r, 1.0, 0.0).astype(BF16)
    run = run_ref[...]
    prefix = _dot(before, chosen.astype(BF16)) + run
    mlane = lax.broadcasted_iota(jnp.int32, (tm, LANES), 1)
    meta = jnp.zeros((tm, LANES), F32)
    ext = jnp.zeros((tm, LANES), F32)
    for k in range(TOP_K):
        rank = jnp.sum(jnp.where(hots[k], prefix, 0.0), axis=-1, keepdims=True)
        meta = jnp.where(mlane == META_EXPERT + k, idxs[k], meta)
        meta = jnp.where(mlane == META_RANK + k, rank, meta)
        ext = jnp.where(mlane == EXT_EXPERT + k, idxs[k], ext)
        rest = exps[k] / den
        for part in range(GATE_PARTS):
            piece = rest.astype(BF16).astype(F32)
            ext = jnp.where(mlane == EXT_GATE + GATE_PARTS * k + part, piece, ext)
            rest = rest - piece
    meta_ref[...] = meta
    hx_ref[:, :d] = hn_hi
    hx_ref[:, d:] = ext.astype(BF16)
    trun_ref[0] = run
    tile_cnt = jnp.sum(chosen, axis=0, keepdims=True)
    run = run + jnp.ceil(tile_cnt / SUBLANES) * SUBLANES
    run_ref[...] = run
    cnt_ref[...] = run


def _router(h1, norm_w, router_w, router_b):
    n, d = h1.shape
    tm = TOKEN_TILE
    tiles = n // tm
    return pl.pallas_call(
        _router_kernel,
        grid=(tiles,),
        in_specs=[pl.BlockSpec((tm, d), lambda i: (i, 0)),
                  pl.BlockSpec((1, d), lambda i: (0, 0)),
                  pl.BlockSpec((d, 2 * N_EXPERTS), lambda i: (0, 0)),
                  pl.BlockSpec((1, N_EXPERTS), lambda i: (0, 0))],
        out_specs=[pl.BlockSpec((tm, d + LANES), lambda i: (i, 0)),
                   pl.BlockSpec((tm, LANES), lambda i: (i, 0)),
                   pl.BlockSpec((1, 1, N_EXPERTS), lambda i: (i, 0, 0)),
                   pl.BlockSpec((1, N_EXPERTS), lambda i: (0, 0))],
        out_shape=[jax.ShapeDtypeStruct((n, d + LANES), BF16),
                   jax.ShapeDtypeStruct((n, LANES), F32),
                   jax.ShapeDtypeStruct((tiles, 1, N_EXPERTS), F32),
                   jax.ShapeDtypeStruct((1, N_EXPERTS), F32)],
        scratch_shapes=[pltpu.VMEM((1, N_EXPERTS), F32)],
        compiler_params=_params("arbitrary"),
        name="router",
    )(h1, norm_w.reshape(1, d), jnp.concatenate(_split_bf16(router_w), axis=1),
      router_b.reshape(1, N_EXPERTS))


def _routing_tables(tile_run, counts, n):
    tiles = tile_run.shape[0]
    counts = counts.reshape(N_EXPERTS).astype(jnp.int32)
    run = tile_run.reshape(tiles, N_EXPERTS).astype(jnp.int32)
    tile_len = jnp.concatenate([run[1:], counts[None]], axis=0) - run
    loc = jnp.cumsum(tile_len, axis=1) - tile_len
    padded = (counts + FFN_BLOCK - 1) // FFN_BLOCK * FFN_BLOCK
    seg_end = jnp.cumsum(padded)
    seg_start = seg_end - padded
    dst0 = seg_start[None, :] + run
    delta = (loc - run).astype(F32).reshape(tiles, 1, N_EXPERTS)
    zero0 = seg_start + counts
    nzero = (padded - counts) // SUBLANES
    n_real = seg_end[-1] // FFN_BLOCK
    max_blocks = (n * TOP_K + tiles * N_EXPERTS * (SUBLANES - 1)
                  + N_EXPERTS * (FFN_BLOCK - 1)) // FFN_BLOCK
    blk_exp = jnp.minimum(jnp.sum(jnp.arange(max_blocks, dtype=jnp.int32)[:, None] * FFN_BLOCK
                                  >= seg_end[None, :], axis=1), N_EXPERTS - 1).astype(jnp.int32)
    return dict(dst0=dst0.reshape(-1), tile_len=tile_len.reshape(-1), loc=loc.reshape(-1),
                delta=delta, zero0=zero0, nzero=nzero, blk_exp=blk_exp,
                n_real=n_real.reshape(1).astype(jnp.int32), cap=max_blocks * FFN_BLOCK)


def _tile_rows(meta_ref, delta_ref):
    meta = meta_ref[...]
    tm = meta.shape[0]
    delta = delta_ref[0]
    lane = lax.broadcasted_iota(jnp.int32, (tm, N_EXPERTS), 1).astype(F32)
    rows = []
    for k in range(TOP_K):
        expert = meta[:, META_EXPERT + k:META_EXPERT + k + 1]
        rank = meta[:, META_RANK + k:META_RANK + k + 1]
        rows.append(rank + jnp.sum(jnp.where(lane == expert, delta, 0.0), axis=-1, keepdims=True))
    return rows


def _chunk_copies(t, tbl_ref, len_ref, loc_ref, make_copy, op):
    def one(seg_row, loc_row, rows):
        cp = make_copy(pl.multiple_of(seg_row, SUBLANES), pl.multiple_of(loc_row, SUBLANES), rows)
        cp.start() if op == "start" else cp.wait()

    for e in range(N_EXPERTS):
        idx = t * N_EXPERTS + e
        seg_row = tbl_ref[idx]
        loc_row = loc_ref[idx]
        length = len_ref[idx]
        full = length // CHUNK

        def body(c, carry):
            one(seg_row + c * CHUNK, loc_row + c * CHUNK, CHUNK)
            return carry

        lax.fori_loop(0, full, body, 0)
        done = full * CHUNK
        rows = CHUNK // 2
        while rows >= SUBLANES:
            @pl.when((length & rows) != 0)
            def _(done=done, rows=rows):
                one(seg_row + done, loc_row + done, rows)
            done = done + (length & rows)
            rows //= 2


def _dispatch_kernel(dst0_ref, len_ref, loc_ref, zero0_ref, nzero_ref, nreal_ref,
                     hx_ref, meta_ref, delta_ref, xb_ref, srt_ref, zero_ref, sem):
    t = pl.program_id(0)
    tm = hx_ref.shape[0]
    rows = _tile_rows(meta_ref, delta_ref)
    mlane = lax.broadcasted_iota(jnp.int32, (tm, LANES), 1)
    packed = jnp.full((tm, LANES), -1.0, F32)
    for k in range(TOP_K):
        packed = jnp.where(mlane == k, rows[k], packed)
    rows_t = packed.T
    prow = lax.broadcasted_iota(jnp.int32, (SORTED_ROWS, tm), 0).astype(F32)
    onehot = jnp.zeros((SORTED_ROWS, tm), F32)
    for k in range(TOP_K):
        onehot = jnp.where(prow == rows_t[k:k + 1, :], 1.0, onehot)
    slot = t % 2
    srt_ref[slot] = _dot(onehot.astype(BF16), hx_ref[...])

    def copy_from(s):
        def copy(seg_row, loc_row, rows):
            return pltpu.make_async_copy(srt_ref.at[s, pl.ds(loc_row, rows)],
                                         xb_ref.at[pl.ds(seg_row, rows)], sem.at[s])
        return copy

    @pl.when(t > 0)
    def _():
        _chunk_copies(t - 1, dst0_ref, len_ref, loc_ref, copy_from(1 - slot), "wait")

    _chunk_copies(t, dst0_ref, len_ref, loc_ref, copy_from(slot), "start")

    @pl.when(t == pl.num_programs(0) - 1)
    def _():
        _chunk_copies(t, dst0_ref, len_ref, loc_ref, copy_from(slot), "wait")
        zero_ref[...] = jnp.zeros_like(zero_ref)
        for op in ("start", "wait"):
            for e in range(N_EXPERTS):
                def body(c, carry):
                    row0 = pl.multiple_of(zero0_ref[e] + c * SUBLANES, SUBLANES)
                    cp = pltpu.make_async_copy(zero_ref.at[pl.ds(0, SUBLANES)],
                                               xb_ref.at[pl.ds(row0, SUBLANES)], sem.at[0])
                    cp.start() if op == "start" else cp.wait()
                    return carry

                lax.fori_loop(0, nzero_ref[e], body, 0)

            def tail(b, carry):
                row0 = pl.multiple_of(b * FFN_BLOCK, FFN_BLOCK)
                cp = pltpu.make_async_copy(zero_ref, xb_ref.at[pl.ds(row0, FFN_BLOCK)], sem.at[0])
                cp.start() if op == "start" else cp.wait()
                return carry

            lax.fori_loop(nreal_ref[0], xb_ref.shape[0] // FFN_BLOCK, tail, 0)


def _dispatch(hx, meta, tbl):
    n, width = hx.shape
    tm = TOKEN_TILE
    grid_spec = pltpu.PrefetchScalarGridSpec(
        num_scalar_prefetch=6,
        grid=(n // tm,),
        in_specs=[pl.BlockSpec((tm, width), lambda i, *_: (i, 0)),
                  pl.BlockSpec((tm, LANES), lambda i, *_: (i, 0)),
                  pl.BlockSpec((1, 1, N_EXPERTS), lambda i, *_: (i, 0, 0))],
        out_specs=pl.BlockSpec(memory_space=pl.ANY),
        scratch_shapes=[pltpu.VMEM((2, SORTED_ROWS, width), F32),
                        pltpu.VMEM((FFN_BLOCK, width), F32),
                        pltpu.SemaphoreType.DMA((2,))],
    )
    return pl.pallas_call(
        _dispatch_kernel,
        grid_spec=grid_spec,
        out_shape=jax.ShapeDtypeStruct((tbl["cap"], width), F32),
        compiler_params=_params("arbitrary"),
        name="dispatch",
    )(tbl["dst0"], tbl["tile_len"], tbl["loc"], tbl["zero0"], tbl["nzero"], tbl["n_real"],
      hx, meta, tbl["delta"])


def _ffn_kernel(exp_ref, nreal_ref, x_ref, wgu_ref, bgu_ref, wd_ref, bd_ref, y_ref, wgu_bf, wd_bf):
    b = pl.program_id(0)
    d_ff, d = wd_ref.shape[1], wd_ref.shape[2]

    @pl.when(b >= nreal_ref[0])
    def _():
        y_ref[...] = jnp.zeros_like(y_ref)

    @pl.when((b == 0) | (exp_ref[b] != exp_ref[jnp.maximum(b - 1, 0)]))
    def _():
        wgu_bf[...] = wgu_ref[0].astype(BF16)
        wd_bf[...] = wd_ref[0].astype(BF16)

    @pl.when(b < nreal_ref[0])
    def _():
        x = x_ref[...]
        ext = x[:, d:]
        expert = exp_ref[b].astype(F32)
        gate = jnp.zeros((x.shape[0], 1), F32)
        for k in range(TOP_K):
            g0 = EXT_GATE + GATE_PARTS * k
            g = ext[:, g0:g0 + 1] + ext[:, g0 + 1:g0 + 2] + ext[:, g0 + 2:g0 + 3]
            gate = gate + jnp.where(ext[:, EXT_EXPERT + k:EXT_EXPERT + k + 1] == expert, g, 0.0)
        gu = _dot(x[:, :d].astype(BF16), wgu_bf[...]) + bgu_ref[0]
        g = jnp.minimum(gu[:, :d_ff], SWIGLU_LIMIT)
        u = jnp.clip(gu[:, d_ff:], -SWIGLU_LIMIT, SWIGLU_LIMIT)
        act = (u + 1.0) * (g * jax.nn.sigmoid(SWIGLU_ALPHA * g))
        y_ref[...] = gate * (_dot(act.astype(BF16), wd_bf[...]) + bd_ref[0])


def _ffn(xb, tbl, w_gate_up, b_gate_up, w_down, b_down):
    cap, width = xb.shape
    e, d, two_ff = w_gate_up.shape
    d_ff = two_ff // 2
    grid_spec = pltpu.PrefetchScalarGridSpec(
        num_scalar_prefetch=2,
        grid=(cap // FFN_BLOCK,),
        in_specs=[pl.BlockSpec((FFN_BLOCK, width), lambda b, ex, nr: (b, 0)),
                  pl.BlockSpec((1, d, two_ff), lambda b, ex, nr: (ex[b], 0, 0)),
                  pl.BlockSpec((1, 1, two_ff), lambda b, ex, nr: (ex[b], 0, 0)),
                  pl.BlockSpec((1, d_ff, d), lambda b, ex, nr: (ex[b], 0, 0)),
                  pl.BlockSpec((1, 1, d), lambda b, ex, nr: (ex[b], 0, 0))],
        out_specs=pl.BlockSpec((FFN_BLOCK, d), lambda b, ex, nr: (b, 0)),
        scratch_shapes=[pltpu.VMEM((d, two_ff), BF16), pltpu.VMEM((d_ff, d), BF16)],
    )
    return pl.pallas_call(
        _ffn_kernel,
        grid_spec=grid_spec,
        out_shape=jax.ShapeDtypeStruct((cap, d), F32),
        compiler_params=_params("arbitrary"),
        name="expert_ffn",
    )(tbl["blk_exp"], tbl["n_real"], xb, w_gate_up,
      b_gate_up.reshape(e, 1, two_ff), w_down, b_down.reshape(e, 1, d))


def _combine_kernel(dst0_ref, len_ref, loc_ref, meta_ref, delta_ref, h_ref, y_ref, o_ref, srt_ref, sem):
    t = pl.program_id(0)
    tm = h_ref.shape[0]

    @pl.when(t == 0)
    def _():
        srt_ref[...] = jnp.zeros_like(srt_ref)

    slot = t % 2

    def copy_into(s):
        def copy(seg_row, loc_row, rows):
            return pltpu.make_async_copy(y_ref.at[pl.ds(seg_row, rows)],
                                         srt_ref.at[s, pl.ds(loc_row, rows)], sem.at[s])
        return copy

    @pl.when(t == 0)
    def _():
        _chunk_copies(t, dst0_ref, len_ref, loc_ref, copy_into(slot), "start")

    @pl.when(t + 1 < pl.num_programs(0))
    def _():
        _chunk_copies(t + 1, dst0_ref, len_ref, loc_ref, copy_into(1 - slot), "start")

    rows = _tile_rows(meta_ref, delta_ref)
    pcol = lax.broadcasted_iota(jnp.int32, (tm, SORTED_ROWS), 1).astype(F32)
    onehot = jnp.zeros((tm, SORTED_ROWS), F32)
    for k in range(TOP_K):
        onehot = jnp.where(pcol == rows[k], 1.0, onehot)
    onehot = onehot.astype(BF16)
    _chunk_copies(t, dst0_ref, len_ref, loc_ref, copy_into(slot), "wait")
    o_ref[...] = h_ref[...] + _dot(onehot, srt_ref[slot].astype(BF16))


def _combine(meta, tbl, h1, yb):
    n, d = h1.shape
    tm = TOKEN_TILE
    grid_spec = pltpu.PrefetchScalarGridSpec(
        num_scalar_prefetch=3,
        grid=(n // tm,),
        in_specs=[pl.BlockSpec((tm, LANES), lambda i, *_: (i, 0)),
                  pl.BlockSpec((1, 1, N_EXPERTS), lambda i, *_: (i, 0, 0)),
                  pl.BlockSpec((tm, d), lambda i, *_: (i, 0)),
                  pl.BlockSpec(memory_space=pl.ANY)],
        out_specs=pl.BlockSpec((tm, d), lambda i, *_: (i, 0)),
        scratch_shapes=[pltpu.VMEM((2, SORTED_ROWS, d), F32), pltpu.SemaphoreType.DMA((2,))],
    )
    return pl.pallas_call(
        _combine_kernel,
        grid_spec=grid_spec,
        out_shape=jax.ShapeDtypeStruct((n, d), F32),
        compiler_params=_params("arbitrary"),
        name="combine",
    )(tbl["dst0"], tbl["tile_len"], tbl["loc"], meta, tbl["delta"], h1, yb)


def kernel(x, attn_norm_w, w_in, q_norm_w, k_norm_w, sinks, w_proj_swa, w_proj_sb, w_out,
           ffn_norm_w, router_w, router_b, w_gate_up, b_gate_up, w_down, b_down):
    batch, seq, d = x.shape
    n = batch * seq
    h = x.reshape(n, d)
    for layer in range(attn_norm_w.shape[0]):
        q_a, kv_a, q_b, k_b, v_b, g_a, g_b = _in_proj(h, attn_norm_w[layer], w_in[layer].astype(BF16))
        y_a = _swa(q_a, kv_a, q_norm_w[layer], k_norm_w[layer], sinks[layer], batch, seq)
        y_b = _sb(q_b, k_b, v_b, batch, seq)
        h1 = _merge(y_a, y_b, g_a, g_b, h, w_proj_swa[layer].astype(BF16),
                    w_proj_sb[layer].astype(BF16), w_out[layer].astype(BF16))
        hx, meta, tile_run, counts = _router(h1, ffn_norm_w[layer], router_w[layer], router_b[layer])
        tbl = _routing_tables(tile_run, counts, n)
        xb = _dispatch(hx, meta, tbl)
        yb = _ffn(xb, tbl, w_gate_up[layer], b_gate_up[layer], w_down[layer], b_down[layer])
        h = _combine(meta, tbl, h1, yb)
    return h.reshape(batch, seq, d)
```

```python
import jax
import jax.numpy as jnp
from jax import lax
from jax.experimental import pallas as pl
from jax.experimental.pallas import tpu as pltpu

HEAD_DIM = 64
SWA_HEADS = 8
SWA_KV_HEADS = 2
SWA_GROUP = SWA_HEADS // SWA_KV_HEADS
BLOCK = 128
SB_HEADS = 8
SWA_WIDTH = SWA_HEADS * HEAD_DIM
SWA_KV_WIDTH = SWA_KV_HEADS * HEAD_DIM
SB_WIDTH = SB_HEADS * HEAD_DIM
N_EXPERTS = 32
TOP_K = 4
SWIGLU_LIMIT = 7.0
SWIGLU_ALPHA = 1.702
NORM_EPS = 1e-5
ATTN_SCALE = HEAD_DIM ** -0.5
SIGN_BIT = -2 ** 31

LANES = 128
SUBLANES = 8
TOKEN_TILE = 512
FFN_BLOCK = 256
CHUNK = 32
SORTED_ROWS = -(-(TOKEN_TILE * TOP_K + N_EXPERTS * (SUBLANES - 1)) // LANES) * LANES
SB_Q_ROWS = 1024
SB_K_ROWS = 256
VMEM_LIMIT = 56 * 1024 * 1024

F32 = jnp.float32
BF16 = jnp.bfloat16
NEG_BIG = -1e30


def _dot(a, b):
    return jnp.dot(a, b, preferred_element_type=F32)


def _dot_nt(a, b):
    return lax.dot_general(a, b, (((1,), (1,)), ((), ())), preferred_element_type=F32)


def _sigmoid(x):
    return jax.nn.sigmoid(x)


def _rms(x, w):
    return x * lax.rsqrt(jnp.mean(x * x, axis=-1, keepdims=True) + NORM_EPS) * w


def _params(*sem):
    return pltpu.CompilerParams(dimension_semantics=sem, vmem_limit_bytes=VMEM_LIMIT)


def _in_proj_kernel(x_ref, nw_ref, w_ref, qa_ref, kva_ref, qb_ref, kb_ref, vb_ref, ga_ref, gb_ref):
    xn = _rms(x_ref[...], nw_ref[...]).astype(BF16)
    off = 0
    for ref, gate in ((qa_ref, False), (kva_ref, False), (qb_ref, False), (kb_ref, False),
                      (vb_ref, False), (ga_ref, True), (gb_ref, True)):
        width = ref.shape[1]
        y = _dot(xn, w_ref[:, off:off + width])
        if gate:
            y = _sigmoid(y)
        ref[...] = y.astype(ref.dtype)
        off += width


def _in_proj(x2, norm_w, w_in):
    n, d = x2.shape
    widths = (SWA_WIDTH, 2 * SWA_KV_WIDTH, SB_WIDTH, SB_WIDTH, SB_WIDTH, d, d)
    tm = TOKEN_TILE
    return pl.pallas_call(
        _in_proj_kernel,
        grid=(n // tm,),
        in_specs=[pl.BlockSpec((tm, d), lambda i: (i, 0)),
                  pl.BlockSpec((1, d), lambda i: (0, 0)),
                  pl.BlockSpec(w_in.shape, lambda i: (0, 0))],
        out_specs=[pl.BlockSpec((tm, w), lambda i: (i, 0)) for w in widths],
        out_shape=[jax.ShapeDtypeStruct((n, w), BF16) for w in widths],
        compiler_params=_params("parallel"),
        name="in_proj",
    )(x2, norm_w.reshape(1, d), w_in)


def _swa_kernel(q_ref, kvc_ref, kvp_ref, qnw_ref, knw_ref, sink_ref, o_ref):
    i = pl.program_id(1)
    half = HEAD_DIM

    def group_mean_sq(x):
        w = x.shape[1]
        same_head = (lax.broadcasted_iota(jnp.int32, (w, w), 0) // half
                     == lax.broadcasted_iota(jnp.int32, (w, w), 1) // half)
        avg = jnp.where(same_head, 1.0 / half, 0.0).astype(BF16)
        hi, lo = _split_bf16(x * x)
        return _dot(hi, avg) + _dot(lo, avg)

    q = q_ref[...].astype(F32)
    kv = jnp.concatenate([kvp_ref[...], kvc_ref[...]], axis=0)
    k = kv[:, :SWA_KV_WIDTH].astype(F32)
    v = kv[:, SWA_KV_WIDTH:].astype(F32)
    qn = (q * lax.rsqrt(group_mean_sq(q) + NORM_EPS) * qnw_ref[...] * ATTN_SCALE).astype(BF16)
    kn = k * lax.rsqrt(group_mean_sq(k) + NORM_EPS) * knw_ref[...]
    lane = lax.broadcasted_iota(jnp.int32, (2 * BLOCK, LANES), 1)
    low = lane < half
    k_swap = pltpu.roll(kn, half, 1)
    v_swap = pltpu.roll(v, half, 1).astype(BF16)
    v_same = v.astype(BF16)
    keys = [[jnp.where(low, kn, 0.0).astype(BF16), jnp.where(low, 0.0, k_swap).astype(BF16)],
            [jnp.where(low, k_swap, 0.0).astype(BF16), jnp.where(low, 0.0, kn).astype(BF16)]]
    vals = [[v_same, v_swap], [v_swap, v_same]]

    row = lax.broadcasted_iota(jnp.int32, (BLOCK, 2 * BLOCK), 0)
    col = lax.broadcasted_iota(jnp.int32, (BLOCK, 2 * BLOCK), 1)
    rel = row + BLOCK - col
    valid = (rel >= 0) & (rel < BLOCK) & ((col >= BLOCK) | (i > 0))
    rel = rel.astype(F32)
    out_lane_low = lax.broadcasted_iota(jnp.int32, (BLOCK, LANES), 1) < half
    for blk in range(SWA_HEADS // 2):
        qb = qn[:, blk * LANES:(blk + 1) * LANES]
        halves = []
        for par in range(2):
            h = 2 * blk + par
            g = h // SWA_GROUP
            slope = 2.0 ** (-8.0 * (h + 1) / SWA_HEADS)
            s = jnp.where(valid, _dot_nt(qb, keys[g][par]) - slope * rel, NEG_BIG)
            sink = sink_ref[h]
            m = jnp.maximum(jnp.max(s, axis=-1, keepdims=True), sink)
            p = jnp.exp(s - m)
            den = jnp.sum(p, axis=-1, keepdims=True) + jnp.exp(sink - m)
            halves.append(_dot(p.astype(BF16), vals[g][par]) / den)
        o_ref[:, blk * LANES:(blk + 1) * LANES] = jnp.where(out_lane_low, halves[0], halves[1]).astype(o_ref.dtype)


def _swa(q_a, kv_a, q_norm_w, k_norm_w, sinks, batch, seq):
    n = q_a.shape[0]
    nb = seq // BLOCK
    return pl.pallas_call(
        _swa_kernel,
        grid=(batch, nb),
        in_specs=[pl.BlockSpec((BLOCK, SWA_WIDTH), lambda b, i: (b * nb + i, 0)),
                  pl.BlockSpec((BLOCK, 2 * SWA_KV_WIDTH), lambda b, i: (b * nb + i, 0)),
                  pl.BlockSpec((BLOCK, 2 * SWA_KV_WIDTH), lambda b, i: (b * nb + jnp.maximum(i - 1, 0), 0)),
                  pl.BlockSpec((1, SWA_WIDTH), lambda b, i: (0, 0)),
                  pl.BlockSpec((1, SWA_KV_WIDTH), lambda b, i: (0, 0)),
                  pl.BlockSpec(memory_space=pltpu.SMEM)],
        out_specs=pl.BlockSpec((BLOCK, SWA_WIDTH), lambda b, i: (b * nb + i, 0)),
        out_shape=jax.ShapeDtypeStruct((n, SWA_WIDTH), BF16),
        compiler_params=_params("parallel", "parallel"),
        name="swa",
    )(q_a, kv_a, kv_a, jnp.tile(q_norm_w, SWA_HEADS).reshape(1, SWA_WIDTH),
      jnp.tile(k_norm_w, SWA_KV_HEADS).reshape(1, SWA_KV_WIDTH), sinks)


def _split_bf16(x):
    hi = x.astype(BF16)
    lo = (x - hi.astype(F32)).astype(BF16)
    return hi, lo


def _sb_kernel(q_ref, k_ref, v_ref, o_ref):
    seq = q_ref.shape[0]
    tq, tk = SB_Q_ROWS, SB_K_ROWS
    nq = seq // tq
    k_per_q = tq // tk
    lane = lax.broadcasted_iota(jnp.int32, (tq, LANES), 1)
    row = lax.broadcasted_iota(jnp.int32, (tq, tk), 0)
    col = lax.broadcasted_iota(jnp.int32, (tq, tk), 1)
    neg_incl = jnp.where(lax.broadcasted_iota(jnp.int32, (tk, tk), 0)
                         >= lax.broadcasted_iota(jnp.int32, (tk, tk), 1), -1.0, 0.0).astype(BF16)

    def softplus(z):
        neg_abs = lax.bitcast_convert_type(lax.bitcast_convert_type(z, jnp.int32) | SIGN_BIT, F32)
        return jnp.maximum(z, 0.0) + jnp.log(1.0 + jnp.exp(neg_abs))

    def neg_suffix_sum(sp):
        return _dot(sp.astype(BF16), neg_incl)

    def add_from_row(x, r0, delta):
        return x + delta if r0 == 0 else jnp.concatenate([x[:r0], x[r0:] + delta], axis=0)

    def k_span(k0s, qms, state, masks, first_rows):
        pairs = [(i, p) for i in range(len(k0s)) for p in range(2)]
        kbs = [k_ref[pl.ds(k0, tk), :] for k0 in k0s]
        vbs = [v_ref[pl.ds(k0, tk), :] for k0 in k0s]
        zs = {(i, p): _dot_nt(qms[p][first_rows[i]:], kbs[i]) for i, p in pairs}
        sps = {ip: softplus(zs[ip]) for ip in pairs}
        for (i, p) in pairs:
            if masks[i] is not None:
                sps[(i, p)] = jnp.where(masks[i], sps[(i, p)], 0.0)
        cs = {ip: neg_suffix_sum(sps[ip]) for ip in pairs}
        accs, runs = list(state[:2]), list(state[2:])
        ws = {}
        for (i, p) in pairs:
            r0 = first_rows[i]
            w = jnp.exp(zs[(i, p)] + cs[(i, p)] + runs[p][r0:])
            ws[(i, p)] = w if masks[i] is None else jnp.where(masks[i], w, 0.0)
            runs[p] = add_from_row(runs[p], r0, cs[(i, p)][:, 0:1])
        for (i, p) in pairs:
            accs[p] = add_from_row(accs[p], first_rows[i], _dot(ws[(i, p)].astype(BF16), vbs[i]))
        return (*accs, *runs)

    def q_block(qi, carry):
        q0 = pl.multiple_of(qi * tq, tq)
        q2 = q_ref[pl.ds(q0, tq), :] * ATTN_SCALE
        qms = (jnp.where(lane < HEAD_DIM, q2, 0.0).astype(BF16),
               jnp.where(lane >= HEAD_DIM, q2, 0.0).astype(BF16))
        state = (jnp.zeros((tq, LANES), F32), jnp.zeros((tq, LANES), F32),
                 jnp.zeros((tq, 1), F32), jnp.zeros((tq, 1), F32))
        order = list(reversed(range(k_per_q)))
        state = k_span([q0 + d * tk for d in order], qms, state,
                       [((col + d * tk) < row)[d * tk:] for d in order], [d * tk for d in order])

        def earlier(jj, state):
            k0s = [pl.multiple_of((qi - 1 - jj) * tq + d * tk, tk) for d in order]
            return k_span(k0s, qms, state, [None] * k_per_q, [0] * k_per_q)

        state = lax.fori_loop(0, qi, earlier, state)
        o_ref[pl.ds(q0, tq), :] = jnp.where(lane < HEAD_DIM, state[0], state[1]).astype(o_ref.dtype)
        return carry

    lax.fori_loop(0, nq, q_block, 0)


def _sb(q_b, k_b, v_b, batch, seq):
    n = q_b.shape[0]
    spec = pl.BlockSpec((seq, LANES), lambda b, p: (b, p))
    return pl.pallas_call(
        _sb_kernel,
        grid=(batch, SB_WIDTH // LANES),
        in_specs=[spec, spec, spec],
        out_specs=spec,
        out_shape=jax.ShapeDtypeStruct((n, SB_WIDTH), BF16),
        compiler_params=_params("parallel", "parallel"),
        name="stick_breaking",
    )(q_b, k_b, v_b)


def _merge_kernel(ya_ref, yb_ref, ga_ref, gb_ref, x_ref, pa_ref, pb_ref, wo_ref, h_ref):
    m = (ga_ref[...].astype(F32) * _dot(ya_ref[...], pa_ref[...])
         + gb_ref[...].astype(F32) * _dot(yb_ref[...], pb_ref[...]))
    h_ref[...] = x_ref[...] + _dot(m.astype(BF16), wo_ref[...])


def _merge(y_a, y_b, g_a, g_b, x2, p_a, p_b, w_o):
    n, d = x2.shape
    tm = TOKEN_TILE
    row = lambda w: pl.BlockSpec((tm, w), lambda i: (i, 0))
    full = lambda a: pl.BlockSpec(a.shape, lambda i: (0, 0))
    return pl.pallas_call(
        _merge_kernel,
        grid=(n // tm,),
        in_specs=[row(SWA_WIDTH), row(SB_WIDTH), row(d), row(d), row(d), full(p_a), full(p_b), full(w_o)],
        out_specs=row(d),
        out_shape=jax.ShapeDtypeStruct((n, d), F32),
        compiler_params=_params("parallel"),
        name="merge",
    )(y_a, y_b, g_a, g_b, x2, p_a, p_b, w_o)


META_EXPERT = 0
META_RANK = TOP_K
EXT_EXPERT = 0
EXT_GATE = TOP_K
GATE_PARTS = 3


def _router_kernel(h_ref, nw_ref, rw_ref, rb_ref, hx_ref, meta_ref, trun_ref, cnt_ref, run_ref):
    tm, d = h_ref.shape

    @pl.when(pl.program_id(0) == 0)
    def _():
        run_ref[...] = jnp.zeros_like(run_ref)

    hn = _rms(h_ref[...], nw_ref[...])
    hn_hi, hn_lo = _split_bf16(hn)
    both = _dot(hn_hi, rw_ref[...])
    logits = (both[:, :N_EXPERTS] + both[:, N_EXPERTS:] + _dot(hn_lo, rw_ref[:, :N_EXPERTS])
              + rb_ref[...])
    lane = lax.broadcasted_iota(jnp.int32, (tm, N_EXPERTS), 1).astype(F32)
    work = logits
    tops, idxs, hots = [], [], []
    for _ in range(TOP_K):
        m = jnp.max(work, axis=-1, keepdims=True)
        idx = jnp.min(jnp.where(work == m, lane, float(N_EXPERTS)), axis=-1, keepdims=True)
        hot = lane == idx
        work = jnp.where(hot, -jnp.inf, work)
        tops.append(m)
        idxs.append(idx)
        hots.append(hot)
    exps = [jnp.exp(t - tops[0]) for t in tops]
    den = exps[0] + exps[1] + exps[2] + exps[3]
    chosen = jnp.where(hots[0] | hots[1] | hots[2] | hots[3], 1.0, 0.0)
    r = lax.broadcasted_iota(jnp.int32, (tm, tm), 0)
    c = lax.broadcasted_iota(jnp.int32, (tm, tm), 1)
    before = jnp.where(c < r, 1.0, 0.0).astype(BF16)
    run = run_ref[...]
    prefix = _dot(before, chosen.astype(BF16)) + run
    mlane = lax.broadcasted_iota(jnp.int32, (tm, LANES), 1)
    meta = jnp.zeros((tm, LANES), F32)
    ext = jnp.zeros((tm, LANES), F32)
    for k in range(TOP_K):
        rank = jnp.sum(jnp.where(hots[k], prefix, 0.0), axis=-1, keepdims=True)
        meta = jnp.where(mlane == META_EXPERT + k, idxs[k], meta)
        meta = jnp.where(mlane == META_RANK + k, rank, meta)
        ext = jnp.where(mlane == EXT_EXPERT + k, idxs[k], ext)
        rest = exps[k] / den
        for part in range(GATE_PARTS):
            piece = rest.astype(BF16).astype(F32)
            ext = jnp.where(mlane == EXT_GATE + GATE_PARTS * k + part, piece, ext)
            rest = rest - piece
    meta_ref[...] = meta
    hx_ref[:, :d] = hn_hi
    hx_ref[:, d:] = ext.astype(BF16)
    trun_ref[0] = run
    tile_cnt = jnp.sum(chosen, axis=0, keepdims=True)
    run = run + jnp.ceil(tile_cnt / SUBLANES) * SUBLANES
    run_ref[...] = run
    cnt_ref[...] = run


def _router(h1, norm_w, router_w, router_b):
    n, d = h1.shape
    tm = TOKEN_TILE
    tiles = n // tm
    return pl.pallas_call(
        _router_kernel,
        grid=(tiles,),
        in_specs=[pl.BlockSpec((tm, d), lambda i: (i, 0)),
                  pl.BlockSpec((1, d), lambda i: (0, 0)),
                  pl.BlockSpec((d, 2 * N_EXPERTS), lambda i: (0, 0)),
                  pl.BlockSpec((1, N_EXPERTS), lambda i: (0, 0))],
        out_specs=[pl.BlockSpec((tm, d + LANES), lambda i: (i, 0)),
                   pl.BlockSpec((tm, LANES), lambda i: (i, 0)),
                   pl.BlockSpec((1, 1, N_EXPERTS), lambda i: (i, 0, 0)),
                   pl.BlockSpec((1, N_EXPERTS), lambda i: (0, 0))],
        out_shape=[jax.ShapeDtypeStruct((n, d + LANES), BF16),
                   jax.ShapeDtypeStruct((n, LANES), F32),
                   jax.ShapeDtypeStruct((tiles, 1, N_EXPERTS), F32),
                   jax.ShapeDtypeStruct((1, N_EXPERTS), F32)],
        scratch_shapes=[pltpu.VMEM((1, N_EXPERTS), F32)],
        compiler_params=_params("arbitrary"),
        name="router",
    )(h1, norm_w.reshape(1, d), jnp.concatenate(_split_bf16(router_w), axis=1),
      router_b.reshape(1, N_EXPERTS))


def _routing_tables(tile_run, counts, n):
    tiles = tile_run.shape[0]
    counts = counts.reshape(N_EXPERTS).astype(jnp.int32)
    run = tile_run.reshape(tiles, N_EXPERTS).astype(jnp.int32)
    tile_len = jnp.concatenate([run[1:], counts[None]], axis=0) - run
    loc = jnp.cumsum(tile_len, axis=1) - tile_len
    padded = (counts + FFN_BLOCK - 1) // FFN_BLOCK * FFN_BLOCK
    seg_end = jnp.cumsum(padded)
    seg_start = seg_end - padded
    dst0 = seg_start[None, :] + run
    delta = (loc - run).astype(F32).reshape(tiles, 1, N_EXPERTS)
    zero0 = seg_start + counts
    nzero = (padded - counts) // SUBLANES
    n_real = seg_end[-1] // FFN_BLOCK
    max_blocks = (n * TOP_K + tiles * N_EXPERTS * (SUBLANES - 1)
                  + N_EXPERTS * (FFN_BLOCK - 1)) // FFN_BLOCK
    blk_exp = jnp.minimum(jnp.sum(jnp.arange(max_blocks, dtype=jnp.int32)[:, None] * FFN_BLOCK
                                  >= seg_end[None, :], axis=1), N_EXPERTS - 1).astype(jnp.int32)
    return dict(dst0=dst0.reshape(-1), tile_len=tile_len.reshape(-1), loc=loc.reshape(-1),
                delta=delta, zero0=zero0, nzero=nzero, blk_exp=blk_exp,
                n_real=n_real.reshape(1).astype(jnp.int32), cap=max_blocks * FFN_BLOCK)


def _tile_rows(meta_ref, delta_ref):
    meta = meta_ref[...]
    tm = meta.shape[0]
    delta = delta_ref[0]
    lane = lax.broadcasted_iota(jnp.int32, (tm, N_EXPERTS), 1).astype(F32)
    rows = []
    for k in range(TOP_K):
        expert = meta[:, META_EXPERT + k:META_EXPERT + k + 1]
        rank = meta[:, META_RANK + k:META_RANK + k + 1]
        rows.append(rank + jnp.sum(jnp.where(lane == expert, delta, 0.0), axis=-1, keepdims=True))
    return rows


def _chunk_copies(t, tbl_ref, len_ref, loc_ref, make_copy, op):
    def one(seg_row, loc_row, rows):
        cp = make_copy(pl.multiple_of(seg_row, SUBLANES), pl.multiple_of(loc_row, SUBLANES), rows)
        cp.start() if op == "start" else cp.wait()

    for e in range(N_EXPERTS):
        idx = t * N_EXPERTS + e
        seg_row = tbl_ref[idx]
        loc_row = loc_ref[idx]
        length = len_ref[idx]
        full = length // CHUNK

        def body(c, carry):
            one(seg_row + c * CHUNK, loc_row + c * CHUNK, CHUNK)
            return carry

        lax.fori_loop(0, full, body, 0)
        done = full * CHUNK
        rows = CHUNK // 2
        while rows >= SUBLANES:
            @pl.when((length & rows) != 0)
            def _(done=done, rows=rows):
                one(seg_row + done, loc_row + done, rows)
            done = done + (length & rows)
            rows //= 2


def _dispatch_kernel(dst0_ref, len_ref, loc_ref, zero0_ref, nzero_ref, nreal_ref,
                     hx_ref, meta_ref, delta_ref, xb_ref, srt_ref, zero_ref, sem):
    t = pl.program_id(0)
    tm = hx_ref.shape[0]
    rows = _tile_rows(meta_ref, delta_ref)
    mlane = lax.broadcasted_iota(jnp.int32, (tm, LANES), 1)
    packed = jnp.full((tm, LANES), -1.0, F32)
    for k in range(TOP_K):
        packed = jnp.where(mlane == k, rows[k], packed)
    rows_t = packed.T
    prow = lax.broadcasted_iota(jnp.int32, (SORTED_ROWS, tm), 0).astype(F32)
    onehot = jnp.zeros((SORTED_ROWS, tm), F32)
    for k in range(TOP_K):
        onehot = jnp.where(prow == rows_t[k:k + 1, :], 1.0, onehot)
    slot = t % 2
    srt_ref[slot] = _dot(onehot.astype(BF16), hx_ref[...])

    def copy_from(s):
        def copy(seg_row, loc_row, rows):
            return pltpu.make_async_copy(srt_ref.at[s, pl.ds(loc_row, rows)],
                                         xb_ref.at[pl.ds(seg_row, rows)], sem.at[s])
        return copy

    @pl.when(t > 0)
    def _():
        _chunk_copies(t - 1, dst0_ref, len_ref, loc_ref, copy_from(1 - slot), "wait")

    _chunk_copies(t, dst0_ref, len_ref, loc_ref, copy_from(slot), "start")

    @pl.when(t == pl.num_programs(0) - 1)
    def _():
        _chunk_copies(t, dst0_ref, len_ref, loc_ref, copy_from(slot), "wait")
        zero_ref[...] = jnp.zeros_like(zero_ref)
        for op in ("start", "wait"):
            for e in range(N_EXPERTS):
                def body(c, carry):
                    row0 = pl.multiple_of(zero0_ref[e] + c * SUBLANES, SUBLANES)
                    cp = pltpu.make_async_copy(zero_ref.at[pl.ds(0, SUBLANES)],
                                               xb_ref.at[pl.ds(row0, SUBLANES)], sem.at[0])
                    cp.start() if op == "start" else cp.wait()
                    return carry

                lax.fori_loop(0, nzero_ref[e], body, 0)

            def tail(b, carry):
                row0 = pl.multiple_of(b * FFN_BLOCK, FFN_BLOCK)
                cp = pltpu.make_async_copy(zero_ref, xb_ref.at[pl.ds(row0, FFN_BLOCK)], sem.at[0])
                cp.start() if op == "start" else cp.wait()
                return carry

            lax.fori_loop(nreal_ref[0], xb_ref.shape[0] // FFN_BLOCK, tail, 0)


def _dispatch(hx, meta, tbl):
    n, width = hx.shape
    tm = TOKEN_TILE
    grid_spec = pltpu.PrefetchScalarGridSpec(
        num_scalar_prefetch=6,
        grid=(n // tm,),
        in_specs=[pl.BlockSpec((tm, width), lambda i, *_: (i, 0)),
                  pl.BlockSpec((tm, LANES), lambda i, *_: (i, 0)),
                  pl.BlockSpec((1, 1, N_EXPERTS), lambda i, *_: (i, 0, 0))],
        out_specs=pl.BlockSpec(memory_space=pl.ANY),
        scratch_shapes=[pltpu.VMEM((2, SORTED_ROWS, width), F32),
                        pltpu.VMEM((FFN_BLOCK, width), F32),
                        pltpu.SemaphoreType.DMA((2,))],
    )
    return pl.pallas_call(
        _dispatch_kernel,
        grid_spec=grid_spec,
        out_shape=jax.ShapeDtypeStruct((tbl["cap"], width), F32),
        compiler_params=_params("arbitrary"),
        name="dispatch",
    )(tbl["dst0"], tbl["tile_len"], tbl["loc"], tbl["zero0"], tbl["nzero"], tbl["n_real"],
      hx, meta, tbl["delta"])


def _ffn_kernel(exp_ref, nreal_ref, x_ref, wgu_ref, bgu_ref, wd_ref, bd_ref, y_ref, wgu_bf, wd_bf):
    b = pl.program_id(0)
    d_ff, d = wd_ref.shape[1], wd_ref.shape[2]

    @pl.when(b >= nreal_ref[0])
    def _():
        y_ref[...] = jnp.zeros_like(y_ref)

    @pl.when((b == 0) | (exp_ref[b] != exp_ref[jnp.maximum(b - 1, 0)]))
    def _():
        wgu_bf[...] = wgu_ref[0].astype(BF16)
        wd_bf[...] = wd_ref[0].astype(BF16)

    @pl.when(b < nreal_ref[0])
    def _():
        x = x_ref[...]
        ext = x[:, d:]
        expert = exp_ref[b].astype(F32)
        gate = jnp.zeros((x.shape[0], 1), F32)
        for k in range(TOP_K):
            g0 = EXT_GATE + GATE_PARTS * k
            g = ext[:, g0:g0 + 1] + ext[:, g0 + 1:g0 + 2] + ext[:, g0 + 2:g0 + 3]
            gate = gate + jnp.where(ext[:, EXT_EXPERT + k:EXT_EXPERT + k + 1] == expert, g, 0.0)
        gu = _dot(x[:, :d].astype(BF16), wgu_bf[...]) + bgu_ref[0]
        g = jnp.minimum(gu[:, :d_ff], SWIGLU_LIMIT)
        u = jnp.clip(gu[:, d_ff:], -SWIGLU_LIMIT, SWIGLU_LIMIT)
        act = (u + 1.0) * (g * jax.nn.sigmoid(SWIGLU_ALPHA * g))
        y_ref[...] = gate * (_dot(act.astype(BF16), wd_bf[...]) + bd_ref[0])


def _ffn(xb, tbl, w_gate_up, b_gate_up, w_down, b_down):
    cap, width = xb.shape
    e, d, two_ff = w_gate_up.shape
    d_ff = two_ff // 2
    grid_spec = pltpu.PrefetchScalarGridSpec(
        num_scalar_prefetch=2,
        grid=(cap // FFN_BLOCK,),
        in_specs=[pl.BlockSpec((FFN_BLOCK, width), lambda b, ex, nr: (b, 0)),
                  pl.BlockSpec((1, d, two_ff), lambda b, ex, nr: (ex[b], 0, 0)),
                  pl.BlockSpec((1, 1, two_ff), lambda b, ex, nr: (ex[b], 0, 0)),
                  pl.BlockSpec((1, d_ff, d), lambda b, ex, nr: (ex[b], 0, 0)),
                  pl.BlockSpec((1, 1, d), lambda b, ex, nr: (ex[b], 0, 0))],
        out_specs=pl.BlockSpec((FFN_BLOCK, d), lambda b, ex, nr: (b, 0)),
        scratch_shapes=[pltpu.VMEM((d, two_ff), BF16), pltpu.VMEM((d_ff, d), BF16)],
    )
    return pl.pallas_call(
        _ffn_kernel,
        grid_spec=grid_spec,
        out_shape=jax.ShapeDtypeStruct((cap, d), F32),
        compiler_params=_params("arbitrary"),
        name="expert_ffn",
    )(tbl["blk_exp"], tbl["n_real"], xb, w_gate_up,
      b_gate_up.reshape(e, 1, two_ff), w_down, b_down.reshape(e, 1, d))


def _combine_kernel(dst0_ref, len_ref, loc_ref, meta_ref, delta_ref, h_ref, y_ref, o_ref, srt_ref, sem):
    t = pl.program_id(0)
    tm = h_ref.shape[0]

    @pl.when(t == 0)
    def _():
        srt_ref[...] = jnp.zeros_like(srt_ref)

    slot = t % 2

    def copy_into(s):
        def copy(seg_row, loc_row, rows):
            return pltpu.make_async_copy(y_ref.at[pl.ds(seg_row, rows)],
                                         srt_ref.at[s, pl.ds(loc_row, rows)], sem.at[s])
        return copy

    @pl.when(t == 0)
    def _():
        _chunk_copies(t, dst0_ref, len_ref, loc_ref, copy_into(slot), "start")

    @pl.when(t + 1 < pl.num_programs(0))
    def _():
        _chunk_copies(t + 1, dst0_ref, len_ref, loc_ref, copy_into(1 - slot), "start")

    rows = _tile_rows(meta_ref, delta_ref)
    pcol = lax.broadcasted_iota(jnp.int32, (tm, SORTED_ROWS), 1).astype(F32)
    onehot = jnp.zeros((tm, SORTED_ROWS), F32)
    for k in range(TOP_K):
        onehot = jnp.where(pcol == rows[k], 1.0, onehot)
    onehot = onehot.astype(BF16)
    _chunk_copies(t, dst0_ref, len_ref, loc_ref, copy_into(slot), "wait")
    o_ref[...] = h_ref[...] + _dot(onehot, srt_ref[slot].astype(BF16))


def _combine(meta, tbl, h1, yb):
    n, d = h1.shape
    tm = TOKEN_TILE
    grid_spec = pltpu.PrefetchScalarGridSpec(
        num_scalar_prefetch=3,
        grid=(n // tm,),
        in_specs=[pl.BlockSpec((tm, LANES), lambda i, *_: (i, 0)),
                  pl.BlockSpec((1, 1, N_EXPERTS), lambda i, *_: (i, 0, 0)),
                  pl.BlockSpec((tm, d), lambda i, *_: (i, 0)),
                  pl.BlockSpec(memory_space=pl.ANY)],
        out_specs=pl.BlockSpec((tm, d), lambda i, *_: (i, 0)),
        scratch_shapes=[pltpu.VMEM((2, SORTED_ROWS, d), F32), pltpu.SemaphoreType.DMA((2,))],
    )
    return pl.pallas_call(
        _combine_kernel,
        grid_spec=grid_spec,
        out_shape=jax.ShapeDtypeStruct((n, d), F32),
        compiler_params=_params("arbitrary"),
        name="combine",
    )(tbl["dst0"], tbl["tile_len"], tbl["loc"], meta, tbl["delta"], h1, yb)


def kernel(x, attn_norm_w, w_in, q_norm_w, k_norm_w, sinks, w_proj_swa, w_proj_sb, w_out,
           ffn_norm_w, router_w, router_b, w_gate_up, b_gate_up, w_down, b_down):
    batch, seq, d = x.shape
    n = batch * seq
    h = x.reshape(n, d)
    for layer in range(attn_norm_w.shape[0]):
        q_a, kv_a, q_b, k_b, v_b, g_a, g_b = _in_proj(h, attn_norm_w[layer], w_in[layer].astype(BF16))
        y_a = _swa(q_a, kv_a, q_norm_w[layer], k_norm_w[layer], sinks[layer], batch, seq)
        y_b = _sb(q_b, k_b, v_b, batch, seq)
        h1 = _merge(y_a, y_b, g_a, g_b, h, w_proj_swa[layer].astype(BF16),
                    w_proj_sb[layer].astype(BF16), w_out[layer].astype(BF16))
        hx, meta, tile_run, counts = _router(h1, ffn_norm_w[layer], router_w[layer], router_b[layer])
        tbl = _routing_tables(tile_run, counts, n)
        xb = _dispatch(hx, meta, tbl)
        yb = _ffn(xb, tbl, w_gate_up[layer], b_gate_up[layer], w_down[layer], b_down[layer])
        h = _combine(meta, tbl, h1, yb)
    return h.reshape(batch, seq, d)
```

```python
import jax
import jax.numpy as jnp
from jax import lax
from jax.experimental import pallas as pl
from jax.experimental.pallas import tpu as pltpu

HEAD_DIM = 64
SWA_HEADS = 8
SWA_KV_HEADS = 2
SWA_GROUP = SWA_HEADS // SWA_KV_HEADS
BLOCK = 128
SB_HEADS = 8
SWA_WIDTH = SWA_HEADS * HEAD_DIM
SWA_KV_WIDTH = SWA_KV_HEADS * HEAD_DIM
SB_WIDTH = SB_HEADS * HEAD_DIM
N_EXPERTS = 32
TOP_K = 4
SWIGLU_LIMIT = 7.0
SWIGLU_ALPHA = 1.702
NORM_EPS = 1e-5
ATTN_SCALE = HEAD_DIM ** -0.5
SIGN_BIT = -2 ** 31

LANES = 128
SUBLANES = 8
TOKEN_TILE = 512
FFN_BLOCK = 512
CHUNK = 32
SORTED_ROWS = -(-(TOKEN_TILE * TOP_K + N_EXPERTS * (SUBLANES - 1)) // LANES) * LANES
SORT_PARTS = 2
SB_Q_ROWS = 1024
SB_K_ROWS = 256
VMEM_LIMIT = 56 * 1024 * 1024

F32 = jnp.float32
BF16 = jnp.bfloat16
NEG_BIG = -1e30


def _dot(a, b):
    return jnp.dot(a, b, preferred_element_type=F32)


def _dot_nt(a, b):
    return lax.dot_general(a, b, (((1,), (1,)), ((), ())), preferred_element_type=F32)


def _sigmoid(x):
    return jax.nn.sigmoid(x)


def _rms(x, w):
    return x * lax.rsqrt(jnp.mean(x * x, axis=-1, keepdims=True) + NORM_EPS) * w


def _params(*sem):
    return pltpu.CompilerParams(dimension_semantics=sem, vmem_limit_bytes=VMEM_LIMIT)


def _in_proj_kernel(x_ref, nw_ref, w_ref, qa_ref, kva_ref, qb_ref, kb_ref, vb_ref, ga_ref, gb_ref):
    xn = _rms(x_ref[...], nw_ref[...]).astype(BF16)
    off = 0
    for ref, gate in ((qa_ref, False), (kva_ref, False), (qb_ref, False), (kb_ref, False),
                      (vb_ref, False), (ga_ref, True), (gb_ref, True)):
        width = ref.shape[1]
        y = _dot(xn, w_ref[:, off:off + width])
        if gate:
            y = _sigmoid(y)
        ref[...] = y.astype(ref.dtype)
        off += width


def _in_proj(x2, norm_w, w_in):
    n, d = x2.shape
    widths = (SWA_WIDTH, 2 * SWA_KV_WIDTH, SB_WIDTH, SB_WIDTH, SB_WIDTH, d, d)
    tm = TOKEN_TILE
    return pl.pallas_call(
        _in_proj_kernel,
        grid=(n // tm,),
        in_specs=[pl.BlockSpec((tm, d), lambda i: (i, 0)),
                  pl.BlockSpec((1, d), lambda i: (0, 0)),
                  pl.BlockSpec(w_in.shape, lambda i: (0, 0))],
        out_specs=[pl.BlockSpec((tm, w), lambda i: (i, 0)) for w in widths],
        out_shape=[jax.ShapeDtypeStruct((n, w), BF16) for w in widths],
        compiler_params=_params("parallel"),
        name="in_proj",
    )(x2, norm_w.reshape(1, d), w_in)


def _swa_kernel(q_ref, kvc_ref, kvp_ref, qnw_ref, knw_ref, sink_ref, o_ref):
    i = pl.program_id(1)
    half = HEAD_DIM

    def group_mean_sq(x):
        w = x.shape[1]
        same_head = (lax.broadcasted_iota(jnp.int32, (w, w), 0) // half
                     == lax.broadcasted_iota(jnp.int32, (w, w), 1) // half)
        avg = jnp.where(same_head, 1.0 / half, 0.0).astype(BF16)
        hi, lo = _split_bf16(x * x)
        return _dot(hi, avg) + _dot(lo, avg)

    q = q_ref[...].astype(F32)
    kv = jnp.concatenate([kvp_ref[...], kvc_ref[...]], axis=0)
    k = kv[:, :SWA_KV_WIDTH].astype(F32)
    v = kv[:, SWA_KV_WIDTH:].astype(F32)
    qn = (q * lax.rsqrt(group_mean_sq(q) + NORM_EPS) * qnw_ref[...] * ATTN_SCALE).astype(BF16)
    kn = k * lax.rsqrt(group_mean_sq(k) + NORM_EPS) * knw_ref[...]
    lane = lax.broadcasted_iota(jnp.int32, (2 * BLOCK, LANES), 1)
    low = lane < half
    k_swap = pltpu.roll(kn, half, 1)
    v_swap = pltpu.roll(v, half, 1).astype(BF16)
    v_same = v.astype(BF16)
    keys = [[jnp.where(low, kn, 0.0).astype(BF16), jnp.where(low, 0.0, k_swap).astype(BF16)],
            [jnp.where(low, k_swap, 0.0).astype(BF16), jnp.where(low, 0.0, kn).astype(BF16)]]
    vals = [[v_same, v_swap], [v_swap, v_same]]

    row = lax.broadcasted_iota(jnp.int32, (BLOCK, 2 * BLOCK), 0)
    col = lax.broadcasted_iota(jnp.int32, (BLOCK, 2 * BLOCK), 1)
    rel = row + BLOCK - col
    valid = (rel >= 0) & (rel < BLOCK) & ((col >= BLOCK) | (i > 0))
    rel = rel.astype(F32)
    out_lane_low = lax.broadcasted_iota(jnp.int32, (BLOCK, LANES), 1) < half
    for blk in range(SWA_HEADS // 2):
        qb = qn[:, blk * LANES:(blk + 1) * LANES]
        halves = []
        for par in range(2):
            h = 2 * blk + par
            g = h // SWA_GROUP
            slope = 2.0 ** (-8.0 * (h + 1) / SWA_HEADS)
            s = jnp.where(valid, _dot_nt(qb, keys[g][par]) - slope * rel, NEG_BIG)
            sink = sink_ref[h]
            m = jnp.maximum(jnp.max(s, axis=-1, keepdims=True), sink)
            p = jnp.exp(s - m)
            den = jnp.sum(p, axis=-1, keepdims=True) + jnp.exp(sink - m)
            halves.append(_dot(p.astype(BF16), vals[g][par]) / den)
        o_ref[:, blk * LANES:(blk + 1) * LANES] = jnp.where(out_lane_low, halves[0], halves[1]).astype(o_ref.dtype)


def _swa(q_a, kv_a, q_norm_w, k_norm_w, sinks, batch, seq):
    n = q_a.shape[0]
    nb = seq // BLOCK
    return pl.pallas_call(
        _swa_kernel,
        grid=(batch, nb),
        in_specs=[pl.BlockSpec((BLOCK, SWA_WIDTH), lambda b, i: (b * nb + i, 0)),
                  pl.BlockSpec((BLOCK, 2 * SWA_KV_WIDTH), lambda b, i: (b * nb + i, 0)),
                  pl.BlockSpec((BLOCK, 2 * SWA_KV_WIDTH), lambda b, i: (b * nb + jnp.maximum(i - 1, 0), 0)),
                  pl.BlockSpec((1, SWA_WIDTH), lambda b, i: (0, 0)),
                  pl.BlockSpec((1, SWA_KV_WIDTH), lambda b, i: (0, 0)),
                  pl.BlockSpec(memory_space=pltpu.SMEM)],
        out_specs=pl.BlockSpec((BLOCK, SWA_WIDTH), lambda b, i: (b * nb + i, 0)),
        out_shape=jax.ShapeDtypeStruct((n, SWA_WIDTH), BF16),
        compiler_params=_params("parallel", "parallel"),
        name="swa",
    )(q_a, kv_a, kv_a, jnp.tile(q_norm_w, SWA_HEADS).reshape(1, SWA_WIDTH),
      jnp.tile(k_norm_w, SWA_KV_HEADS).reshape(1, SWA_KV_WIDTH), sinks)


def _split_bf16(x):
    hi = x.astype(BF16)
    lo = (x - hi.astype(F32)).astype(BF16)
    return hi, lo


def _sb_kernel(q_ref, k_ref, v_ref, o_ref):
    seq = q_ref.shape[0]
    tq, tk = SB_Q_ROWS, SB_K_ROWS
    nq = seq // tq
    k_per_q = tq // tk
    lane = lax.broadcasted_iota(jnp.int32, (tq, LANES), 1)
    row = lax.broadcasted_iota(jnp.int32, (tq, tk), 0)
    col = lax.broadcasted_iota(jnp.int32, (tq, tk), 1)
    neg_incl = jnp.where(lax.broadcasted_iota(jnp.int32, (tk, tk), 0)
                         >= lax.broadcasted_iota(jnp.int32, (tk, tk), 1), -1.0, 0.0).astype(BF16)

    def softplus(z):
        neg_abs = lax.bitcast_convert_type(lax.bitcast_convert_type(z, jnp.int32) | SIGN_BIT, F32)
        return jnp.maximum(z, 0.0) + jnp.log(1.0 + jnp.exp(neg_abs))

    def neg_suffix_sum(sp):
        return _dot(sp.astype(BF16), neg_incl)

    def add_from_row(x, r0, delta):
        return x + delta if r0 == 0 else jnp.concatenate([x[:r0], x[r0:] + delta], axis=0)

    def k_span(k0s, qms, state, masks, first_rows):
        pairs = [(i, p) for i in range(len(k0s)) for p in range(2)]
        kbs = [k_ref[pl.ds(k0, tk), :] for k0 in k0s]
        vbs = [v_ref[pl.ds(k0, tk), :] for k0 in k0s]
        zs = {(i, p): _dot_nt(qms[p][first_rows[i]:], kbs[i]) for i, p in pairs}
        sps = {ip: softplus(zs[ip]) for ip in pairs}
        for (i, p) in pairs:
            if masks[i] is not None:
                sps[(i, p)] = jnp.where(masks[i], sps[(i, p)], 0.0)
        cs = {ip: neg_suffix_sum(sps[ip]) for ip in pairs}
        accs, runs = list(state[:2]), list(state[2:])
        ws = {}
        for (i, p) in pairs:
            r0 = first_rows[i]
            w = jnp.exp(zs[(i, p)] + cs[(i, p)] + runs[p][r0:])
            ws[(i, p)] = w if masks[i] is None else jnp.where(masks[i], w, 0.0)
            runs[p] = add_from_row(runs[p], r0, cs[(i, p)][:, 0:1])
        for (i, p) in pairs:
            accs[p] = add_from_row(accs[p], first_rows[i], _dot(ws[(i, p)].astype(BF16), vbs[i]))
        return (*accs, *runs)

    def q_block(qi, carry):
        q0 = pl.multiple_of(qi * tq, tq)
        q2 = q_ref[pl.ds(q0, tq), :] * ATTN_SCALE
        qms = (jnp.where(lane < HEAD_DIM, q2, 0.0).astype(BF16),
               jnp.where(lane >= HEAD_DIM, q2, 0.0).astype(BF16))
        state = (jnp.zeros((tq, LANES), F32), jnp.zeros((tq, LANES), F32),
                 jnp.zeros((tq, 1), F32), jnp.zeros((tq, 1), F32))
        order = list(reversed(range(k_per_q)))
        state = k_span([q0 + d * tk for d in order], qms, state,
                       [((col + d * tk) < row)[d * tk:] for d in order], [d * tk for d in order])

        def earlier(jj, state):
            k0s = [pl.multiple_of((qi - 1 - jj) * tq + d * tk, tk) for d in order]
            return k_span(k0s, qms, state, [None] * k_per_q, [0] * k_per_q)

        state = lax.fori_loop(0, qi, earlier, state)
        o_ref[pl.ds(q0, tq), :] = jnp.where(lane < HEAD_DIM, state[0], state[1]).astype(o_ref.dtype)
        return carry

    lax.fori_loop(0, nq, q_block, 0)


def _sb(q_b, k_b, v_b, batch, seq):
    n = q_b.shape[0]
    spec = pl.BlockSpec((seq, LANES), lambda b, p: (b, p))
    return pl.pallas_call(
        _sb_kernel,
        grid=(batch, SB_WIDTH // LANES),
        in_specs=[spec, spec, spec],
        out_specs=spec,
        out_shape=jax.ShapeDtypeStruct((n, SB_WIDTH), BF16),
        compiler_params=_params("parallel", "parallel"),
        name="stick_breaking",
    )(q_b, k_b, v_b)


def _merge_kernel(ya_ref, yb_ref, ga_ref, gb_ref, x_ref, pa_ref, pb_ref, wo_ref, h_ref):
    m = (ga_ref[...].astype(F32) * _dot(ya_ref[...], pa_ref[...])
         + gb_ref[...].astype(F32) * _dot(yb_ref[...], pb_ref[...]))
    h_ref[...] = x_ref[...] + _dot(m.astype(BF16), wo_ref[...])


def _merge(y_a, y_b, g_a, g_b, x2, p_a, p_b, w_o):
    n, d = x2.shape
    tm = TOKEN_TILE
    row = lambda w: pl.BlockSpec((tm, w), lambda i: (i, 0))
    full = lambda a: pl.BlockSpec(a.shape, lambda i: (0, 0))
    return pl.pallas_call(
        _merge_kernel,
        grid=(n // tm,),
        in_specs=[row(SWA_WIDTH), row(SB_WIDTH), row(d), row(d), row(d), full(p_a), full(p_b), full(w_o)],
        out_specs=row(d),
        out_shape=jax.ShapeDtypeStruct((n, d), F32),
        compiler_params=_params("parallel"),
        name="merge",
    )(y_a, y_b, g_a, g_b, x2, p_a, p_b, w_o)


META_EXPERT = 0
META_RANK = TOP_K
EXT_EXPERT = 0
EXT_GATE = TOP_K
GATE_PARTS = 3


def _router_kernel(h_ref, nw_ref, rw_ref, rb_ref, hx_ref, meta_ref, trun_ref, cnt_ref, run_ref):
    tm, d = h_ref.shape

    @pl.when(pl.program_id(0) == 0)
    def _():
        run_ref[...] = jnp.zeros_like(run_ref)

    hn = _rms(h_ref[...], nw_ref[...])
    hn_hi, hn_lo = _split_bf16(hn)
    both = _dot(hn_hi, rw_ref[...])
    logits = (both[:, :N_EXPERTS] + both[:, N_EXPERTS:] + _dot(hn_lo, rw_ref[:, :N_EXPERTS])
              + rb_ref[...])
    lane = lax.broadcasted_iota(jnp.int32, (tm, N_EXPERTS), 1).astype(F32)
    work = logits
    tops, idxs, hots = [], [], []
    for _ in range(TOP_K):
        m = jnp.max(work, axis=-1, keepdims=True)
        idx = jnp.min(jnp.where(work == m, lane, float(N_EXPERTS)), axis=-1, keepdims=True)
        hot = lane == idx
        work = jnp.where(hot, -jnp.inf, work)
        tops.append(m)
        idxs.append(idx)
        hots.append(hot)
    exps = [jnp.exp(t - tops[0]) for t in tops]
    den = exps[0] + exps[1] + exps[2] + exps[3]
    chosen = jnp.where(hots[0] | hots[1] | hots[2] | hots[3], 1.0, 0.0)
    r = lax.broadcasted_iota(jnp.int32, (tm, tm), 0)
    c = lax.broadcasted_iota(jnp.int32, (tm, tm), 1)
    before = jnp.where(c < r, 1.0, 0.0).astype(BF16)
    run = run_ref[...]
    prefix = _dot(before, chosen.astype(BF16)) + run
    mlane = lax.broadcasted_iota(jnp.int32, (tm, LANES), 1)
    meta = jnp.zeros((tm, LANES), F32)
    ext = jnp.zeros((tm, LANES), F32)
    for k in range(TOP_K):
        rank = jnp.sum(jnp.where(hots[k], prefix, 0.0), axis=-1, keepdims=True)
        meta = jnp.where(mlane == META_EXPERT + k, idxs[k], meta)
        meta = jnp.where(mlane == META_RANK + k, rank, meta)
        ext = jnp.where(mlane == EXT_EXPERT + k, idxs[k], ext)
        rest = exps[k] / den
        for part in range(GATE_PARTS):
            piece = rest.astype(BF16).astype(F32)
            ext = jnp.where(mlane == EXT_GATE + GATE_PARTS * k + part, piece, ext)
            rest = rest - piece
    meta_ref[...] = meta
    hx_ref[:, :d] = hn_hi
    hx_ref[:, d:] = ext.astype(BF16)
    trun_ref[0] = run
    tile_cnt = jnp.sum(chosen, axis=0, keepdims=True)
    run = run + jnp.ceil(tile_cnt / SUBLANES) * SUBLANES
    run_ref[...] = run
    cnt_ref[...] = run


def _router(h1, norm_w, router_w, router_b):
    n, d = h1.shape
    tm = TOKEN_TILE
    tiles = n // tm
    return pl.pallas_call(
        _router_kernel,
        grid=(tiles,),
        in_specs=[pl.BlockSpec((tm, d), lambda i: (i, 0)),
                  pl.BlockSpec((1, d), lambda i: (0, 0)),
                  pl.BlockSpec((d, 2 * N_EXPERTS), lambda i: (0, 0)),
                  pl.BlockSpec((1, N_EXPERTS), lambda i: (0, 0))],
        out_specs=[pl.BlockSpec((tm, d + LANES), lambda i: (i, 0)),
                   pl.BlockSpec((tm, LANES), lambda i: (i, 0)),
                   pl.BlockSpec((1, 1, N_EXPERTS), lambda i: (i, 0, 0)),
                   pl.BlockSpec((1, N_EXPERTS), lambda i: (0, 0))],
        out_shape=[jax.ShapeDtypeStruct((n, d + LANES), BF16),
                   jax.ShapeDtypeStruct((n, LANES), F32),
                   jax.ShapeDtypeStruct((tiles, 1, N_EXPERTS), F32),
                   jax.ShapeDtypeStruct((1, N_EXPERTS), F32)],
        scratch_shapes=[pltpu.VMEM((1, N_EXPERTS), F32)],
        compiler_params=_params("arbitrary"),
        name="router",
    )(h1, norm_w.reshape(1, d), jnp.concatenate(_split_bf16(router_w), axis=1),
      router_b.reshape(1, N_EXPERTS))


def _routing_tables(tile_run, counts, n):
    tiles = tile_run.shape[0]
    counts = counts.reshape(N_EXPERTS).astype(jnp.int32)
    run = tile_run.reshape(tiles, N_EXPERTS).astype(jnp.int32)
    tile_len = jnp.concatenate([run[1:], counts[None]], axis=0) - run
    loc = jnp.cumsum(tile_len, axis=1) - tile_len
    padded = (counts + FFN_BLOCK - 1) // FFN_BLOCK * FFN_BLOCK
    seg_end = jnp.cumsum(padded)
    seg_start = seg_end - padded
    dst0 = seg_start[None, :] + run
    delta = (loc - run).astype(F32).reshape(tiles, 1, N_EXPERTS)
    zero0 = seg_start + counts
    nzero = (padded - counts) // SUBLANES
    n_real = seg_end[-1] // FFN_BLOCK
    max_blocks = (n * TOP_K + tiles * N_EXPERTS * (SUBLANES - 1)
                  + N_EXPERTS * (FFN_BLOCK - 1)) // FFN_BLOCK
    blk_exp = jnp.minimum(jnp.sum(jnp.arange(max_blocks, dtype=jnp.int32)[:, None] * FFN_BLOCK
                                  >= seg_end[None, :], axis=1), N_EXPERTS - 1).astype(jnp.int32)
    return dict(dst0=dst0.reshape(-1), tile_len=tile_len.reshape(-1), loc=loc.reshape(-1),
                delta=delta, zero0=zero0, nzero=nzero, blk_exp=blk_exp,
                n_real=n_real.reshape(1).astype(jnp.int32), cap=max_blocks * FFN_BLOCK)


def _tile_rows(meta_ref, delta_ref):
    meta = meta_ref[...]
    tm = meta.shape[0]
    delta = delta_ref[0]
    lane = lax.broadcasted_iota(jnp.int32, (tm, N_EXPERTS), 1).astype(F32)
    rows = []
    for k in range(TOP_K):
        expert = meta[:, META_EXPERT + k:META_EXPERT + k + 1]
        rank = meta[:, META_RANK + k:META_RANK + k + 1]
        rows.append(rank + jnp.sum(jnp.where(lane == expert, delta, 0.0), axis=-1, keepdims=True))
    return rows


def _chunk_copies(t, tbl_ref, len_ref, loc_ref, make_copy, op):
    def one(seg_row, loc_row, rows):
        cp = make_copy(pl.multiple_of(seg_row, SUBLANES), pl.multiple_of(loc_row, SUBLANES), rows)
        cp.start() if op == "start" else cp.wait()

    for e in range(N_EXPERTS):
        idx = t * N_EXPERTS + e
        seg_row = tbl_ref[idx]
        loc_row = loc_ref[idx]
        length = len_ref[idx]
        full = length // CHUNK

        def body(c, carry):
            one(seg_row + c * CHUNK, loc_row + c * CHUNK, CHUNK)
            return carry

        lax.fori_loop(0, full, body, 0)
        done = full * CHUNK
        rows = CHUNK // 2
        while rows >= SUBLANES:
            @pl.when((length & rows) != 0)
            def _(done=done, rows=rows):
                one(seg_row + done, loc_row + done, rows)
            done = done + (length & rows)
            rows //= 2


def _dispatch_kernel(dst0_ref, len_ref, loc_ref, zero0_ref, nzero_ref, nreal_ref,
                     hx_ref, meta_ref, delta_ref, xb_ref, srt_ref, zero_ref, sem):
    t = pl.program_id(0)
    tm = hx_ref.shape[0]
    rows = _tile_rows(meta_ref, delta_ref)
    mlane = lax.broadcasted_iota(jnp.int32, (tm, LANES), 1)
    packed = jnp.full((tm, LANES), -1.0, F32)
    for k in range(TOP_K):
        packed = jnp.where(mlane == k, rows[k], packed)
    rows_t = packed.T
    slot = t % 2
    part = SORTED_ROWS // SORT_PARTS
    for c in range(SORT_PARTS):
        prow = (lax.broadcasted_iota(jnp.int32, (part, tm), 0) + c * part).astype(F32)
        onehot = jnp.zeros((part, tm), F32)
        for k in range(TOP_K):
            onehot = jnp.where(prow == rows_t[k:k + 1, :], 1.0, onehot)
        srt_ref[slot, c * part:(c + 1) * part] = _dot(onehot.astype(BF16), hx_ref[...])

    def copy_from(s):
        def copy(seg_row, loc_row, rows):
            return pltpu.make_async_copy(srt_ref.at[s, pl.ds(loc_row, rows)],
                                         xb_ref.at[pl.ds(seg_row, rows)], sem.at[s])
        return copy

    @pl.when(t > 0)
    def _():
        _chunk_copies(t - 1, dst0_ref, len_ref, loc_ref, copy_from(1 - slot), "wait")

    _chunk_copies(t, dst0_ref, len_ref, loc_ref, copy_from(slot), "start")

    @pl.when(t == pl.num_programs(0) - 1)
    def _():
        _chunk_copies(t, dst0_ref, len_ref, loc_ref, copy_from(slot), "wait")
        zero_ref[...] = jnp.zeros_like(zero_ref)
        for op in ("start", "wait"):
            for e in range(N_EXPERTS):
                def body(c, carry):
                    row0 = pl.multiple_of(zero0_ref[e] + c * SUBLANES, SUBLANES)
                    cp = pltpu.make_async_copy(zero_ref.at[pl.ds(0, SUBLANES)],
                                               xb_ref.at[pl.ds(row0, SUBLANES)], sem.at[0])
                    cp.start() if op == "start" else cp.wait()
                    return carry

                lax.fori_loop(0, nzero_ref[e], body, 0)

            def tail(b, carry):
                row0 = pl.multiple_of(b * FFN_BLOCK, FFN_BLOCK)
                cp = pltpu.make_async_copy(zero_ref, xb_ref.at[pl.ds(row0, FFN_BLOCK)], sem.at[0])
                cp.start() if op == "start" else cp.wait()
                return carry

            lax.fori_loop(nreal_ref[0], xb_ref.shape[0] // FFN_BLOCK, tail, 0)


def _dispatch(hx, meta, tbl):
    n, width = hx.shape
    tm = TOKEN_TILE
    grid_spec = pltpu.PrefetchScalarGridSpec(
        num_scalar_prefetch=6,
        grid=(n // tm,),
        in_specs=[pl.BlockSpec((tm, width), lambda i, *_: (i, 0)),
                  pl.BlockSpec((tm, LANES), lambda i, *_: (i, 0)),
                  pl.BlockSpec((1, 1, N_EXPERTS), lambda i, *_: (i, 0, 0))],
        out_specs=pl.BlockSpec(memory_space=pl.ANY),
        scratch_shapes=[pltpu.VMEM((2, SORTED_ROWS, width), F32),
                        pltpu.VMEM((FFN_BLOCK, width), F32),
                        pltpu.SemaphoreType.DMA((2,))],
    )
    return pl.pallas_call(
        _dispatch_kernel,
        grid_spec=grid_spec,
        out_shape=jax.ShapeDtypeStruct((tbl["cap"], width), F32),
        compiler_params=_params("arbitrary"),
        name="dispatch",
    )(tbl["dst0"], tbl["tile_len"], tbl["loc"], tbl["zero0"], tbl["nzero"], tbl["n_real"],
      hx, meta, tbl["delta"])


def _ffn_kernel(exp_ref, nreal_ref, x_ref, wgu_ref, bgu_ref, wd_ref, bd_ref, y_ref, wgu_bf, wd_bf):
    b = pl.program_id(0)
    d_ff, d = wd_ref.shape[1], wd_ref.shape[2]

    @pl.when(b >= nreal_ref[0])
    def _():
        y_ref[...] = jnp.zeros_like(y_ref)

    @pl.when((b == 0) | (exp_ref[b] != exp_ref[jnp.maximum(b - 1, 0)]))
    def _():
        wgu_bf[...] = wgu_ref[0].astype(BF16)
        wd_bf[...] = wd_ref[0].astype(BF16)

    @pl.when(b < nreal_ref[0])
    def _():
        x = x_ref[...]
        ext = x[:, d:]
        expert = exp_ref[b].astype(F32)
        gate = jnp.zeros((x.shape[0], 1), F32)
        for k in range(TOP_K):
            g0 = EXT_GATE + GATE_PARTS * k
            g = ext[:, g0:g0 + 1] + ext[:, g0 + 1:g0 + 2] + ext[:, g0 + 2:g0 + 3]
            gate = gate + jnp.where(ext[:, EXT_EXPERT + k:EXT_EXPERT + k + 1] == expert, g, 0.0)
        gu = _dot(x[:, :d].astype(BF16), wgu_bf[...]) + bgu_ref[0]
        g = jnp.minimum(gu[:, :d_ff], SWIGLU_LIMIT)
        u = jnp.clip(gu[:, d_ff:], -SWIGLU_LIMIT, SWIGLU_LIMIT)
        act = (u + 1.0) * (g * jax.nn.sigmoid(SWIGLU_ALPHA * g))
        y_ref[...] = gate * (_dot(act.astype(BF16), wd_bf[...]) + bd_ref[0])


def _ffn(xb, tbl, w_gate_up, b_gate_up, w_down, b_down):
    cap, width = xb.shape
    e, d, two_ff = w_gate_up.shape
    d_ff = two_ff // 2
    grid_spec = pltpu.PrefetchScalarGridSpec(
        num_scalar_prefetch=2,
        grid=(cap // FFN_BLOCK,),
        in_specs=[pl.BlockSpec((FFN_BLOCK, width), lambda b, ex, nr: (b, 0)),
                  pl.BlockSpec((1, d, two_ff), lambda b, ex, nr: (ex[b], 0, 0)),
                  pl.BlockSpec((1, 1, two_ff), lambda b, ex, nr: (ex[b], 0, 0)),
                  pl.BlockSpec((1, d_ff, d), lambda b, ex, nr: (ex[b], 0, 0)),
                  pl.BlockSpec((1, 1, d), lambda b, ex, nr: (ex[b], 0, 0))],
        out_specs=pl.BlockSpec((FFN_BLOCK, d), lambda b, ex, nr: (b, 0)),
        scratch_shapes=[pltpu.VMEM((d, two_ff), BF16), pltpu.VMEM((d_ff, d), BF16)],
    )
    return pl.pallas_call(
        _ffn_kernel,
        grid_spec=grid_spec,
        out_shape=jax.ShapeDtypeStruct((cap, d), F32),
        compiler_params=_params("arbitrary"),
        name="expert_ffn",
    )(tbl["blk_exp"], tbl["n_real"], xb, w_gate_up,
      b_gate_up.reshape(e, 1, two_ff), w_down, b_down.reshape(e, 1, d))


def _combine_kernel(dst0_ref, len_ref, loc_ref, meta_ref, delta_ref, h_ref, y_ref, o_ref, srt_ref, sem):
    t = pl.program_id(0)
    tm = h_ref.shape[0]

    @pl.when(t == 0)
    def _():
        srt_ref[...] = jnp.zeros_like(srt_ref)

    slot = t % 2

    def copy_into(s):
        def copy(seg_row, loc_row, rows):
            return pltpu.make_async_copy(y_ref.at[pl.ds(seg_row, rows)],
                                         srt_ref.at[s, pl.ds(loc_row, rows)], sem.at[s])
        return copy

    @pl.when(t == 0)
    def _():
        _chunk_copies(t, dst0_ref, len_ref, loc_ref, copy_into(slot), "start")

    @pl.when(t + 1 < pl.num_programs(0))
    def _():
        _chunk_copies(t + 1, dst0_ref, len_ref, loc_ref, copy_into(1 - slot), "start")

    rows = _tile_rows(meta_ref, delta_ref)
    part = SORTED_ROWS // SORT_PARTS

    def onehot_part(c):
        pcol = (lax.broadcasted_iota(jnp.int32, (tm, part), 1) + c * part).astype(F32)
        onehot = jnp.zeros((tm, part), F32)
        for k in range(TOP_K):
            onehot = jnp.where(pcol == rows[k], 1.0, onehot)
        return onehot.astype(BF16)

    first = onehot_part(0)
    _chunk_copies(t, dst0_ref, len_ref, loc_ref, copy_into(slot), "wait")
    out = h_ref[...]
    for c in range(SORT_PARTS):
        onehot = first if c == 0 else onehot_part(c)
        out = out + _dot(onehot, srt_ref[slot, c * part:(c + 1) * part].astype(BF16))
    o_ref[...] = out


def _combine(meta, tbl, h1, yb):
    n, d = h1.shape
    tm = TOKEN_TILE
    grid_spec = pltpu.PrefetchScalarGridSpec(
        num_scalar_prefetch=3,
        grid=(n // tm,),
        in_specs=[pl.BlockSpec((tm, LANES), lambda i, *_: (i, 0)),
                  pl.BlockSpec((1, 1, N_EXPERTS), lambda i, *_: (i, 0, 0)),
                  pl.BlockSpec((tm, d), lambda i, *_: (i, 0)),
                  pl.BlockSpec(memory_space=pl.ANY)],
        out_specs=pl.BlockSpec((tm, d), lambda i, *_: (i, 0)),
        scratch_shapes=[pltpu.VMEM((2, SORTED_ROWS, d), F32), pltpu.SemaphoreType.DMA((2,))],
    )
    return pl.pallas_call(
        _combine_kernel,
        grid_spec=grid_spec,
        out_shape=jax.ShapeDtypeStruct((n, d), F32),
        compiler_params=_params("arbitrary"),
        name="combine",
    )(tbl["dst0"], tbl["tile_len"], tbl["loc"], meta, tbl["delta"], h1, yb)


def kernel(x, attn_norm_w, w_in, q_norm_w, k_norm_w, sinks, w_proj_swa, w_proj_sb, w_out,
           ffn_norm_w, router_w, router_b, w_gate_up, b_gate_up, w_down, b_down):
    batch, seq, d = x.shape
    n = batch * seq
    h = x.reshape(n, d)
    for layer in range(attn_norm_w.shape[0]):
        q_a, kv_a, q_b, k_b, v_b, g_a, g_b = _in_proj(h, attn_norm_w[layer], w_in[layer].astype(BF16))
        y_a = _swa(q_a, kv_a, q_norm_w[layer], k_norm_w[layer], sinks[layer], batch, seq)
        y_b = _sb(q_b, k_b, v_b, batch, seq)
        h1 = _merge(y_a, y_b, g_a, g_b, h, w_proj_swa[layer].astype(BF16),
                    w_proj_sb[layer].astype(BF16), w_out[layer].astype(BF16))
        hx, meta, tile_run, counts = _router(h1, ffn_norm_w[layer], router_w[layer], router_b[layer])
        tbl = _routing_tables(tile_run, counts, n)
        xb = _dispatch(hx, meta, tbl)
        yb = _ffn(xb, tbl, w_gate_up[layer], b_gate_up[layer], w_down[layer], b_down[layer])
        h = _combine(meta, tbl, h1, yb)
    return h.reshape(batch, seq, d)
```

```python
import jax
import jax.numpy as jnp
from jax import lax
from jax.experimental import pallas as pl
from jax.experimental.pallas import tpu as pltpu

HEAD_DIM = 64
SWA_HEADS = 8
SWA_KV_HEADS = 2
SWA_GROUP = SWA_HEADS // SWA_KV_HEADS
BLOCK = 128
SB_HEADS = 8
SWA_WIDTH = SWA_HEADS * HEAD_DIM
SWA_KV_WIDTH = SWA_KV_HEADS * HEAD_DIM
SB_WIDTH = SB_HEADS * HEAD_DIM
N_EXPERTS = 32
TOP_K = 4
SWIGLU_LIMIT = 7.0
SWIGLU_ALPHA = 1.702
NORM_EPS = 1e-5
ATTN_SCALE = HEAD_DIM ** -0.5
SIGN_BIT = -2 ** 31

LANES = 128
SUBLANES = 8
TOKEN_TILE = 512
FFN_BLOCK = 512
CHUNK = 32
SORTED_ROWS = -(-(TOKEN_TILE * TOP_K + N_EXPERTS * (SUBLANES - 1)) // LANES) * LANES
SORT_PARTS = 2
SB_Q_ROWS = 1024
SB_K_ROWS = 256
SB_GROUP = 1
SB_STAGGER = 1
VMEM_LIMIT = 56 * 1024 * 1024

F32 = jnp.float32
BF16 = jnp.bfloat16
NEG_BIG = -1e30


def _dot(a, b):
    return jnp.dot(a, b, preferred_element_type=F32)


def _dot_nt(a, b):
    return lax.dot_general(a, b, (((1,), (1,)), ((), ())), preferred_element_type=F32)


def _sigmoid(x):
    return jax.nn.sigmoid(x)


def _rms(x, w):
    return x * lax.rsqrt(jnp.mean(x * x, axis=-1, keepdims=True) + NORM_EPS) * w


def _params(*sem):
    return pltpu.CompilerParams(dimension_semantics=sem, vmem_limit_bytes=VMEM_LIMIT)


def _in_proj_kernel(x_ref, nw_ref, w_ref, qa_ref, kva_ref, qb_ref, kb_ref, vb_ref, ga_ref, gb_ref):
    xn = _rms(x_ref[...], nw_ref[...]).astype(BF16)
    off = 0
    for ref, gate in ((qa_ref, False), (kva_ref, False), (qb_ref, False), (kb_ref, False),
                      (vb_ref, False), (ga_ref, True), (gb_ref, True)):
        width = ref.shape[1]
        y = _dot(xn, w_ref[:, off:off + width])
        if gate:
            y = _sigmoid(y)
        ref[...] = y.astype(ref.dtype)
        off += width


def _in_proj(x2, norm_w, w_in):
    n, d = x2.shape
    widths = (SWA_WIDTH, 2 * SWA_KV_WIDTH, SB_WIDTH, SB_WIDTH, SB_WIDTH, d, d)
    tm = TOKEN_TILE
    return pl.pallas_call(
        _in_proj_kernel,
        grid=(n // tm,),
        in_specs=[pl.BlockSpec((tm, d), lambda i: (i, 0)),
                  pl.BlockSpec((1, d), lambda i: (0, 0)),
                  pl.BlockSpec(w_in.shape, lambda i: (0, 0))],
        out_specs=[pl.BlockSpec((tm, w), lambda i: (i, 0)) for w in widths],
        out_shape=[jax.ShapeDtypeStruct((n, w), BF16) for w in widths],
        compiler_params=_params("parallel"),
        name="in_proj",
    )(x2, norm_w.reshape(1, d), w_in)


def _swa_kernel(q_ref, kvc_ref, kvp_ref, qnw_ref, knw_ref, sink_ref, o_ref):
    i = pl.program_id(1)
    half = HEAD_DIM

    def group_mean_sq(x):
        w = x.shape[1]
        same_head = (lax.broadcasted_iota(jnp.int32, (w, w), 0) // half
                     == lax.broadcasted_iota(jnp.int32, (w, w), 1) // half)
        avg = jnp.where(same_head, 1.0 / half, 0.0).astype(BF16)
        hi, lo = _split_bf16(x * x)
        return _dot(hi, avg) + _dot(lo, avg)

    q = q_ref[...].astype(F32)
    kv = jnp.concatenate([kvp_ref[...], kvc_ref[...]], axis=0)
    k = kv[:, :SWA_KV_WIDTH].astype(F32)
    v = kv[:, SWA_KV_WIDTH:].astype(F32)
    qn = (q * lax.rsqrt(group_mean_sq(q) + NORM_EPS) * qnw_ref[...] * ATTN_SCALE).astype(BF16)
    kn = k * lax.rsqrt(group_mean_sq(k) + NORM_EPS) * knw_ref[...]
    lane = lax.broadcasted_iota(jnp.int32, (2 * BLOCK, LANES), 1)
    low = lane < half
    k_swap = pltpu.roll(kn, half, 1)
    v_swap = pltpu.roll(v, half, 1).astype(BF16)
    v_same = v.astype(BF16)
    keys = [[jnp.where(low, kn, 0.0).astype(BF16), jnp.where(low, 0.0, k_swap).astype(BF16)],
            [jnp.where(low, k_swap, 0.0).astype(BF16), jnp.where(low, 0.0, kn).astype(BF16)]]
    vals = [[v_same, v_swap], [v_swap, v_same]]

    row = lax.broadcasted_iota(jnp.int32, (BLOCK, 2 * BLOCK), 0)
    col = lax.broadcasted_iota(jnp.int32, (BLOCK, 2 * BLOCK), 1)
    rel = row + BLOCK - col
    valid = (rel >= 0) & (rel < BLOCK) & ((col >= BLOCK) | (i > 0))
    rel = rel.astype(F32)
    out_lane_low = lax.broadcasted_iota(jnp.int32, (BLOCK, LANES), 1) < half
    for blk in range(SWA_HEADS // 2):
        qb = qn[:, blk * LANES:(blk + 1) * LANES]
        halves = []
        for par in range(2):
            h = 2 * blk + par
            g = h // SWA_GROUP
            slope = 2.0 ** (-8.0 * (h + 1) / SWA_HEADS)
            s = jnp.where(valid, _dot_nt(qb, keys[g][par]) - slope * rel, NEG_BIG)
            sink = sink_ref[h]
            m = jnp.maximum(jnp.max(s, axis=-1, keepdims=True), sink)
            p = jnp.exp(s - m)
            den = jnp.sum(p, axis=-1, keepdims=True) + jnp.exp(sink - m)
            halves.append(_dot(p.astype(BF16), vals[g][par]) / den)
        o_ref[:, blk * LANES:(blk + 1) * LANES] = jnp.where(out_lane_low, halves[0], halves[1]).astype(o_ref.dtype)


def _swa(q_a, kv_a, q_norm_w, k_norm_w, sinks, batch, seq):
    n = q_a.shape[0]
    nb = seq // BLOCK
    return pl.pallas_call(
        _swa_kernel,
        grid=(batch, nb),
        in_specs=[pl.BlockSpec((BLOCK, SWA_WIDTH), lambda b, i: (b * nb + i, 0)),
                  pl.BlockSpec((BLOCK, 2 * SWA_KV_WIDTH), lambda b, i: (b * nb + i, 0)),
                  pl.BlockSpec((BLOCK, 2 * SWA_KV_WIDTH), lambda b, i: (b * nb + jnp.maximum(i - 1, 0), 0)),
                  pl.BlockSpec((1, SWA_WIDTH), lambda b, i: (0, 0)),
                  pl.BlockSpec((1, SWA_KV_WIDTH), lambda b, i: (0, 0)),
                  pl.BlockSpec(memory_space=pltpu.SMEM)],
        out_specs=pl.BlockSpec((BLOCK, SWA_WIDTH), lambda b, i: (b * nb + i, 0)),
        out_shape=jax.ShapeDtypeStruct((n, SWA_WIDTH), BF16),
        compiler_params=_params("parallel", "parallel"),
        name="swa",
    )(q_a, kv_a, kv_a, jnp.tile(q_norm_w, SWA_HEADS).reshape(1, SWA_WIDTH),
      jnp.tile(k_norm_w, SWA_KV_HEADS).reshape(1, SWA_KV_WIDTH), sinks)


def _split_bf16(x):
    hi = x.astype(BF16)
    lo = (x - hi.astype(F32)).astype(BF16)
    return hi, lo


def _sb_kernel(q_ref, k_ref, v_ref, o_ref):
    seq = q_ref.shape[0]
    tq, tk = SB_Q_ROWS, SB_K_ROWS
    nq = seq // tq
    k_per_q = tq // tk
    lane = lax.broadcasted_iota(jnp.int32, (tq, LANES), 1)
    row = lax.broadcasted_iota(jnp.int32, (tq, tk), 0)
    col = lax.broadcasted_iota(jnp.int32, (tq, tk), 1)
    neg_incl = jnp.where(lax.broadcasted_iota(jnp.int32, (tk, tk), 0)
                         >= lax.broadcasted_iota(jnp.int32, (tk, tk), 1), -1.0, 0.0).astype(BF16)

    def softplus(z):
        neg_abs = lax.bitcast_convert_type(lax.bitcast_convert_type(z, jnp.int32) | SIGN_BIT, F32)
        return jnp.maximum(z, 0.0) + jnp.log(1.0 + jnp.exp(neg_abs))

    def neg_suffix_sum(sp):
        return _dot(sp.astype(BF16), neg_incl)

    def add_from_row(x, r0, delta):
        return x + delta if r0 == 0 else jnp.concatenate([x[:r0], x[r0:] + delta], axis=0)

    def k_span(k0s, qms, state, masks, first_rows):
        pairs = [(i, p) for p in range(2) for i in range(len(k0s))]
        kbs = [k_ref[pl.ds(k0, tk), :] for k0 in k0s]
        vbs = [v_ref[pl.ds(k0, tk), :] for k0 in k0s]
        accs, runs = list(state[:2]), list(state[2:])
        zs, sps, cs, ws = {}, {}, {}, {}

        def stage(s, i, p):
            r0 = first_rows[i]
            if s == 0:
                zs[(i, p)] = _dot_nt(qms[p][r0:], kbs[i])
            elif s == 1:
                sp = softplus(zs[(i, p)])
                sps[(i, p)] = sp if masks[i] is None else jnp.where(masks[i], sp, 0.0)
            elif s == 2:
                cs[(i, p)] = neg_suffix_sum(sps[(i, p)])
            elif s == 3:
                w = jnp.exp(zs[(i, p)] + cs[(i, p)] + runs[p][r0:])
                ws[(i, p)] = w if masks[i] is None else jnp.where(masks[i], w, 0.0)
                runs[p] = add_from_row(runs[p], r0, cs[(i, p)][:, 0:1])
            else:
                accs[p] = add_from_row(accs[p], r0, _dot(ws[(i, p)].astype(BF16), vbs[i]))

        n_stages = 5
        for step in range(((len(pairs) - 1) // SB_GROUP) * SB_STAGGER + n_stages):
            for n, (i, p) in enumerate(pairs):
                s = step - (n // SB_GROUP) * SB_STAGGER
                if 0 <= s < n_stages:
                    stage(s, i, p)
        return (*accs, *runs)

    def q_block(qi, carry):
        q0 = pl.multiple_of(qi * tq, tq)
        q2 = q_ref[pl.ds(q0, tq), :] * ATTN_SCALE
        qms = (jnp.where(lane < HEAD_DIM, q2, 0.0).astype(BF16),
               jnp.where(lane >= HEAD_DIM, q2, 0.0).astype(BF16))
        state = (jnp.zeros((tq, LANES), F32), jnp.zeros((tq, LANES), F32),
                 jnp.zeros((tq, 1), F32), jnp.zeros((tq, 1), F32))
        order = list(reversed(range(k_per_q)))
        state = k_span([q0 + d * tk for d in order], qms, state,
                       [((col + d * tk) < row)[d * tk:] for d in order], [d * tk for d in order])

        def earlier(jj, state):
            k0s = [pl.multiple_of((qi - 1 - jj) * tq + d * tk, tk) for d in order]
            return k_span(k0s, qms, state, [None] * k_per_q, [0] * k_per_q)

        state = lax.fori_loop(0, qi, earlier, state)
        o_ref[pl.ds(q0, tq), :] = jnp.where(lane < HEAD_DIM, state[0], state[1]).astype(o_ref.dtype)
        return carry

    lax.fori_loop(0, nq, q_block, 0)


def _sb(q_b, k_b, v_b, batch, seq):
    n = q_b.shape[0]
    spec = pl.BlockSpec((seq, LANES), lambda b, p: (b, p))
    return pl.pallas_call(
        _sb_kernel,
        grid=(batch, SB_WIDTH // LANES),
        in_specs=[spec, spec, spec],
        out_specs=spec,
        out_shape=jax.ShapeDtypeStruct((n, SB_WIDTH), BF16),
        compiler_params=_params("parallel", "parallel"),
        name="stick_breaking",
    )(q_b, k_b, v_b)


def _merge_kernel(ya_ref, yb_ref, ga_ref, gb_ref, x_ref, pa_ref, pb_ref, wo_ref, h_ref):
    m = (ga_ref[...].astype(F32) * _dot(ya_ref[...], pa_ref[...])
         + gb_ref[...].astype(F32) * _dot(yb_ref[...], pb_ref[...]))
    h_ref[...] = x_ref[...] + _dot(m.astype(BF16), wo_ref[...])


def _merge(y_a, y_b, g_a, g_b, x2, p_a, p_b, w_o):
    n, d = x2.shape
    tm = TOKEN_TILE
    row = lambda w: pl.BlockSpec((tm, w), lambda i: (i, 0))
    full = lambda a: pl.BlockSpec(a.shape, lambda i: (0, 0))
    return pl.pallas_call(
        _merge_kernel,
        grid=(n // tm,),
        in_specs=[row(SWA_WIDTH), row(SB_WIDTH), row(d), row(d), row(d), full(p_a), full(p_b), full(w_o)],
        out_specs=row(d),
        out_shape=jax.ShapeDtypeStruct((n, d), F32),
        compiler_params=_params("parallel"),
        name="merge",
    )(y_a, y_b, g_a, g_b, x2, p_a, p_b, w_o)


META_EXPERT = 0
META_RANK = TOP_K
EXT_EXPERT = 0
EXT_GATE = TOP_K
GATE_PARTS = 3


def _router_kernel(h_ref, nw_ref, rw_ref, rb_ref, hx_ref, meta_ref, trun_ref, cnt_ref, run_ref):
    tm, d = h_ref.shape

    @pl.when(pl.program_id(0) == 0)
    def _():
        run_ref[...] = jnp.zeros_like(run_ref)

    hn = _rms(h_ref[...], nw_ref[...])
    hn_hi, hn_lo = _split_bf16(hn)
    both = _dot(hn_hi, rw_ref[...])
    logits = (both[:, :N_EXPERTS] + both[:, N_EXPERTS:] + _dot(hn_lo, rw_ref[:, :N_EXPERTS])
              + rb_ref[...])
    lane = lax.broadcasted_iota(jnp.int32, (tm, N_EXPERTS), 1).astype(F32)
    work = logits
    tops, idxs, hots = [], [], []
    for _ in range(TOP_K):
        m = jnp.max(work, axis=-1, keepdims=True)
        idx = jnp.min(jnp.where(work == m, lane, float(N_EXPERTS)), axis=-1, keepdims=True)
        hot = lane == idx
        work = jnp.where(hot, -jnp.inf, work)
        tops.append(m)
        idxs.append(idx)
        hots.append(hot)
    exps = [jnp.exp(t - tops[0]) for t in tops]
    den = exps[0] + exps[1] + exps[2] + exps[3]
    chosen = jnp.where(hots[0] | hots[1] | hots[2] | hots[3], 1.0, 0.0)
    r = lax.broadcasted_iota(jnp.int32, (tm, tm), 0)
    c = lax.broadcasted_iota(jnp.int32, (tm, tm), 1)
    before = jnp.where(c < r, 1.0, 0.0).astype(BF16)
    run = run_ref[...]
    prefix = _dot(before, chosen.astype(BF16)) + run
    mlane = lax.broadcasted_iota(jnp.int32, (tm, LANES), 1)
    meta = jnp.zeros((tm, LANES), F32)
    ext = jnp.zeros((tm, LANES), F32)
    for k in range(TOP_K):
        rank = jnp.sum(jnp.where(hots[k], prefix, 0.0), axis=-1, keepdims=True)
        meta = jnp.where(mlane == META_EXPERT + k, idxs[k], meta)
        meta = jnp.where(mlane == META_RANK + k, rank, meta)
        ext = jnp.where(mlane == EXT_EXPERT + k, idxs[k], ext)
        rest = exps[k] / den
        for part in range(GATE_PARTS):
            piece = rest.astype(BF16).astype(F32)
            ext = jnp.where(mlane == EXT_GATE + GATE_PARTS * k + part, piece, ext)
            rest = rest - piece
    meta_ref[...] = meta
    hx_ref[:, :d] = hn_hi
    hx_ref[:, d:] = ext.astype(BF16)
    trun_ref[0] = run
    tile_cnt = jnp.sum(chosen, axis=0, keepdims=True)
    run = run + jnp.ceil(tile_cnt / SUBLANES) * SUBLANES
    run_ref[...] = run
    cnt_ref[...] = run


def _router(h1, norm_w, router_w, router_b):
    n, d = h1.shape
    tm = TOKEN_TILE
    tiles = n // tm
    return pl.pallas_call(
        _router_kernel,
        grid=(tiles,),
        in_specs=[pl.BlockSpec((tm, d), lambda i: (i, 0)),
                  pl.BlockSpec((1, d), lambda i: (0, 0)),
                  pl.BlockSpec((d, 2 * N_EXPERTS), lambda i: (0, 0)),
                  pl.BlockSpec((1, N_EXPERTS), lambda i: (0, 0))],
        out_specs=[pl.BlockSpec((tm, d + LANES), lambda i: (i, 0)),
                   pl.BlockSpec((tm, LANES), lambda i: (i, 0)),
                   pl.BlockSpec((1, 1, N_EXPERTS), lambda i: (i, 0, 0)),
                   pl.BlockSpec((1, N_EXPERTS), lambda i: (0, 0))],
        out_shape=[jax.ShapeDtypeStruct((n, d + LANES), BF16),
                   jax.ShapeDtypeStruct((n, LANES), F32),
                   jax.ShapeDtypeStruct((tiles, 1, N_EXPERTS), F32),
                   jax.ShapeDtypeStruct((1, N_EXPERTS), F32)],
        scratch_shapes=[pltpu.VMEM((1, N_EXPERTS), F32)],
        compiler_params=_params("arbitrary"),
        name="router",
    )(h1, norm_w.reshape(1, d), jnp.concatenate(_split_bf16(router_w), axis=1),
      router_b.reshape(1, N_EXPERTS))


def _routing_tables(tile_run, counts, n):
    tiles = tile_run.shape[0]
    counts = counts.reshape(N_EXPERTS).astype(jnp.int32)
    run = tile_run.reshape(tiles, N_EXPERTS).astype(jnp.int32)
    tile_len = jnp.concatenate([run[1:], counts[None]], axis=0) - run
    loc = jnp.cumsum(tile_len, axis=1) - tile_len
    padded = (counts + FFN_BLOCK - 1) // FFN_BLOCK * FFN_BLOCK
    seg_end = jnp.cumsum(padded)
    seg_start = seg_end - padded
    dst0 = seg_start[None, :] + run
    delta = (loc - run).astype(F32).reshape(tiles, 1, N_EXPERTS)
    zero0 = seg_start + counts
    nzero = (padded - counts) // SUBLANES
    n_real = seg_end[-1] // FFN_BLOCK
    max_blocks = (n * TOP_K + tiles * N_EXPERTS * (SUBLANES - 1)
                  + N_EXPERTS * (FFN_BLOCK - 1)) // FFN_BLOCK
    blk_exp = jnp.minimum(jnp.sum(jnp.arange(max_blocks, dtype=jnp.int32)[:, None] * FFN_BLOCK
                                  >= seg_end[None, :], axis=1), N_EXPERTS - 1).astype(jnp.int32)
    return dict(dst0=dst0.reshape(-1), tile_len=tile_len.reshape(-1), loc=loc.reshape(-1),
                delta=delta, zero0=zero0, nzero=nzero, blk_exp=blk_exp,
                n_real=n_real.reshape(1).astype(jnp.int32), cap=max_blocks * FFN_BLOCK)


def _tile_rows(meta_ref, delta_ref):
    meta = meta_ref[...]
    tm = meta.shape[0]
    delta = delta_ref[0]
    lane = lax.broadcasted_iota(jnp.int32, (tm, N_EXPERTS), 1).astype(F32)
    rows = []
    for k in range(TOP_K):
        expert = meta[:, META_EXPERT + k:META_EXPERT + k + 1]
        rank = meta[:, META_RANK + k:META_RANK + k + 1]
        rows.append(rank + jnp.sum(jnp.where(lane == expert, delta, 0.0), axis=-1, keepdims=True))
    return rows


def _chunk_copies(t, tbl_ref, len_ref, loc_ref, make_copy, op):
    def one(seg_row, loc_row, rows):
        cp = make_copy(pl.multiple_of(seg_row, SUBLANES), pl.multiple_of(loc_row, SUBLANES), rows)
        cp.start() if op == "start" else cp.wait()

    for e in range(N_EXPERTS):
        idx = t * N_EXPERTS + e
        seg_row = tbl_ref[idx]
        loc_row = loc_ref[idx]
        length = len_ref[idx]
        full = length // CHUNK

        def body(c, carry):
            one(seg_row + c * CHUNK, loc_row + c * CHUNK, CHUNK)
            return carry

        lax.fori_loop(0, full, body, 0)
        done = full * CHUNK
        rows = CHUNK // 2
        while rows >= SUBLANES:
            @pl.when((length & rows) != 0)
            def _(done=done, rows=rows):
                one(seg_row + done, loc_row + done, rows)
            done = done + (length & rows)
            rows //= 2


def _dispatch_kernel(dst0_ref, len_ref, loc_ref, zero0_ref, nzero_ref, nreal_ref,
                     hx_ref, meta_ref, delta_ref, xb_ref, srt_ref, zero_ref, sem):
    t = pl.program_id(0)
    tm = hx_ref.shape[0]
    rows = _tile_rows(meta_ref, delta_ref)
    mlane = lax.broadcasted_iota(jnp.int32, (tm, LANES), 1)
    packed = jnp.full((tm, LANES), -1.0, F32)
    for k in range(TOP_K):
        packed = jnp.where(mlane == k, rows[k], packed)
    rows_t = packed.T
    slot = t % 2
    part = SORTED_ROWS // SORT_PARTS
    for c in range(SORT_PARTS):
        prow = (lax.broadcasted_iota(jnp.int32, (part, tm), 0) + c * part).astype(F32)
        onehot = jnp.zeros((part, tm), F32)
        for k in range(TOP_K):
            onehot = jnp.where(prow == rows_t[k:k + 1, :], 1.0, onehot)
        srt_ref[slot, c * part:(c + 1) * part] = _dot(onehot.astype(BF16), hx_ref[...])

    def copy_from(s):
        def copy(seg_row, loc_row, rows):
            return pltpu.make_async_copy(srt_ref.at[s, pl.ds(loc_row, rows)],
                                         xb_ref.at[pl.ds(seg_row, rows)], sem.at[s])
        return copy

    @pl.when(t > 0)
    def _():
        _chunk_copies(t - 1, dst0_ref, len_ref, loc_ref, copy_from(1 - slot), "wait")

    _chunk_copies(t, dst0_ref, len_ref, loc_ref, copy_from(slot), "start")

    @pl.when(t == pl.num_programs(0) - 1)
    def _():
        _chunk_copies(t, dst0_ref, len_ref, loc_ref, copy_from(slot), "wait")
        zero_ref[...] = jnp.zeros_like(zero_ref)
        for op in ("start", "wait"):
            for e in range(N_EXPERTS):
                def body(c, carry):
                    row0 = pl.multiple_of(zero0_ref[e] + c * SUBLANES, SUBLANES)
                    cp = pltpu.make_async_copy(zero_ref.at[pl.ds(0, SUBLANES)],
                                               xb_ref.at[pl.ds(row0, SUBLANES)], sem.at[0])
                    cp.start() if op == "start" else cp.wait()
                    return carry

                lax.fori_loop(0, nzero_ref[e], body, 0)

            def tail(b, carry):
                row0 = pl.multiple_of(b * FFN_BLOCK, FFN_BLOCK)
                cp = pltpu.make_async_copy(zero_ref, xb_ref.at[pl.ds(row0, FFN_BLOCK)], sem.at[0])
                cp.start() if op == "start" else cp.wait()
                return carry

            lax.fori_loop(nreal_ref[0], xb_ref.shape[0] // FFN_BLOCK, tail, 0)


def _dispatch(hx, meta, tbl):
    n, width = hx.shape
    tm = TOKEN_TILE
    grid_spec = pltpu.PrefetchScalarGridSpec(
        num_scalar_prefetch=6,
        grid=(n // tm,),
        in_specs=[pl.BlockSpec((tm, width), lambda i, *_: (i, 0)),
                  pl.BlockSpec((tm, LANES), lambda i, *_: (i, 0)),
                  pl.BlockSpec((1, 1, N_EXPERTS), lambda i, *_: (i, 0, 0))],
        out_specs=pl.BlockSpec(memory_space=pl.ANY),
        scratch_shapes=[pltpu.VMEM((2, SORTED_ROWS, width), F32),
                        pltpu.VMEM((FFN_BLOCK, width), F32),
                        pltpu.SemaphoreType.DMA((2,))],
    )
    return pl.pallas_call(
        _dispatch_kernel,
        grid_spec=grid_spec,
        out_shape=jax.ShapeDtypeStruct((tbl["cap"], width), F32),
        compiler_params=_params("arbitrary"),
        name="dispatch",
    )(tbl["dst0"], tbl["tile_len"], tbl["loc"], tbl["zero0"], tbl["nzero"], tbl["n_real"],
      hx, meta, tbl["delta"])


def _ffn_kernel(exp_ref, nreal_ref, x_ref, wgu_ref, bgu_ref, wd_ref, bd_ref, y_ref, wgu_bf, wd_bf):
    b = pl.program_id(0)
    d_ff, d = wd_ref.shape[1], wd_ref.shape[2]

    @pl.when(b >= nreal_ref[0])
    def _():
        y_ref[...] = jnp.zeros_like(y_ref)

    @pl.when((b == 0) | (exp_ref[b] != exp_ref[jnp.maximum(b - 1, 0)]))
    def _():
        wgu_bf[...] = wgu_ref[0].astype(BF16)
        wd_bf[...] = wd_ref[0].astype(BF16)

    @pl.when(b < nreal_ref[0])
    def _():
        x = x_ref[...]
        ext = x[:, d:]
        expert = exp_ref[b].astype(F32)
        gate = jnp.zeros((x.shape[0], 1), F32)
        for k in range(TOP_K):
            g0 = EXT_GATE + GATE_PARTS * k
            g = ext[:, g0:g0 + 1] + ext[:, g0 + 1:g0 + 2] + ext[:, g0 + 2:g0 + 3]
            gate = gate + jnp.where(ext[:, EXT_EXPERT + k:EXT_EXPERT + k + 1] == expert, g, 0.0)
        gu = _dot(x[:, :d].astype(BF16), wgu_bf[...]) + bgu_ref[0]
        g = jnp.minimum(gu[:, :d_ff], SWIGLU_LIMIT)
        u = jnp.clip(gu[:, d_ff:], -SWIGLU_LIMIT, SWIGLU_LIMIT)
        act = (u + 1.0) * (g * jax.nn.sigmoid(SWIGLU_ALPHA * g))
        y_ref[...] = gate * (_dot(act.astype(BF16), wd_bf[...]) + bd_ref[0])


def _ffn(xb, tbl, w_gate_up, b_gate_up, w_down, b_down):
    cap, width = xb.shape
    e, d, two_ff = w_gate_up.shape
    d_ff = two_ff // 2
    grid_spec = pltpu.PrefetchScalarGridSpec(
        num_scalar_prefetch=2,
        grid=(cap // FFN_BLOCK,),
        in_specs=[pl.BlockSpec((FFN_BLOCK, width), lambda b, ex, nr: (b, 0)),
                  pl.BlockSpec((1, d, two_ff), lambda b, ex, nr: (ex[b], 0, 0)),
                  pl.BlockSpec((1, 1, two_ff), lambda b, ex, nr: (ex[b], 0, 0)),
                  pl.BlockSpec((1, d_ff, d), lambda b, ex, nr: (ex[b], 0, 0)),
                  pl.BlockSpec((1, 1, d), lambda b, ex, nr: (ex[b], 0, 0))],
        out_specs=pl.BlockSpec((FFN_BLOCK, d), lambda b, ex, nr: (b, 0)),
        scratch_shapes=[pltpu.VMEM((d, two_ff), BF16), pltpu.VMEM((d_ff, d), BF16)],
    )
    return pl.pallas_call(
        _ffn_kernel,
        grid_spec=grid_spec,
        out_shape=jax.ShapeDtypeStruct((cap, d), F32),
        compiler_params=_params("arbitrary"),
        name="expert_ffn",
    )(tbl["blk_exp"], tbl["n_real"], xb, w_gate_up,
      b_gate_up.reshape(e, 1, two_ff), w_down, b_down.reshape(e, 1, d))


def _combine_kernel(dst0_ref, len_ref, loc_ref, meta_ref, delta_ref, h_ref, y_ref, o_ref, srt_ref, sem):
    t = pl.program_id(0)
    tm = h_ref.shape[0]

    @pl.when(t == 0)
    def _():
        srt_ref[...] = jnp.zeros_like(srt_ref)

    slot = t % 2

    def copy_into(s):
        def copy(seg_row, loc_row, rows):
            return pltpu.make_async_copy(y_ref.at[pl.ds(seg_row, rows)],
                                         srt_ref.at[s, pl.ds(loc_row, rows)], sem.at[s])
        return copy

    @pl.when(t == 0)
    def _():
        _chunk_copies(t, dst0_ref, len_ref, loc_ref, copy_into(slot), "start")

    @pl.when(t + 1 < pl.num_programs(0))
    def _():
        _chunk_copies(t + 1, dst0_ref, len_ref, loc_ref, copy_into(1 - slot), "start")

    rows = _tile_rows(meta_ref, delta_ref)
    part = SORTED_ROWS // SORT_PARTS

    def onehot_part(c):
        pcol = (lax.broadcasted_iota(jnp.int32, (tm, part), 1) + c * part).astype(F32)
        onehot = jnp.zeros((tm, part), F32)
        for k in range(TOP_K):
            onehot = jnp.where(pcol == rows[k], 1.0, onehot)
        return onehot.astype(BF16)

    first = onehot_part(0)
    _chunk_copies(t, dst0_ref, len_ref, loc_ref, copy_into(slot), "wait")
    out = h_ref[...]
    for c in range(SORT_PARTS):
        onehot = first if c == 0 else onehot_part(c)
        out = out + _dot(onehot, srt_ref[slot, c * part:(c + 1) * part].astype(BF16))
    o_ref[...] = out


def _combine(meta, tbl, h1, yb):
    n, d = h1.shape
    tm = TOKEN_TILE
    grid_spec = pltpu.PrefetchScalarGridSpec(
        num_scalar_prefetch=3,
        grid=(n // tm,),
        in_specs=[pl.BlockSpec((tm, LANES), lambda i, *_: (i, 0)),
                  pl.BlockSpec((1, 1, N_EXPERTS), lambda i, *_: (i, 0, 0)),
                  pl.BlockSpec((tm, d), lambda i, *_: (i, 0)),
                  pl.BlockSpec(memory_space=pl.ANY)],
        out_specs=pl.BlockSpec((tm, d), lambda i, *_: (i, 0)),
        scratch_shapes=[pltpu.VMEM((2, SORTED_ROWS, d), F32), pltpu.SemaphoreType.DMA((2,))],
    )
    return pl.pallas_call(
        _combine_kernel,
        grid_spec=grid_spec,
        out_shape=jax.ShapeDtypeStruct((n, d), F32),
        compiler_params=_params("arbitrary"),
        name="combine",
    )(tbl["dst0"], tbl["tile_len"], tbl["loc"], meta, tbl["delta"], h1, yb)


def kernel(x, attn_norm_w, w_in, q_norm_w, k_norm_w, sinks, w_proj_swa, w_proj_sb, w_out,
           ffn_norm_w, router_w, router_b, w_gate_up, b_gate_up, w_down, b_down):
    batch, seq, d = x.shape
    n = batch * seq
    h = x.reshape(n, d)
    for layer in range(attn_norm_w.shape[0]):
        q_a, kv_a, q_b, k_b, v_b, g_a, g_b = _in_proj(h, attn_norm_w[layer], w_in[layer].astype(BF16))
        y_a = _swa(q_a, kv_a, q_norm_w[layer], k_norm_w[layer], sinks[layer], batch, seq)
        y_b = _sb(q_b, k_b, v_b, batch, seq)
        h1 = _merge(y_a, y_b, g_a, g_b, h, w_proj_swa[layer].astype(BF16),
                    w_proj_sb[layer].astype(BF16), w_out[layer].astype(BF16))
        hx, meta, tile_run, counts = _router(h1, ffn_norm_w[layer], router_w[layer], router_b[layer])
        tbl = _routing_tables(tile_run, counts, n)
        xb = _dispatch(hx, meta, tbl)
        yb = _ffn(xb, tbl, w_gate_up[layer], b_gate_up[layer], w_down[layer], b_down[layer])
        h = _combine(meta, tbl, h1, yb)
    return h.reshape(batch, seq, d)
```

```python
import jax
import jax.numpy as jnp
from jax import lax
from jax.experimental import pallas as pl
from jax.experimental.pallas import tpu as pltpu

HEAD_DIM = 64
SWA_HEADS = 8
SWA_KV_HEADS = 2
SWA_GROUP = SWA_HEADS // SWA_KV_HEADS
BLOCK = 128
SB_HEADS = 8
SWA_WIDTH = SWA_HEADS * HEAD_DIM
SWA_KV_WIDTH = SWA_KV_HEADS * HEAD_DIM
SB_WIDTH = SB_HEADS * HEAD_DIM
N_EXPERTS = 32
TOP_K = 4
SWIGLU_LIMIT = 7.0
SWIGLU_ALPHA = 1.702
NORM_EPS = 1e-5
ATTN_SCALE = HEAD_DIM ** -0.5
SIGN_BIT = -2 ** 31

LANES = 128
SUBLANES = 8
TOKEN_TILE = 512
FFN_BLOCK = 512
CHUNK = 32
SORTED_ROWS = -(-(TOKEN_TILE * TOP_K + N_EXPERTS * (SUBLANES - 1)) // LANES) * LANES
SORT_PARTS = 2
SB_Q_ROWS = 1024
SB_K_ROWS = 256
SB_GROUP = 1
SB_STAGGER = 1
VMEM_LIMIT = 56 * 1024 * 1024

F32 = jnp.float32
BF16 = jnp.bfloat16
NEG_BIG = -1e30


def _dot(a, b):
    return jnp.dot(a, b, preferred_element_type=F32)


def _dot_nt(a, b):
    return lax.dot_general(a, b, (((1,), (1,)), ((), ())), preferred_element_type=F32)


def _sigmoid(x):
    return jax.nn.sigmoid(x)


def _rms(x, w):
    return x * lax.rsqrt(jnp.mean(x * x, axis=-1, keepdims=True) + NORM_EPS) * w


def _params(*sem):
    return pltpu.CompilerParams(dimension_semantics=sem, vmem_limit_bytes=VMEM_LIMIT)


def _in_proj_kernel(x_ref, nw_ref, w_ref, qa_ref, kva_ref, qb_ref, kb_ref, vb_ref, ga_ref, gb_ref):
    xn = _rms(x_ref[...], nw_ref[...]).astype(BF16)
    off = 0
    for ref, gate in ((qa_ref, False), (kva_ref, False), (qb_ref, False), (kb_ref, False),
                      (vb_ref, False), (ga_ref, True), (gb_ref, True)):
        width = ref.shape[1]
        y = _dot(xn, w_ref[:, off:off + width])
        if gate:
            y = _sigmoid(y)
        ref[...] = y.astype(ref.dtype)
        off += width


def _in_proj(x2, norm_w, w_in):
    n, d = x2.shape
    widths = (SWA_WIDTH, 2 * SWA_KV_WIDTH, SB_WIDTH, SB_WIDTH, SB_WIDTH, d, d)
    tm = TOKEN_TILE
    return pl.pallas_call(
        _in_proj_kernel,
        grid=(n // tm,),
        in_specs=[pl.BlockSpec((tm, d), lambda i: (i, 0)),
                  pl.BlockSpec((1, d), lambda i: (0, 0)),
                  pl.BlockSpec(w_in.shape, lambda i: (0, 0))],
        out_specs=[pl.BlockSpec((tm, w), lambda i: (i, 0)) for w in widths],
        out_shape=[jax.ShapeDtypeStruct((n, w), BF16) for w in widths],
        compiler_params=_params("parallel"),
        name="in_proj",
    )(x2, norm_w.reshape(1, d), w_in)


def _swa_kernel(q_ref, kvc_ref, kvp_ref, qnw_ref, knw_ref, sink_ref, o_ref):
    i = pl.program_id(1)
    half = HEAD_DIM

    def group_mean_sq(x):
        w = x.shape[1]
        same_head = (lax.broadcasted_iota(jnp.int32, (w, w), 0) // half
                     == lax.broadcasted_iota(jnp.int32, (w, w), 1) // half)
        avg = jnp.where(same_head, 1.0 / half, 0.0).astype(BF16)
        hi, lo = _split_bf16(x * x)
        return _dot(hi, avg) + _dot(lo, avg)

    q = q_ref[...].astype(F32)
    kv = jnp.concatenate([kvp_ref[...], kvc_ref[...]], axis=0)
    k = kv[:, :SWA_KV_WIDTH].astype(F32)
    v = kv[:, SWA_KV_WIDTH:].astype(F32)
    qn = (q * lax.rsqrt(group_mean_sq(q) + NORM_EPS) * qnw_ref[...] * ATTN_SCALE).astype(BF16)
    kn = k * lax.rsqrt(group_mean_sq(k) + NORM_EPS) * knw_ref[...]
    lane = lax.broadcasted_iota(jnp.int32, (2 * BLOCK, LANES), 1)
    low = lane < half
    k_swap = pltpu.roll(kn, half, 1)
    v_swap = pltpu.roll(v, half, 1).astype(BF16)
    v_same = v.astype(BF16)
    keys = [[jnp.where(low, kn, 0.0).astype(BF16), jnp.where(low, 0.0, k_swap).astype(BF16)],
            [jnp.where(low, k_swap, 0.0).astype(BF16), jnp.where(low, 0.0, kn).astype(BF16)]]
    vals = [[v_same, v_swap], [v_swap, v_same]]

    row = lax.broadcasted_iota(jnp.int32, (BLOCK, 2 * BLOCK), 0)
    col = lax.broadcasted_iota(jnp.int32, (BLOCK, 2 * BLOCK), 1)
    rel = row + BLOCK - col
    valid = (rel >= 0) & (rel < BLOCK) & ((col >= BLOCK) | (i > 0))
    rel = rel.astype(F32)
    out_lane_low = lax.broadcasted_iota(jnp.int32, (BLOCK, LANES), 1) < half
    for blk in range(SWA_HEADS // 2):
        qb = qn[:, blk * LANES:(blk + 1) * LANES]
        halves = []
        for par in range(2):
            h = 2 * blk + par
            g = h // SWA_GROUP
            slope = 2.0 ** (-8.0 * (h + 1) / SWA_HEADS)
            s = jnp.where(valid, _dot_nt(qb, keys[g][par]) - slope * rel, NEG_BIG)
            sink = sink_ref[h]
            m = jnp.maximum(jnp.max(s, axis=-1, keepdims=True), sink)
            p = jnp.exp(s - m)
            den = jnp.sum(p, axis=-1, keepdims=True) + jnp.exp(sink - m)
            halves.append(_dot(p.astype(BF16), vals[g][par]) / den)
        o_ref[:, blk * LANES:(blk + 1) * LANES] = jnp.where(out_lane_low, halves[0], halves[1]).astype(o_ref.dtype)


def _swa(q_a, kv_a, q_norm_w, k_norm_w, sinks, batch, seq):
    n = q_a.shape[0]
    nb = seq // BLOCK
    return pl.pallas_call(
        _swa_kernel,
        grid=(batch, nb),
        in_specs=[pl.BlockSpec((BLOCK, SWA_WIDTH), lambda b, i: (b * nb + i, 0)),
                  pl.BlockSpec((BLOCK, 2 * SWA_KV_WIDTH), lambda b, i: (b * nb + i, 0)),
                  pl.BlockSpec((BLOCK, 2 * SWA_KV_WIDTH), lambda b, i: (b * nb + jnp.maximum(i - 1, 0), 0)),
                  pl.BlockSpec((1, SWA_WIDTH), lambda b, i: (0, 0)),
                  pl.BlockSpec((1, SWA_KV_WIDTH), lambda b, i: (0, 0)),
                  pl.BlockSpec(memory_space=pltpu.SMEM)],
        out_specs=pl.BlockSpec((BLOCK, SWA_WIDTH), lambda b, i: (b * nb + i, 0)),
        out_shape=jax.ShapeDtypeStruct((n, SWA_WIDTH), BF16),
        compiler_params=_params("parallel", "parallel"),
        name="swa",
    )(q_a, kv_a, kv_a, jnp.tile(q_norm_w, SWA_HEADS).reshape(1, SWA_WIDTH),
      jnp.tile(k_norm_w, SWA_KV_HEADS).reshape(1, SWA_KV_WIDTH), sinks)


def _split_bf16(x):
    hi = x.astype(BF16)
    lo = (x - hi.astype(F32)).astype(BF16)
    return hi, lo


def _sb_kernel(q_ref, k_ref, v_ref, o_ref):
    seq = q_ref.shape[0]
    tq, tk = SB_Q_ROWS, SB_K_ROWS
    nq = seq // tq
    k_per_q = tq // tk
    lane = lax.broadcasted_iota(jnp.int32, (tq, LANES), 1)
    row = lax.broadcasted_iota(jnp.int32, (tq, tk), 0)
    col = lax.broadcasted_iota(jnp.int32, (tq, tk), 1)
    neg_incl = jnp.where(lax.broadcasted_iota(jnp.int32, (tk, tk), 0)
                         >= lax.broadcasted_iota(jnp.int32, (tk, tk), 1), -1.0, 0.0).astype(BF16)

    def softplus(z):
        neg_abs = lax.bitcast_convert_type(lax.bitcast_convert_type(z, jnp.int32) | SIGN_BIT, F32)
        return jnp.maximum(z, 0.0) + jnp.log(1.0 + jnp.exp(neg_abs))

    def neg_suffix_sum(sp):
        return _dot(sp.astype(BF16), neg_incl)

    def add_from_row(x, r0, delta):
        return x + delta if r0 == 0 else jnp.concatenate([x[:r0], x[r0:] + delta], axis=0)

    def k_span(k0s, qms, state, masks, first_rows):
        pairs = [(i, p) for p in range(2) for i in range(len(k0s))]
        kbs = [k_ref[pl.ds(k0, tk), :] for k0 in k0s]
        vbs = [v_ref[pl.ds(k0, tk), :] for k0 in k0s]
        accs, runs = list(state[:2]), list(state[2:])
        zs, sps, cs, ws = {}, {}, {}, {}

        def stage(s, i, p):
            r0 = first_rows[i]
            if s == 0:
                zs[(i, p)] = _dot_nt(qms[p][r0:], kbs[i])
            elif s == 1:
                sp = softplus(zs[(i, p)])
                sps[(i, p)] = sp if masks[i] is None else jnp.where(masks[i], sp, 0.0)
            elif s == 2:
                cs[(i, p)] = neg_suffix_sum(sps[(i, p)])
            elif s == 3:
                w = jnp.exp(zs[(i, p)] + cs[(i, p)] + runs[p][r0:])
                ws[(i, p)] = w if masks[i] is None else jnp.where(masks[i], w, 0.0)
                runs[p] = add_from_row(runs[p], r0, cs[(i, p)][:, 0:1])
            else:
                accs[p] = add_from_row(accs[p], r0, _dot(ws[(i, p)].astype(BF16), vbs[i]))

        n_stages = 5
        for step in range(((len(pairs) - 1) // SB_GROUP) * SB_STAGGER + n_stages):
            for n, (i, p) in enumerate(pairs):
                s = step - (n // SB_GROUP) * SB_STAGGER
                if 0 <= s < n_stages:
                    stage(s, i, p)
        return (*accs, *runs)

    def q_block(qi, carry):
        q0 = pl.multiple_of(qi * tq, tq)
        q2 = q_ref[pl.ds(q0, tq), :] * ATTN_SCALE
        qms = (jnp.where(lane < HEAD_DIM, q2, 0.0).astype(BF16),
               jnp.where(lane >= HEAD_DIM, q2, 0.0).astype(BF16))
        state = (jnp.zeros((tq, LANES), F32), jnp.zeros((tq, LANES), F32),
                 jnp.zeros((tq, 1), F32), jnp.zeros((tq, 1), F32))
        order = list(reversed(range(k_per_q)))
        state = k_span([q0 + d * tk for d in order], qms, state,
                       [((col + d * tk) < row)[d * tk:] for d in order], [d * tk for d in order])

        def earlier(jj, state):
            k0s = [pl.multiple_of((qi - 1 - jj) * tq + d * tk, tk) for d in order]
            return k_span(k0s, qms, state, [None] * k_per_q, [0] * k_per_q)

        state = lax.fori_loop(0, qi, earlier, state)
        o_ref[pl.ds(q0, tq), :] = jnp.where(lane < HEAD_DIM, state[0], state[1]).astype(o_ref.dtype)
        return carry

    lax.fori_loop(0, nq, q_block, 0)


def _sb(q_b, k_b, v_b, batch, seq):
    n = q_b.shape[0]
    spec = pl.BlockSpec((seq, LANES), lambda b, p: (b, p))
    return pl.pallas_call(
        _sb_kernel,
        grid=(batch, SB_WIDTH // LANES),
        in_specs=[spec, spec, spec],
        out_specs=spec,
        out_shape=jax.ShapeDtypeStruct((n, SB_WIDTH), BF16),
        compiler_params=_params("parallel", "parallel"),
        name="stick_breaking",
    )(q_b, k_b, v_b)


def _merge_kernel(ya_ref, yb_ref, ga_ref, gb_ref, x_ref, pa_ref, pb_ref, wo_ref, h_ref):
    m = (ga_ref[...].astype(F32) * _dot(ya_ref[...], pa_ref[...])
         + gb_ref[...].astype(F32) * _dot(yb_ref[...], pb_ref[...]))
    h_ref[...] = x_ref[...] + _dot(m.astype(BF16), wo_ref[...])


def _merge(y_a, y_b, g_a, g_b, x2, p_a, p_b, w_o):
    n, d = x2.shape
    tm = TOKEN_TILE
    row = lambda w: pl.BlockSpec((tm, w), lambda i: (i, 0))
    full = lambda a: pl.BlockSpec(a.shape, lambda i: (0, 0))
    return pl.pallas_call(
        _merge_kernel,
        grid=(n // tm,),
        in_specs=[row(SWA_WIDTH), row(SB_WIDTH), row(d), row(d), row(d), full(p_a), full(p_b), full(w_o)],
        out_specs=row(d),
        out_shape=jax.ShapeDtypeStruct((n, d), F32),
        compiler_params=_params("parallel"),
        name="merge",
    )(y_a, y_b, g_a, g_b, x2, p_a, p_b, w_o)


META_EXPERT = 0
META_RANK = TOP_K
EXT_EXPERT = 0
EXT_GATE = TOP_K
GATE_PARTS = 3


def _router_kernel(h_ref, nw_ref, rw_ref, rb_ref, hx_ref, meta_ref, trun_ref, cnt_ref, run_ref):
    tm, d = h_ref.shape

    @pl.when(pl.program_id(0) == 0)
    def _():
        run_ref[...] = jnp.zeros_like(run_ref)

    hn = _rms(h_ref[...], nw_ref[...])
    hn_hi, hn_lo = _split_bf16(hn)
    both = _dot(hn_hi, rw_ref[...])
    logits = (both[:, :N_EXPERTS] + both[:, N_EXPERTS:] + _dot(hn_lo, rw_ref[:, :N_EXPERTS])
              + rb_ref[...])
    lane = lax.broadcasted_iota(jnp.int32, (tm, N_EXPERTS), 1).astype(F32)
    work = logits
    tops, idxs, hots = [], [], []
    for _ in range(TOP_K):
        m = jnp.max(work, axis=-1, keepdims=True)
        idx = jnp.min(jnp.where(work == m, lane, float(N_EXPERTS)), axis=-1, keepdims=True)
        hot = lane == idx
        work = jnp.where(hot, -jnp.inf, work)
        tops.append(m)
        idxs.append(idx)
        hots.append(hot)
    exps = [jnp.exp(t - tops[0]) for t in tops]
    den = exps[0] + exps[1] + exps[2] + exps[3]
    chosen = jnp.where(hots[0] | hots[1] | hots[2] | hots[3], 1.0, 0.0)
    r = lax.broadcasted_iota(jnp.int32, (tm, tm), 0)
    c = lax.broadcasted_iota(jnp.int32, (tm, tm), 1)
    before = jnp.where(c < r, 1.0, 0.0).astype(BF16)
    run = run_ref[...]
    prefix = _dot(before, chosen.astype(BF16)) + run
    mlane = lax.broadcasted_iota(jnp.int32, (tm, LANES), 1)
    meta = jnp.zeros((tm, LANES), F32)
    ext = jnp.zeros((tm, LANES), F32)
    for k in range(TOP_K):
        rank = jnp.sum(jnp.where(hots[k], prefix, 0.0), axis=-1, keepdims=True)
        meta = jnp.where(mlane == META_EXPERT + k, idxs[k], meta)
        meta = jnp.where(mlane == META_RANK + k, rank, meta)
        ext = jnp.where(mlane == EXT_EXPERT + k, idxs[k], ext)
        rest = exps[k] / den
        for part in range(GATE_PARTS):
            piece = rest.astype(BF16).astype(F32)
            ext = jnp.where(mlane == EXT_GATE + GATE_PARTS * k + part, piece, ext)
            rest = rest - piece
    meta_ref[...] = meta
    hx_ref[:, :d] = hn_hi
    hx_ref[:, d:] = ext.astype(BF16)
    trun_ref[0] = run
    tile_cnt = jnp.sum(chosen, axis=0, keepdims=True)
    run = run + jnp.ceil(tile_cnt / SUBLANES) * SUBLANES
    run_ref[...] = run
    cnt_ref[...] = run


def _router(h1, norm_w, router_w, router_b):
    n, d = h1.shape
    tm = TOKEN_TILE
    tiles = n // tm
    return pl.pallas_call(
        _router_kernel,
        grid=(tiles,),
        in_specs=[pl.BlockSpec((tm, d), lambda i: (i, 0)),
                  pl.BlockSpec((1, d), lambda i: (0, 0)),
                  pl.BlockSpec((d, 2 * N_EXPERTS), lambda i: (0, 0)),
                  pl.BlockSpec((1, N_EXPERTS), lambda i: (0, 0))],
        out_specs=[pl.BlockSpec((tm, d + LANES), lambda i: (i, 0)),
                   pl.BlockSpec((tm, LANES), lambda i: (i, 0)),
                   pl.BlockSpec((1, 1, N_EXPERTS), lambda i: (i, 0, 0)),
                   pl.BlockSpec((1, N_EXPERTS), lambda i: (0, 0))],
        out_shape=[jax.ShapeDtypeStruct((n, d + LANES), BF16),
                   jax.ShapeDtypeStruct((n, LANES), F32),
                   jax.ShapeDtypeStruct((tiles, 1, N_EXPERTS), F32),
                   jax.ShapeDtypeStruct((1, N_EXPERTS), F32)],
        scratch_shapes=[pltpu.VMEM((1, N_EXPERTS), F32)],
        compiler_params=_params("arbitrary"),
        name="router",
    )(h1, norm_w.reshape(1, d), jnp.concatenate(_split_bf16(router_w), axis=1),
      router_b.reshape(1, N_EXPERTS))


def _routing_tables(tile_run, counts, n):
    tiles = tile_run.shape[0]
    counts = counts.reshape(N_EXPERTS).astype(jnp.int32)
    run = tile_run.reshape(tiles, N_EXPERTS).astype(jnp.int32)
    tile_len = jnp.concatenate([run[1:], counts[None]], axis=0) - run
    loc = jnp.cumsum(tile_len, axis=1) - tile_len
    padded = (counts + FFN_BLOCK - 1) // FFN_BLOCK * FFN_BLOCK
    seg_end = jnp.cumsum(padded)
    seg_start = seg_end - padded
    dst0 = seg_start[None, :] + run
    delta = (loc - run).astype(F32).reshape(tiles, 1, N_EXPERTS)
    zero0 = seg_start + counts
    nzero = (padded - counts) // SUBLANES
    n_real = seg_end[-1] // FFN_BLOCK
    max_blocks = (n * TOP_K + tiles * N_EXPERTS * (SUBLANES - 1)
                  + N_EXPERTS * (FFN_BLOCK - 1)) // FFN_BLOCK
    blk_exp = jnp.minimum(jnp.sum(jnp.arange(max_blocks, dtype=jnp.int32)[:, None] * FFN_BLOCK
                                  >= seg_end[None, :], axis=1), N_EXPERTS - 1).astype(jnp.int32)
    ncopy = [jnp.sum(tile_len // CHUNK, axis=1)]
    ncopy += [jnp.sum((tile_len // rows) % 2, axis=1) for rows in _copy_sizes()[1:]]
    ncopy = jnp.stack(ncopy, axis=1).astype(jnp.int32)
    return dict(dst0=dst0.reshape(-1), tile_len=tile_len.reshape(-1), loc=loc.reshape(-1),
                ncopy=ncopy.reshape(-1), delta=delta, zero0=zero0, nzero=nzero, blk_exp=blk_exp,
                n_real=n_real.reshape(1).astype(jnp.int32), cap=max_blocks * FFN_BLOCK)


def _tile_rows(meta_ref, delta_ref):
    meta = meta_ref[...]
    tm = meta.shape[0]
    delta = delta_ref[0]
    lane = lax.broadcasted_iota(jnp.int32, (tm, N_EXPERTS), 1).astype(F32)
    rows = []
    for k in range(TOP_K):
        expert = meta[:, META_EXPERT + k:META_EXPERT + k + 1]
        rank = meta[:, META_RANK + k:META_RANK + k + 1]
        rows.append(rank + jnp.sum(jnp.where(lane == expert, delta, 0.0), axis=-1, keepdims=True))
    return rows


def _copy_sizes():
    sizes, rows = [CHUNK], CHUNK // 2
    while rows >= SUBLANES:
        sizes.append(rows)
        rows //= 2
    return sizes


def _wait_copies(t, ncopy_ref, make_copy):
    sizes = _copy_sizes()
    for c, rows in enumerate(sizes):
        def body(_, carry, rows=rows):
            make_copy(0, 0, rows).wait()
            return carry

        lax.fori_loop(0, ncopy_ref[t * len(sizes) + c], body, 0)


def _start_copies(t, tbl_ref, len_ref, loc_ref, make_copy):
    def one(seg_row, loc_row, rows):
        make_copy(pl.multiple_of(seg_row, SUBLANES), pl.multiple_of(loc_row, SUBLANES), rows).start()

    for e in range(N_EXPERTS):
        idx = t * N_EXPERTS + e
        seg_row = tbl_ref[idx]
        loc_row = loc_ref[idx]
        length = len_ref[idx]
        full = length // CHUNK

        def body(c, carry):
            one(seg_row + c * CHUNK, loc_row + c * CHUNK, CHUNK)
            return carry

        lax.fori_loop(0, full, body, 0)
        done = full * CHUNK
        rows = CHUNK // 2
        while rows >= SUBLANES:
            @pl.when((length & rows) != 0)
            def _(done=done, rows=rows):
                one(seg_row + done, loc_row + done, rows)
            done = done + (length & rows)
            rows //= 2


def _dispatch_kernel(dst0_ref, len_ref, loc_ref, ncopy_ref, zero0_ref, nzero_ref, nreal_ref,
                     hx_ref, meta_ref, delta_ref, xb_ref, srt_ref, zero_ref, sem):
    t = pl.program_id(0)
    tm = hx_ref.shape[0]
    rows = _tile_rows(meta_ref, delta_ref)
    mlane = lax.broadcasted_iota(jnp.int32, (tm, LANES), 1)
    packed = jnp.full((tm, LANES), -1.0, F32)
    for k in range(TOP_K):
        packed = jnp.where(mlane == k, rows[k], packed)
    rows_t = packed.T
    slot = t % 2
    part = SORTED_ROWS // SORT_PARTS
    for c in range(SORT_PARTS):
        prow = (lax.broadcasted_iota(jnp.int32, (part, tm), 0) + c * part).astype(F32)
        onehot = jnp.zeros((part, tm), F32)
        for k in range(TOP_K):
            onehot = jnp.where(prow == rows_t[k:k + 1, :], 1.0, onehot)
        srt_ref[slot, c * part:(c + 1) * part] = _dot(onehot.astype(BF16), hx_ref[...])

    def copy_from(s):
        def copy(seg_row, loc_row, rows):
            return pltpu.make_async_copy(srt_ref.at[s, pl.ds(loc_row, rows)],
                                         xb_ref.at[pl.ds(seg_row, rows)], sem.at[s])
        return copy

    @pl.when(t > 0)
    def _():
        _wait_copies(t - 1, ncopy_ref, copy_from(1 - slot))

    _start_copies(t, dst0_ref, len_ref, loc_ref, copy_from(slot))

    @pl.when(t == pl.num_programs(0) - 1)
    def _():
        _wait_copies(t, ncopy_ref, copy_from(slot))
        zero_ref[...] = jnp.zeros_like(zero_ref)
        for op in ("start", "wait"):
            for e in range(N_EXPERTS):
                def body(c, carry):
                    row0 = pl.multiple_of(zero0_ref[e] + c * SUBLANES, SUBLANES)
                    cp = pltpu.make_async_copy(zero_ref.at[pl.ds(0, SUBLANES)],
                                               xb_ref.at[pl.ds(row0, SUBLANES)], sem.at[0])
                    cp.start() if op == "start" else cp.wait()
                    return carry

                lax.fori_loop(0, nzero_ref[e], body, 0)

            def tail(b, carry):
                row0 = pl.multiple_of(b * FFN_BLOCK, FFN_BLOCK)
                cp = pltpu.make_async_copy(zero_ref, xb_ref.at[pl.ds(row0, FFN_BLOCK)], sem.at[0])
                cp.start() if op == "start" else cp.wait()
                return carry

            lax.fori_loop(nreal_ref[0], xb_ref.shape[0] // FFN_BLOCK, tail, 0)


def _dispatch(hx, meta, tbl):
    n, width = hx.shape
    tm = TOKEN_TILE
    grid_spec = pltpu.PrefetchScalarGridSpec(
        num_scalar_prefetch=7,
        grid=(n // tm,),
        in_specs=[pl.BlockSpec((tm, width), lambda i, *_: (i, 0)),
                  pl.BlockSpec((tm, LANES), lambda i, *_: (i, 0)),
                  pl.BlockSpec((1, 1, N_EXPERTS), lambda i, *_: (i, 0, 0))],
        out_specs=pl.BlockSpec(memory_space=pl.ANY),
        scratch_shapes=[pltpu.VMEM((2, SORTED_ROWS, width), F32),
                        pltpu.VMEM((FFN_BLOCK, width), F32),
                        pltpu.SemaphoreType.DMA((2,))],
    )
    return pl.pallas_call(
        _dispatch_kernel,
        grid_spec=grid_spec,
        out_shape=jax.ShapeDtypeStruct((tbl["cap"], width), F32),
        compiler_params=_params("arbitrary"),
        name="dispatch",
    )(tbl["dst0"], tbl["tile_len"], tbl["loc"], tbl["ncopy"], tbl["zero0"], tbl["nzero"], tbl["n_real"],
      hx, meta, tbl["delta"])


def _ffn_kernel(exp_ref, nreal_ref, x_ref, wgu_ref, bgu_ref, wd_ref, bd_ref, y_ref, wgu_bf, wd_bf):
    b = pl.program_id(0)
    d_ff, d = wd_ref.shape[1], wd_ref.shape[2]

    @pl.when(b >= nreal_ref[0])
    def _():
        y_ref[...] = jnp.zeros_like(y_ref)

    @pl.when((b == 0) | (exp_ref[b] != exp_ref[jnp.maximum(b - 1, 0)]))
    def _():
        wgu_bf[...] = wgu_ref[0].astype(BF16)
        wd_bf[...] = wd_ref[0].astype(BF16)

    @pl.when(b < nreal_ref[0])
    def _():
        x = x_ref[...]
        ext = x[:, d:]
        expert = exp_ref[b].astype(F32)
        gate = jnp.zeros((x.shape[0], 1), F32)
        for k in range(TOP_K):
            g0 = EXT_GATE + GATE_PARTS * k
            g = ext[:, g0:g0 + 1] + ext[:, g0 + 1:g0 + 2] + ext[:, g0 + 2:g0 + 3]
            gate = gate + jnp.where(ext[:, EXT_EXPERT + k:EXT_EXPERT + k + 1] == expert, g, 0.0)
        gu = _dot(x[:, :d].astype(BF16), wgu_bf[...]) + bgu_ref[0]
        g = jnp.minimum(gu[:, :d_ff], SWIGLU_LIMIT)
        u = jnp.clip(gu[:, d_ff:], -SWIGLU_LIMIT, SWIGLU_LIMIT)
        act = (u + 1.0) * (g * jax.nn.sigmoid(SWIGLU_ALPHA * g))
        y_ref[...] = gate * (_dot(act.astype(BF16), wd_bf[...]) + bd_ref[0])


def _ffn(xb, tbl, w_gate_up, b_gate_up, w_down, b_down):
    cap, width = xb.shape
    e, d, two_ff = w_gate_up.shape
    d_ff = two_ff // 2
    grid_spec = pltpu.PrefetchScalarGridSpec(
        num_scalar_prefetch=2,
        grid=(cap // FFN_BLOCK,),
        in_specs=[pl.BlockSpec((FFN_BLOCK, width), lambda b, ex, nr: (b, 0)),
                  pl.BlockSpec((1, d, two_ff), lambda b, ex, nr: (ex[b], 0, 0)),
                  pl.BlockSpec((1, 1, two_ff), lambda b, ex, nr: (ex[b], 0, 0)),
                  pl.BlockSpec((1, d_ff, d), lambda b, ex, nr: (ex[b], 0, 0)),
                  pl.BlockSpec((1, 1, d), lambda b, ex, nr: (ex[b], 0, 0))],
        out_specs=pl.BlockSpec((FFN_BLOCK, d), lambda b, ex, nr: (b, 0)),
        scratch_shapes=[pltpu.VMEM((d, two_ff), BF16), pltpu.VMEM((d_ff, d), BF16)],
    )
    return pl.pallas_call(
        _ffn_kernel,
        grid_spec=grid_spec,
        out_shape=jax.ShapeDtypeStruct((cap, d), F32),
        compiler_params=_params("arbitrary"),
        name="expert_ffn",
    )(tbl["blk_exp"], tbl["n_real"], xb, w_gate_up,
      b_gate_up.reshape(e, 1, two_ff), w_down, b_down.reshape(e, 1, d))


def _combine_kernel(dst0_ref, len_ref, loc_ref, ncopy_ref, meta_ref, delta_ref, h_ref, y_ref, o_ref,
                    srt_ref, sem):
    t = pl.program_id(0)
    tm = h_ref.shape[0]

    @pl.when(t == 0)
    def _():
        srt_ref[...] = jnp.zeros_like(srt_ref)

    slot = t % 2

    def copy_into(s):
        def copy(seg_row, loc_row, rows):
            return pltpu.make_async_copy(y_ref.at[pl.ds(seg_row, rows)],
                                         srt_ref.at[s, pl.ds(loc_row, rows)], sem.at[s])
        return copy

    @pl.when(t == 0)
    def _():
        _start_copies(t, dst0_ref, len_ref, loc_ref, copy_into(slot))

    @pl.when(t + 1 < pl.num_programs(0))
    def _():
        _start_copies(t + 1, dst0_ref, len_ref, loc_ref, copy_into(1 - slot))

    rows = _tile_rows(meta_ref, delta_ref)
    part = SORTED_ROWS // SORT_PARTS

    def onehot_part(c):
        pcol = (lax.broadcasted_iota(jnp.int32, (tm, part), 1) + c * part).astype(F32)
        onehot = jnp.zeros((tm, part), F32)
        for k in range(TOP_K):
            onehot = jnp.where(pcol == rows[k], 1.0, onehot)
        return onehot.astype(BF16)

    first = onehot_part(0)
    _wait_copies(t, ncopy_ref, copy_into(slot))
    out = h_ref[...]
    for c in range(SORT_PARTS):
        onehot = first if c == 0 else onehot_part(c)
        out = out + _dot(onehot, srt_ref[slot, c * part:(c + 1) * part].astype(BF16))
    o_ref[...] = out


def _combine(meta, tbl, h1, yb):
    n, d = h1.shape
    tm = TOKEN_TILE
    grid_spec = pltpu.PrefetchScalarGridSpec(
        num_scalar_prefetch=4,
        grid=(n // tm,),
        in_specs=[pl.BlockSpec((tm, LANES), lambda i, *_: (i, 0)),
                  pl.BlockSpec((1, 1, N_EXPERTS), lambda i, *_: (i, 0, 0)),
                  pl.BlockSpec((tm, d), lambda i, *_: (i, 0)),
                  pl.BlockSpec(memory_space=pl.ANY)],
        out_specs=pl.BlockSpec((tm, d), lambda i, *_: (i, 0)),
        scratch_shapes=[pltpu.VMEM((2, SORTED_ROWS, d), F32), pltpu.SemaphoreType.DMA((2,))],
    )
    return pl.pallas_call(
        _combine_kernel,
        grid_spec=grid_spec,
        out_shape=jax.ShapeDtypeStruct((n, d), F32),
        compiler_params=_params("arbitrary"),
        name="combine",
    )(tbl["dst0"], tbl["tile_len"], tbl["loc"], tbl["ncopy"], meta, tbl["delta"], h1, yb)


def kernel(x, attn_norm_w, w_in, q_norm_w, k_norm_w, sinks, w_proj_swa, w_proj_sb, w_out,
           ffn_norm_w, router_w, router_b, w_gate_up, b_gate_up, w_down, b_down):
    batch, seq, d = x.shape
    n = batch * seq
    h = x.reshape(n, d)
    for layer in range(attn_norm_w.shape[0]):
        q_a, kv_a, q_b, k_b, v_b, g_a, g_b = _in_proj(h, attn_norm_w[layer], w_in[layer].astype(BF16))
        y_a = _swa(q_a, kv_a, q_norm_w[layer], k_norm_w[layer], sinks[layer], batch, seq)
        y_b = _sb(q_b, k_b, v_b, batch, seq)
        h1 = _merge(y_a, y_b, g_a, g_b, h, w_proj_swa[layer].astype(BF16),
                    w_proj_sb[layer].astype(BF16), w_out[layer].astype(BF16))
        hx, meta, tile_run, counts = _router(h1, ffn_norm_w[layer], router_w[layer], router_b[layer])
        tbl = _routing_tables(tile_run, counts, n)
        xb = _dispatch(hx, meta, tbl)
        yb = _ffn(xb, tbl, w_gate_up[layer], b_gate_up[layer], w_down[layer], b_down[layer])
        h = _combine(meta, tbl, h1, yb)
    return h.reshape(batch, seq, d)
```

```python
import jax
import jax.numpy as jnp
from jax import lax
from jax.experimental import pallas as pl
from jax.experimental.pallas import tpu as pltpu

HEAD_DIM = 64
SWA_HEADS = 8
SWA_KV_HEADS = 2
SWA_GROUP = SWA_HEADS // SWA_KV_HEADS
BLOCK = 128
SB_HEADS = 8
SWA_WIDTH = SWA_HEADS * HEAD_DIM
SWA_KV_WIDTH = SWA_KV_HEADS * HEAD_DIM
SB_WIDTH = SB_HEADS * HEAD_DIM
N_EXPERTS = 32
TOP_K = 4
SWIGLU_LIMIT = 7.0
SWIGLU_ALPHA = 1.702
NORM_EPS = 1e-5
ATTN_SCALE = HEAD_DIM ** -0.5
SIGN_BIT = -2 ** 31

LANES = 128
SUBLANES = 8
TOKEN_TILE = 512
FFN_BLOCK = 512
CHUNK = 32
SORTED_ROWS = -(-(TOKEN_TILE * TOP_K + N_EXPERTS * (SUBLANES - 1)) // LANES) * LANES
SWA_BLOCKS = 8
SORT_PARTS = 2
SB_Q_ROWS = 1024
SB_K_ROWS = 256
SB_GROUP = 1
SB_STAGGER = 1
VMEM_LIMIT = 56 * 1024 * 1024

F32 = jnp.float32
BF16 = jnp.bfloat16
NEG_BIG = -1e30


def _dot(a, b):
    return jnp.dot(a, b, preferred_element_type=F32)


def _dot_nt(a, b):
    return lax.dot_general(a, b, (((1,), (1,)), ((), ())), preferred_element_type=F32)


def _sigmoid(x):
    return jax.nn.sigmoid(x)


def _rms(x, w):
    return x * lax.rsqrt(jnp.mean(x * x, axis=-1, keepdims=True) + NORM_EPS) * w


def _params(*sem):
    return pltpu.CompilerParams(dimension_semantics=sem, vmem_limit_bytes=VMEM_LIMIT)


def _in_proj_kernel(x_ref, nw_ref, w_ref, qa_ref, kva_ref, qb_ref, kb_ref, vb_ref, ga_ref, gb_ref):
    xn = _rms(x_ref[...], nw_ref[...]).astype(BF16)
    off = 0
    for ref, gate in ((qa_ref, False), (kva_ref, False), (qb_ref, False), (kb_ref, False),
                      (vb_ref, False), (ga_ref, True), (gb_ref, True)):
        width = ref.shape[1]
        y = _dot(xn, w_ref[:, off:off + width])
        if gate:
            y = _sigmoid(y)
        ref[...] = y.astype(ref.dtype)
        off += width


def _in_proj(x2, norm_w, w_in):
    n, d = x2.shape
    widths = (SWA_WIDTH, 2 * SWA_KV_WIDTH, SB_WIDTH, SB_WIDTH, SB_WIDTH, d, d)
    tm = TOKEN_TILE
    return pl.pallas_call(
        _in_proj_kernel,
        grid=(n // tm,),
        in_specs=[pl.BlockSpec((tm, d), lambda i: (i, 0)),
                  pl.BlockSpec((1, d), lambda i: (0, 0)),
                  pl.BlockSpec(w_in.shape, lambda i: (0, 0))],
        out_specs=[pl.BlockSpec((tm, w), lambda i: (i, 0)) for w in widths],
        out_shape=[jax.ShapeDtypeStruct((n, w), BF16) for w in widths],
        compiler_params=_params("parallel"),
        name="in_proj",
    )(x2, norm_w.reshape(1, d), w_in)


def _swa_kernel(q_ref, kvc_ref, kvp_ref, qnw_ref, knw_ref, sink_ref, o_ref):
    i = pl.program_id(1)
    half = HEAD_DIM

    def group_mean_sq(x):
        w = x.shape[1]
        same_head = (lax.broadcasted_iota(jnp.int32, (w, w), 0) // half
                     == lax.broadcasted_iota(jnp.int32, (w, w), 1) // half)
        avg = jnp.where(same_head, 1.0 / half, 0.0).astype(BF16)
        hi, lo = _split_bf16(x * x)
        return _dot(hi, avg) + _dot(lo, avg)

    q = q_ref[...].astype(F32)
    kv = jnp.concatenate([kvp_ref[...], kvc_ref[...]], axis=0)
    k = kv[:, :SWA_KV_WIDTH].astype(F32)
    v = kv[:, SWA_KV_WIDTH:].astype(F32)
    qn = (q * lax.rsqrt(group_mean_sq(q) + NORM_EPS) * qnw_ref[...] * ATTN_SCALE).astype(BF16)
    kn = k * lax.rsqrt(group_mean_sq(k) + NORM_EPS) * knw_ref[...]
    lane = lax.broadcasted_iota(jnp.int32, kn.shape, 1)
    low = lane < half
    k_swap = pltpu.roll(kn, half, 1)
    v_swap = pltpu.roll(v, half, 1).astype(BF16)
    v_same = v.astype(BF16)
    keys = [[jnp.where(low, kn, 0.0).astype(BF16), jnp.where(low, 0.0, k_swap).astype(BF16)],
            [jnp.where(low, k_swap, 0.0).astype(BF16), jnp.where(low, 0.0, kn).astype(BF16)]]
    vals = [[v_same, v_swap], [v_swap, v_same]]

    row = lax.broadcasted_iota(jnp.int32, (BLOCK, 2 * BLOCK), 0)
    col = lax.broadcasted_iota(jnp.int32, (BLOCK, 2 * BLOCK), 1)
    rel = row + BLOCK - col
    in_window = (rel >= 0) & (rel < BLOCK)
    rel = rel.astype(F32)
    out_lane_low = lax.broadcasted_iota(jnp.int32, (BLOCK, LANES), 1) < half
    for sub in range(SWA_BLOCKS):
        rows = slice(sub * BLOCK, (sub + 1) * BLOCK)
        band = slice(sub * BLOCK, (sub + 2) * BLOCK)
        valid = in_window & ((col >= BLOCK) | (i > 0)) if sub == 0 else in_window
        for blk in range(SWA_HEADS // 2):
            qb = qn[rows, blk * LANES:(blk + 1) * LANES]
            halves = []
            for par in range(2):
                h = 2 * blk + par
                g = h // SWA_GROUP
                slope = 2.0 ** (-8.0 * (h + 1) / SWA_HEADS)
                s = jnp.where(valid, _dot_nt(qb, keys[g][par][band]) - slope * rel, NEG_BIG)
                sink = sink_ref[h]
                m = jnp.maximum(jnp.max(s, axis=-1, keepdims=True), sink)
                p = jnp.exp(s - m)
                den = jnp.sum(p, axis=-1, keepdims=True) + jnp.exp(sink - m)
                halves.append(_dot(p.astype(BF16), vals[g][par][band]) / den)
            o_ref[rows, blk * LANES:(blk + 1) * LANES] = jnp.where(
                out_lane_low, halves[0], halves[1]).astype(o_ref.dtype)


def _swa(q_a, kv_a, q_norm_w, k_norm_w, sinks, batch, seq):
    n = q_a.shape[0]
    rows = SWA_BLOCKS * BLOCK
    nb = seq // rows
    return pl.pallas_call(
        _swa_kernel,
        grid=(batch, nb),
        in_specs=[pl.BlockSpec((rows, SWA_WIDTH), lambda b, i: (b * nb + i, 0)),
                  pl.BlockSpec((rows, 2 * SWA_KV_WIDTH), lambda b, i: (b * nb + i, 0)),
                  pl.BlockSpec((BLOCK, 2 * SWA_KV_WIDTH),
                               lambda b, i: (jnp.maximum((b * nb + i) * SWA_BLOCKS - 1, 0), 0)),
                  pl.BlockSpec((1, SWA_WIDTH), lambda b, i: (0, 0)),
                  pl.BlockSpec((1, SWA_KV_WIDTH), lambda b, i: (0, 0)),
                  pl.BlockSpec(memory_space=pltpu.SMEM)],
        out_specs=pl.BlockSpec((rows, SWA_WIDTH), lambda b, i: (b * nb + i, 0)),
        out_shape=jax.ShapeDtypeStruct((n, SWA_WIDTH), BF16),
        compiler_params=_params("parallel", "parallel"),
        name="swa",
    )(q_a, kv_a, kv_a, jnp.tile(q_norm_w, SWA_HEADS).reshape(1, SWA_WIDTH),
      jnp.tile(k_norm_w, SWA_KV_HEADS).reshape(1, SWA_KV_WIDTH), sinks)


def _split_bf16(x):
    hi = x.astype(BF16)
    lo = (x - hi.astype(F32)).astype(BF16)
    return hi, lo


def _sb_kernel(q_ref, k_ref, v_ref, o_ref):
    seq = q_ref.shape[0]
    tq, tk = SB_Q_ROWS, SB_K_ROWS
    nq = seq // tq
    k_per_q = tq // tk
    lane = lax.broadcasted_iota(jnp.int32, (tq, LANES), 1)
    row = lax.broadcasted_iota(jnp.int32, (tq, tk), 0)
    col = lax.broadcasted_iota(jnp.int32, (tq, tk), 1)
    neg_incl = jnp.where(lax.broadcasted_iota(jnp.int32, (tk, tk), 0)
                         >= lax.broadcasted_iota(jnp.int32, (tk, tk), 1), -1.0, 0.0).astype(BF16)

    def softplus(z):
        neg_abs = lax.bitcast_convert_type(lax.bitcast_convert_type(z, jnp.int32) | SIGN_BIT, F32)
        return jnp.maximum(z, 0.0) + jnp.log(1.0 + jnp.exp(neg_abs))

    def neg_suffix_sum(sp):
        return _dot(sp.astype(BF16), neg_incl)

    def add_from_row(x, r0, delta):
        return x + delta if r0 == 0 else jnp.concatenate([x[:r0], x[r0:] + delta], axis=0)

    def k_span(k0s, qms, state, masks, first_rows):
        pairs = [(i, p) for p in range(2) for i in range(len(k0s))]
        kbs = [k_ref[pl.ds(k0, tk), :] for k0 in k0s]
        vbs = [v_ref[pl.ds(k0, tk), :] for k0 in k0s]
        accs, runs = list(state[:2]), list(state[2:])
        zs, sps, cs, ws = {}, {}, {}, {}

        def stage(s, i, p):
            r0 = first_rows[i]
            if s == 0:
                zs[(i, p)] = _dot_nt(qms[p][r0:], kbs[i])
            elif s == 1:
                sp = softplus(zs[(i, p)])
                sps[(i, p)] = sp if masks[i] is None else jnp.where(masks[i], sp, 0.0)
            elif s == 2:
                cs[(i, p)] = neg_suffix_sum(sps[(i, p)])
            elif s == 3:
                w = jnp.exp(zs[(i, p)] + cs[(i, p)] + runs[p][r0:])
                ws[(i, p)] = w if masks[i] is None else jnp.where(masks[i], w, 0.0)
                runs[p] = add_from_row(runs[p], r0, cs[(i, p)][:, 0:1])
            else:
                accs[p] = add_from_row(accs[p], r0, _dot(ws[(i, p)].astype(BF16), vbs[i]))

        n_stages = 5
        for step in range(((len(pairs) - 1) // SB_GROUP) * SB_STAGGER + n_stages):
            for n, (i, p) in enumerate(pairs):
                s = step - (n // SB_GROUP) * SB_STAGGER
                if 0 <= s < n_stages:
                    stage(s, i, p)
        return (*accs, *runs)

    def q_block(qi, carry):
        q0 = pl.multiple_of(qi * tq, tq)
        q2 = q_ref[pl.ds(q0, tq), :] * ATTN_SCALE
        qms = (jnp.where(lane < HEAD_DIM, q2, 0.0).astype(BF16),
               jnp.where(lane >= HEAD_DIM, q2, 0.0).astype(BF16))
        state = (jnp.zeros((tq, LANES), F32), jnp.zeros((tq, LANES), F32),
                 jnp.zeros((tq, 1), F32), jnp.zeros((tq, 1), F32))
        order = list(reversed(range(k_per_q)))
        state = k_span([q0 + d * tk for d in order], qms, state,
                       [((col + d * tk) < row)[d * tk:] for d in order], [d * tk for d in order])

        def earlier(jj, state):
            k0s = [pl.multiple_of((qi - 1 - jj) * tq + d * tk, tk) for d in order]
            return k_span(k0s, qms, state, [None] * k_per_q, [0] * k_per_q)

        state = lax.fori_loop(0, qi, earlier, state)
        o_ref[pl.ds(q0, tq), :] = jnp.where(lane < HEAD_DIM, state[0], state[1]).astype(o_ref.dtype)
        return carry

    lax.fori_loop(0, nq, q_block, 0)


def _sb(q_b, k_b, v_b, batch, seq):
    n = q_b.shape[0]
    spec = pl.BlockSpec((seq, LANES), lambda b, p: (b, p))
    return pl.pallas_call(
        _sb_kernel,
        grid=(batch, SB_WIDTH // LANES),
        in_specs=[spec, spec, spec],
        out_specs=spec,
        out_shape=jax.ShapeDtypeStruct((n, SB_WIDTH), BF16),
        compiler_params=_params("parallel", "parallel"),
        name="stick_breaking",
    )(q_b, k_b, v_b)


def _merge_kernel(ya_ref, yb_ref, ga_ref, gb_ref, x_ref, pa_ref, pb_ref, wo_ref, h_ref):
    m = (ga_ref[...].astype(F32) * _dot(ya_ref[...], pa_ref[...])
         + gb_ref[...].astype(F32) * _dot(yb_ref[...], pb_ref[...]))
    h_ref[...] = x_ref[...] + _dot(m.astype(BF16), wo_ref[...])


def _merge(y_a, y_b, g_a, g_b, x2, p_a, p_b, w_o):
    n, d = x2.shape
    tm = TOKEN_TILE
    row = lambda w: pl.BlockSpec((tm, w), lambda i: (i, 0))
    full = lambda a: pl.BlockSpec(a.shape, lambda i: (0, 0))
    return pl.pallas_call(
        _merge_kernel,
        grid=(n // tm,),
        in_specs=[row(SWA_WIDTH), row(SB_WIDTH), row(d), row(d), row(d), full(p_a), full(p_b), full(w_o)],
        out_specs=row(d),
        out_shape=jax.ShapeDtypeStruct((n, d), F32),
        compiler_params=_params("parallel"),
        name="merge",
    )(y_a, y_b, g_a, g_b, x2, p_a, p_b, w_o)


META_EXPERT = 0
META_RANK = TOP_K
EXT_EXPERT = 0
EXT_GATE = TOP_K
GATE_PARTS = 3


def _router_kernel(h_ref, nw_ref, rw_ref, rb_ref, hx_ref, meta_ref, trun_ref, cnt_ref, run_ref):
    tm, d = h_ref.shape

    @pl.when(pl.program_id(0) == 0)
    def _():
        run_ref[...] = jnp.zeros_like(run_ref)

    hn = _rms(h_ref[...], nw_ref[...])
    hn_hi, hn_lo = _split_bf16(hn)
    both = _dot(hn_hi, rw_ref[...])
    logits = (both[:, :N_EXPERTS] + both[:, N_EXPERTS:] + _dot(hn_lo, rw_ref[:, :N_EXPERTS])
              + rb_ref[...])
    lane = lax.broadcasted_iota(jnp.int32, (tm, N_EXPERTS), 1).astype(F32)
    work = logits
    tops, idxs, hots = [], [], []
    for _ in range(TOP_K):
        m = jnp.max(work, axis=-1, keepdims=True)
        idx = jnp.min(jnp.where(work == m, lane, float(N_EXPERTS)), axis=-1, keepdims=True)
        hot = lane == idx
        work = jnp.where(hot, -jnp.inf, work)
        tops.append(m)
        idxs.append(idx)
        hots.append(hot)
    exps = [jnp.exp(t - tops[0]) for t in tops]
    den = exps[0] + exps[1] + exps[2] + exps[3]
    chosen = jnp.where(hots[0] | hots[1] | hots[2] | hots[3], 1.0, 0.0)
    r = lax.broadcasted_iota(jnp.int32, (tm, tm), 0)
    c = lax.broadcasted_iota(jnp.int32, (tm, tm), 1)
    before = jnp.where(c < r, 1.0, 0.0).astype(BF16)
    run = run_ref[...]
    prefix = _dot(before, chosen.astype(BF16)) + run
    mlane = lax.broadcasted_iota(jnp.int32, (tm, LANES), 1)
    meta = jnp.zeros((tm, LANES), F32)
    ext = jnp.zeros((tm, LANES), F32)
    for k in range(TOP_K):
        rank = jnp.sum(jnp.where(hots[k], prefix, 0.0), axis=-1, keepdims=True)
        meta = jnp.where(mlane == META_EXPERT + k, idxs[k], meta)
        meta = jnp.where(mlane == META_RANK + k, rank, meta)
        ext = jnp.where(mlane == EXT_EXPERT + k, idxs[k], ext)
        rest = exps[k] / den
        for part in range(GATE_PARTS):
            piece = rest.astype(BF16).astype(F32)
            ext = jnp.where(mlane == EXT_GATE + GATE_PARTS * k + part, piece, ext)
            rest = rest - piece
    meta_ref[...] = meta
    hx_ref[:, :d] = hn_hi
    hx_ref[:, d:] = ext.astype(BF16)
    trun_ref[0] = run
    tile_cnt = jnp.sum(chosen, axis=0, keepdims=True)
    run = run + jnp.ceil(tile_cnt / SUBLANES) * SUBLANES
    run_ref[...] = run
    cnt_ref[...] = run


def _router(h1, norm_w, router_w, router_b):
    n, d = h1.shape
    tm = TOKEN_TILE
    tiles = n // tm
    return pl.pallas_call(
        _router_kernel,
        grid=(tiles,),
        in_specs=[pl.BlockSpec((tm, d), lambda i: (i, 0)),
                  pl.BlockSpec((1, d), lambda i: (0, 0)),
                  pl.BlockSpec((d, 2 * N_EXPERTS), lambda i: (0, 0)),
                  pl.BlockSpec((1, N_EXPERTS), lambda i: (0, 0))],
        out_specs=[pl.BlockSpec((tm, d + LANES), lambda i: (i, 0)),
                   pl.BlockSpec((tm, LANES), lambda i: (i, 0)),
                   pl.BlockSpec((1, 1, N_EXPERTS), lambda i: (i, 0, 0)),
                   pl.BlockSpec((1, N_EXPERTS), lambda i: (0, 0))],
        out_shape=[jax.ShapeDtypeStruct((n, d + LANES), BF16),
                   jax.ShapeDtypeStruct((n, LANES), F32),
                   jax.ShapeDtypeStruct((tiles, 1, N_EXPERTS), F32),
                   jax.ShapeDtypeStruct((1, N_EXPERTS), F32)],
        scratch_shapes=[pltpu.VMEM((1, N_EXPERTS), F32)],
        compiler_params=_params("arbitrary"),
        name="router",
    )(h1, norm_w.reshape(1, d), jnp.concatenate(_split_bf16(router_w), axis=1),
      router_b.reshape(1, N_EXPERTS))


def _routing_tables(tile_run, counts, n):
    tiles = tile_run.shape[0]
    counts = counts.reshape(N_EXPERTS).astype(jnp.int32)
    run = tile_run.reshape(tiles, N_EXPERTS).astype(jnp.int32)
    tile_len = jnp.concatenate([run[1:], counts[None]], axis=0) - run
    loc = jnp.cumsum(tile_len, axis=1) - tile_len
    padded = (counts + FFN_BLOCK - 1) // FFN_BLOCK * FFN_BLOCK
    seg_end = jnp.cumsum(padded)
    seg_start = seg_end - padded
    dst0 = seg_start[None, :] + run
    delta = (loc - run).astype(F32).reshape(tiles, 1, N_EXPERTS)
    zero0 = seg_start + counts
    nzero = (padded - counts) // SUBLANES
    n_real = seg_end[-1] // FFN_BLOCK
    max_blocks = (n * TOP_K + tiles * N_EXPERTS * (SUBLANES - 1)
                  + N_EXPERTS * (FFN_BLOCK - 1)) // FFN_BLOCK
    blk_exp = jnp.minimum(jnp.sum(jnp.arange(max_blocks, dtype=jnp.int32)[:, None] * FFN_BLOCK
                                  >= seg_end[None, :], axis=1), N_EXPERTS - 1).astype(jnp.int32)
    ncopy = [jnp.sum(tile_len // CHUNK, axis=1)]
    ncopy += [jnp.sum((tile_len // rows) % 2, axis=1) for rows in _copy_sizes()[1:]]
    ncopy = jnp.stack(ncopy, axis=1).astype(jnp.int32)
    return dict(dst0=dst0.reshape(-1), tile_len=tile_len.reshape(-1), loc=loc.reshape(-1),
                ncopy=ncopy.reshape(-1), delta=delta, zero0=zero0, nzero=nzero, blk_exp=blk_exp,
                n_real=n_real.reshape(1).astype(jnp.int32), cap=max_blocks * FFN_BLOCK)


def _tile_rows(meta_ref, delta_ref):
    meta = meta_ref[...]
    tm = meta.shape[0]
    delta = delta_ref[0]
    lane = lax.broadcasted_iota(jnp.int32, (tm, N_EXPERTS), 1).astype(F32)
    rows = []
    for k in range(TOP_K):
        expert = meta[:, META_EXPERT + k:META_EXPERT + k + 1]
        rank = meta[:, META_RANK + k:META_RANK + k + 1]
        rows.append(rank + jnp.sum(jnp.where(lane == expert, delta, 0.0), axis=-1, keepdims=True))
    return rows


def _copy_sizes():
    sizes, rows = [CHUNK], CHUNK // 2
    while rows >= SUBLANES:
        sizes.append(rows)
        rows //= 2
    return sizes


def _wait_copies(t, ncopy_ref, make_copy):
    sizes = _copy_sizes()
    for c, rows in enumerate(sizes):
        def body(_, carry, rows=rows):
            make_copy(0, 0, rows).wait()
            return carry

        lax.fori_loop(0, ncopy_ref[t * len(sizes) + c], body, 0)


def _start_copies(t, tbl_ref, len_ref, loc_ref, make_copy):
    def one(seg_row, loc_row, rows):
        make_copy(pl.multiple_of(seg_row, SUBLANES), pl.multiple_of(loc_row, SUBLANES), rows).start()

    for e in range(N_EXPERTS):
        idx = t * N_EXPERTS + e
        seg_row = tbl_ref[idx]
        loc_row = loc_ref[idx]
        length = len_ref[idx]
        full = length // CHUNK

        def body(c, carry):
            one(seg_row + c * CHUNK, loc_row + c * CHUNK, CHUNK)
            return carry

        lax.fori_loop(0, full, body, 0)
        done = full * CHUNK
        rows = CHUNK // 2
        while rows >= SUBLANES:
            @pl.when((length & rows) != 0)
            def _(done=done, rows=rows):
                one(seg_row + done, loc_row + done, rows)
            done = done + (length & rows)
            rows //= 2


def _dispatch_kernel(dst0_ref, len_ref, loc_ref, ncopy_ref, zero0_ref, nzero_ref, nreal_ref,
                     hx_ref, meta_ref, delta_ref, xb_ref, srt_ref, zero_ref, sem):
    t = pl.program_id(0)
    tm = hx_ref.shape[0]
    rows = _tile_rows(meta_ref, delta_ref)
    mlane = lax.broadcasted_iota(jnp.int32, (tm, LANES), 1)
    packed = jnp.full((tm, LANES), -1.0, F32)
    for k in range(TOP_K):
        packed = jnp.where(mlane == k, rows[k], packed)
    rows_t = packed.T
    slot = t % 2
    part = SORTED_ROWS // SORT_PARTS
    for c in range(SORT_PARTS):
        prow = (lax.broadcasted_iota(jnp.int32, (part, tm), 0) + c * part).astype(F32)
        onehot = jnp.zeros((part, tm), F32)
        for k in range(TOP_K):
            onehot = jnp.where(prow == rows_t[k:k + 1, :], 1.0, onehot)
        srt_ref[slot, c * part:(c + 1) * part] = _dot(onehot.astype(BF16), hx_ref[...])

    def copy_from(s):
        def copy(seg_row, loc_row, rows):
            return pltpu.make_async_copy(srt_ref.at[s, pl.ds(loc_row, rows)],
                                         xb_ref.at[pl.ds(seg_row, rows)], sem.at[s])
        return copy

    @pl.when(t > 0)
    def _():
        _wait_copies(t - 1, ncopy_ref, copy_from(1 - slot))

    _start_copies(t, dst0_ref, len_ref, loc_ref, copy_from(slot))

    @pl.when(t == pl.num_programs(0) - 1)
    def _():
        _wait_copies(t, ncopy_ref, copy_from(slot))
        zero_ref[...] = jnp.zeros_like(zero_ref)
        for op in ("start", "wait"):
            for e in range(N_EXPERTS):
                def body(c, carry):
                    row0 = pl.multiple_of(zero0_ref[e] + c * SUBLANES, SUBLANES)
                    cp = pltpu.make_async_copy(zero_ref.at[pl.ds(0, SUBLANES)],
                                               xb_ref.at[pl.ds(row0, SUBLANES)], sem.at[0])
                    cp.start() if op == "start" else cp.wait()
                    return carry

                lax.fori_loop(0, nzero_ref[e], body, 0)

            def tail(b, carry):
                row0 = pl.multiple_of(b * FFN_BLOCK, FFN_BLOCK)
                cp = pltpu.make_async_copy(zero_ref, xb_ref.at[pl.ds(row0, FFN_BLOCK)], sem.at[0])
                cp.start() if op == "start" else cp.wait()
                return carry

            lax.fori_loop(nreal_ref[0], xb_ref.shape[0] // FFN_BLOCK, tail, 0)


def _dispatch(hx, meta, tbl):
    n, width = hx.shape
    tm = TOKEN_TILE
    grid_spec = pltpu.PrefetchScalarGridSpec(
        num_scalar_prefetch=7,
        grid=(n // tm,),
        in_specs=[pl.BlockSpec((tm, width), lambda i, *_: (i, 0)),
                  pl.BlockSpec((tm, LANES), lambda i, *_: (i, 0)),
                  pl.BlockSpec((1, 1, N_EXPERTS), lambda i, *_: (i, 0, 0))],
        out_specs=pl.BlockSpec(memory_space=pl.ANY),
        scratch_shapes=[pltpu.VMEM((2, SORTED_ROWS, width), F32),
                        pltpu.VMEM((FFN_BLOCK, width), F32),
                        pltpu.SemaphoreType.DMA((2,))],
    )
    return pl.pallas_call(
        _dispatch_kernel,
        grid_spec=grid_spec,
        out_shape=jax.ShapeDtypeStruct((tbl["cap"], width), F32),
        compiler_params=_params("arbitrary"),
        name="dispatch",
    )(tbl["dst0"], tbl["tile_len"], tbl["loc"], tbl["ncopy"], tbl["zero0"], tbl["nzero"], tbl["n_real"],
      hx, meta, tbl["delta"])


def _ffn_kernel(exp_ref, nreal_ref, x_ref, wgu_ref, bgu_ref, wd_ref, bd_ref, y_ref, wgu_bf, wd_bf):
    b = pl.program_id(0)
    d_ff, d = wd_ref.shape[1], wd_ref.shape[2]

    @pl.when(b >= nreal_ref[0])
    def _():
        y_ref[...] = jnp.zeros_like(y_ref)

    @pl.when((b == 0) | (exp_ref[b] != exp_ref[jnp.maximum(b - 1, 0)]))
    def _():
        wgu_bf[...] = wgu_ref[0].astype(BF16)
        wd_bf[...] = wd_ref[0].astype(BF16)

    @pl.when(b < nreal_ref[0])
    def _():
        x = x_ref[...]
        ext = x[:, d:]
        expert = exp_ref[b].astype(F32)
        gate = jnp.zeros((x.shape[0], 1), F32)
        for k in range(TOP_K):
            g0 = EXT_GATE + GATE_PARTS * k
            g = ext[:, g0:g0 + 1] + ext[:, g0 + 1:g0 + 2] + ext[:, g0 + 2:g0 + 3]
            gate = gate + jnp.where(ext[:, EXT_EXPERT + k:EXT_EXPERT + k + 1] == expert, g, 0.0)
        gu = _dot(x[:, :d].astype(BF16), wgu_bf[...]) + bgu_ref[0]
        g = jnp.minimum(gu[:, :d_ff], SWIGLU_LIMIT)
        u = jnp.clip(gu[:, d_ff:], -SWIGLU_LIMIT, SWIGLU_LIMIT)
        act = (u + 1.0) * (g * jax.nn.sigmoid(SWIGLU_ALPHA * g))
        y_ref[...] = gate * (_dot(act.astype(BF16), wd_bf[...]) + bd_ref[0])


def _ffn(xb, tbl, w_gate_up, b_gate_up, w_down, b_down):
    cap, width = xb.shape
    e, d, two_ff = w_gate_up.shape
    d_ff = two_ff // 2
    grid_spec = pltpu.PrefetchScalarGridSpec(
        num_scalar_prefetch=2,
        grid=(cap // FFN_BLOCK,),
        in_specs=[pl.BlockSpec((FFN_BLOCK, width), lambda b, ex, nr: (b, 0)),
                  pl.BlockSpec((1, d, two_ff), lambda b, ex, nr: (ex[b], 0, 0)),
                  pl.BlockSpec((1, 1, two_ff), lambda b, ex, nr: (ex[b], 0, 0)),
                  pl.BlockSpec((1, d_ff, d), lambda b, ex, nr: (ex[b], 0, 0)),
                  pl.BlockSpec((1, 1, d), lambda b, ex, nr: (ex[b], 0, 0))],
        out_specs=pl.BlockSpec((FFN_BLOCK, d), lambda b, ex, nr: (b, 0)),
        scratch_shapes=[pltpu.VMEM((d, two_ff), BF16), pltpu.VMEM((d_ff, d), BF16)],
    )
    return pl.pallas_call(
        _ffn_kernel,
        grid_spec=grid_spec,
        out_shape=jax.ShapeDtypeStruct((cap, d), F32),
        compiler_params=_params("arbitrary"),
        name="expert_ffn",
    )(tbl["blk_exp"], tbl["n_real"], xb, w_gate_up,
      b_gate_up.reshape(e, 1, two_ff), w_down, b_down.reshape(e, 1, d))


def _combine_kernel(dst0_ref, len_ref, loc_ref, ncopy_ref, meta_ref, delta_ref, h_ref, y_ref, o_ref,
                    srt_ref, sem):
    t = pl.program_id(0)
    tm = h_ref.shape[0]

    @pl.when(t == 0)
    def _():
        srt_ref[...] = jnp.zeros_like(srt_ref)

    slot = t % 2

    def copy_into(s):
        def copy(seg_row, loc_row, rows):
            return pltpu.make_async_copy(y_ref.at[pl.ds(seg_row, rows)],
                                         srt_ref.at[s, pl.ds(loc_row, rows)], sem.at[s])
        return copy

    @pl.when(t == 0)
    def _():
        _start_copies(t, dst0_ref, len_ref, loc_ref, copy_into(slot))

    @pl.when(t + 1 < pl.num_programs(0))
    def _():
        _start_copies(t + 1, dst0_ref, len_ref, loc_ref, copy_into(1 - slot))

    rows = _tile_rows(meta_ref, delta_ref)
    part = SORTED_ROWS // SORT_PARTS

    def onehot_part(c):
        pcol = (lax.broadcasted_iota(jnp.int32, (tm, part), 1) + c * part).astype(F32)
        onehot = jnp.zeros((tm, part), F32)
        for k in range(TOP_K):
            onehot = jnp.where(pcol == rows[k], 1.0, onehot)
        return onehot.astype(BF16)

    first = onehot_part(0)
    _wait_copies(t, ncopy_ref, copy_into(slot))
    out = h_ref[...]
    for c in range(SORT_PARTS):
        onehot = first if c == 0 else onehot_part(c)
        out = out + _dot(onehot, srt_ref[slot, c * part:(c + 1) * part].astype(BF16))
    o_ref[...] = out


def _combine(meta, tbl, h1, yb):
    n, d = h1.shape
    tm = TOKEN_TILE
    grid_spec = pltpu.PrefetchScalarGridSpec(
        num_scalar_prefetch=4,
        grid=(n // tm,),
        in_specs=[pl.BlockSpec((tm, LANES), lambda i, *_: (i, 0)),
                  pl.BlockSpec((1, 1, N_EXPERTS), lambda i, *_: (i, 0, 0)),
                  pl.BlockSpec((tm, d), lambda i, *_: (i, 0)),
                  pl.BlockSpec(memory_space=pl.ANY)],
        out_specs=pl.BlockSpec((tm, d), lambda i, *_: (i, 0)),
        scratch_shapes=[pltpu.VMEM((2, SORTED_ROWS, d), F32), pltpu.SemaphoreType.DMA((2,))],
    )
    return pl.pallas_call(
        _combine_kernel,
        grid_spec=grid_spec,
        out_shape=jax.ShapeDtypeStruct((n, d), F32),
        compiler_params=_params("arbitrary"),
        name="combine",
    )(tbl["dst0"], tbl["tile_len"], tbl["loc"], tbl["ncopy"], meta, tbl["delta"], h1, yb)


def kernel(x, attn_norm_w, w_in, q_norm_w, k_norm_w, sinks, w_proj_swa, w_proj_sb, w_out,
           ffn_norm_w, router_w, router_b, w_gate_up, b_gate_up, w_down, b_down):
    batch, seq, d = x.shape
    n = batch * seq
    h = x.reshape(n, d)
    for layer in range(attn_norm_w.shape[0]):
        q_a, kv_a, q_b, k_b, v_b, g_a, g_b = _in_proj(h, attn_norm_w[layer], w_in[layer].astype(BF16))
        y_a = _swa(q_a, kv_a, q_norm_w[layer], k_norm_w[layer], sinks[layer], batch, seq)
        y_b = _sb(q_b, k_b, v_b, batch, seq)
        h1 = _merge(y_a, y_b, g_a, g_b, h, w_proj_swa[layer].astype(BF16),
                    w_proj_sb[layer].astype(BF16), w_out[layer].astype(BF16))
        hx, meta, tile_run, counts = _router(h1, ffn_norm_w[layer], router_w[layer], router_b[layer])
        tbl = _routing_tables(tile_run, counts, n)
        xb = _dispatch(hx, meta, tbl)
        yb = _ffn(xb, tbl, w_gate_up[layer], b_gate_up[layer], w_down[layer], b_down[layer])
        h = _combine(meta, tbl, h1, yb)
    return h.reshape(batch, seq, d)
```

```python
import jax
import jax.numpy as jnp
from jax import lax
from jax.experimental import pallas as pl
from jax.experimental.pallas import tpu as pltpu

HEAD_DIM = 64
SWA_HEADS = 8
SWA_KV_HEADS = 2
SWA_GROUP = SWA_HEADS // SWA_KV_HEADS
BLOCK = 128
SB_HEADS = 8
SWA_WIDTH = SWA_HEADS * HEAD_DIM
SWA_KV_WIDTH = SWA_KV_HEADS * HEAD_DIM
SB_WIDTH = SB_HEADS * HEAD_DIM
N_EXPERTS = 32
TOP_K = 4
SWIGLU_LIMIT = 7.0
SWIGLU_ALPHA = 1.702
NORM_EPS = 1e-5
ATTN_SCALE = HEAD_DIM ** -0.5
SIGN_BIT = -2 ** 31

LANES = 128
SUBLANES = 8
DENSE_TILE = 1024
TOKEN_TILE = 512
FFN_BLOCK = 512
CHUNK = 32
SORTED_ROWS = -(-(TOKEN_TILE * TOP_K + N_EXPERTS * (SUBLANES - 1)) // LANES) * LANES
SWA_BLOCKS = 8
SORT_PARTS = 2
SB_Q_ROWS = 1024
SB_K_ROWS = 256
SB_GROUP = 1
SB_STAGGER = 1
VMEM_LIMIT = 56 * 1024 * 1024

F32 = jnp.float32
BF16 = jnp.bfloat16
NEG_BIG = -1e30


def _dot(a, b):
    return jnp.dot(a, b, preferred_element_type=F32)


def _dot_nt(a, b):
    return lax.dot_general(a, b, (((1,), (1,)), ((), ())), preferred_element_type=F32)


def _sigmoid(x):
    return jax.nn.sigmoid(x)


def _rms(x, w):
    return x * lax.rsqrt(jnp.mean(x * x, axis=-1, keepdims=True) + NORM_EPS) * w


def _params(*sem):
    return pltpu.CompilerParams(dimension_semantics=sem, vmem_limit_bytes=VMEM_LIMIT)


def _in_proj_kernel(x_ref, nw_ref, w_ref, qa_ref, kva_ref, qb_ref, kb_ref, vb_ref, ga_ref, gb_ref):
    xn = _rms(x_ref[...], nw_ref[...]).astype(BF16)
    off = 0
    for ref, gate in ((qa_ref, False), (kva_ref, False), (qb_ref, False), (kb_ref, False),
                      (vb_ref, False), (ga_ref, True), (gb_ref, True)):
        width = ref.shape[1]
        y = _dot(xn, w_ref[:, off:off + width])
        if gate:
            y = _sigmoid(y)
        ref[...] = y.astype(ref.dtype)
        off += width


def _in_proj(x2, norm_w, w_in):
    n, d = x2.shape
    widths = (SWA_WIDTH, 2 * SWA_KV_WIDTH, SB_WIDTH, SB_WIDTH, SB_WIDTH, d, d)
    tm = DENSE_TILE
    return pl.pallas_call(
        _in_proj_kernel,
        grid=(n // tm,),
        in_specs=[pl.BlockSpec((tm, d), lambda i: (i, 0)),
                  pl.BlockSpec((1, d), lambda i: (0, 0)),
                  pl.BlockSpec(w_in.shape, lambda i: (0, 0))],
        out_specs=[pl.BlockSpec((tm, w), lambda i: (i, 0)) for w in widths],
        out_shape=[jax.ShapeDtypeStruct((n, w), BF16) for w in widths],
        compiler_params=_params("parallel"),
        name="in_proj",
    )(x2, norm_w.reshape(1, d), w_in)


def _swa_kernel(q_ref, kvc_ref, kvp_ref, qnw_ref, knw_ref, sink_ref, o_ref):
    i = pl.program_id(1)
    half = HEAD_DIM

    def group_mean_sq(x):
        w = x.shape[1]
        same_head = (lax.broadcasted_iota(jnp.int32, (w, w), 0) // half
                     == lax.broadcasted_iota(jnp.int32, (w, w), 1) // half)
        avg = jnp.where(same_head, 1.0 / half, 0.0).astype(BF16)
        hi, lo = _split_bf16(x * x)
        return _dot(hi, avg) + _dot(lo, avg)

    q = q_ref[...].astype(F32)
    kv = jnp.concatenate([kvp_ref[...], kvc_ref[...]], axis=0)
    k = kv[:, :SWA_KV_WIDTH].astype(F32)
    v = kv[:, SWA_KV_WIDTH:].astype(F32)
    qn = (q * lax.rsqrt(group_mean_sq(q) + NORM_EPS) * qnw_ref[...] * ATTN_SCALE).astype(BF16)
    kn = k * lax.rsqrt(group_mean_sq(k) + NORM_EPS) * knw_ref[...]
    lane = lax.broadcasted_iota(jnp.int32, kn.shape, 1)
    low = lane < half
    k_swap = pltpu.roll(kn, half, 1)
    v_swap = pltpu.roll(v, half, 1).astype(BF16)
    v_same = v.astype(BF16)
    keys = [[jnp.where(low, kn, 0.0).astype(BF16), jnp.where(low, 0.0, k_swap).astype(BF16)],
            [jnp.where(low, k_swap, 0.0).astype(BF16), jnp.where(low, 0.0, kn).astype(BF16)]]
    vals = [[v_same, v_swap], [v_swap, v_same]]

    row = lax.broadcasted_iota(jnp.int32, (BLOCK, 2 * BLOCK), 0)
    col = lax.broadcasted_iota(jnp.int32, (BLOCK, 2 * BLOCK), 1)
    rel = row + BLOCK - col
    in_window = (rel >= 0) & (rel < BLOCK)
    rel = rel.astype(F32)
    out_lane_low = lax.broadcasted_iota(jnp.int32, (BLOCK, LANES), 1) < half
    for sub in range(SWA_BLOCKS):
        rows = slice(sub * BLOCK, (sub + 1) * BLOCK)
        band = slice(sub * BLOCK, (sub + 2) * BLOCK)
        valid = in_window & ((col >= BLOCK) | (i > 0)) if sub == 0 else in_window
        for blk in range(SWA_HEADS // 2):
            qb = qn[rows, blk * LANES:(blk + 1) * LANES]
            halves = []
            for par in range(2):
                h = 2 * blk + par
                g = h // SWA_GROUP
                slope = 2.0 ** (-8.0 * (h + 1) / SWA_HEADS)
                s = jnp.where(valid, _dot_nt(qb, keys[g][par][band]) - slope * rel, NEG_BIG)
                sink = sink_ref[h]
                m = jnp.maximum(jnp.max(s, axis=-1, keepdims=True), sink)
                p = jnp.exp(s - m)
                den = jnp.sum(p, axis=-1, keepdims=True) + jnp.exp(sink - m)
                halves.append(_dot(p.astype(BF16), vals[g][par][band]) / den)
            o_ref[rows, blk * LANES:(blk + 1) * LANES] = jnp.where(
                out_lane_low, halves[0], halves[1]).astype(o_ref.dtype)


def _swa(q_a, kv_a, q_norm_w, k_norm_w, sinks, batch, seq):
    n = q_a.shape[0]
    rows = SWA_BLOCKS * BLOCK
    nb = seq // rows
    return pl.pallas_call(
        _swa_kernel,
        grid=(batch, nb),
        in_specs=[pl.BlockSpec((rows, SWA_WIDTH), lambda b, i: (b * nb + i, 0)),
                  pl.BlockSpec((rows, 2 * SWA_KV_WIDTH), lambda b, i: (b * nb + i, 0)),
                  pl.BlockSpec((BLOCK, 2 * SWA_KV_WIDTH),
                               lambda b, i: (jnp.maximum((b * nb + i) * SWA_BLOCKS - 1, 0), 0)),
                  pl.BlockSpec((1, SWA_WIDTH), lambda b, i: (0, 0)),
                  pl.BlockSpec((1, SWA_KV_WIDTH), lambda b, i: (0, 0)),
                  pl.BlockSpec(memory_space=pltpu.SMEM)],
        out_specs=pl.BlockSpec((rows, SWA_WIDTH), lambda b, i: (b * nb + i, 0)),
        out_shape=jax.ShapeDtypeStruct((n, SWA_WIDTH), BF16),
        compiler_params=_params("parallel", "parallel"),
        name="swa",
    )(q_a, kv_a, kv_a, jnp.tile(q_norm_w, SWA_HEADS).reshape(1, SWA_WIDTH),
      jnp.tile(k_norm_w, SWA_KV_HEADS).reshape(1, SWA_KV_WIDTH), sinks)


def _split_bf16(x):
    hi = x.astype(BF16)
    lo = (x - hi.astype(F32)).astype(BF16)
    return hi, lo


def _sb_kernel(q_ref, k_ref, v_ref, o_ref):
    seq = q_ref.shape[0]
    tq, tk = SB_Q_ROWS, SB_K_ROWS
    nq = seq // tq
    k_per_q = tq // tk
    lane = lax.broadcasted_iota(jnp.int32, (tq, LANES), 1)
    row = lax.broadcasted_iota(jnp.int32, (tq, tk), 0)
    col = lax.broadcasted_iota(jnp.int32, (tq, tk), 1)
    neg_incl = jnp.where(lax.broadcasted_iota(jnp.int32, (tk, tk), 0)
                         >= lax.broadcasted_iota(jnp.int32, (tk, tk), 1), -1.0, 0.0).astype(BF16)

    def softplus(z):
        neg_abs = lax.bitcast_convert_type(lax.bitcast_convert_type(z, jnp.int32) | SIGN_BIT, F32)
        return jnp.maximum(z, 0.0) + jnp.log(1.0 + jnp.exp(neg_abs))

    def neg_suffix_sum(sp):
        return _dot(sp.astype(BF16), neg_incl)

    def add_from_row(x, r0, delta):
        return x + delta if r0 == 0 else jnp.concatenate([x[:r0], x[r0:] + delta], axis=0)

    def k_span(k0s, qms, state, masks, first_rows):
        pairs = [(i, p) for p in range(2) for i in range(len(k0s))]
        kbs = [k_ref[pl.ds(k0, tk), :] for k0 in k0s]
        vbs = [v_ref[pl.ds(k0, tk), :] for k0 in k0s]
        accs, runs = list(state[:2]), list(state[2:])
        zs, sps, cs, ws = {}, {}, {}, {}

        def stage(s, i, p):
            r0 = first_rows[i]
            if s == 0:
                zs[(i, p)] = _dot_nt(qms[p][r0:], kbs[i])
            elif s == 1:
                sp = softplus(zs[(i, p)])
                sps[(i, p)] = sp if masks[i] is None else jnp.where(masks[i], sp, 0.0)
            elif s == 2:
                cs[(i, p)] = neg_suffix_sum(sps[(i, p)])
            elif s == 3:
                w = jnp.exp(zs[(i, p)] + cs[(i, p)] + runs[p][r0:])
                ws[(i, p)] = w if masks[i] is None else jnp.where(masks[i], w, 0.0)
                runs[p] = add_from_row(runs[p], r0, cs[(i, p)][:, 0:1])
            else:
                accs[p] = add_from_row(accs[p], r0, _dot(ws[(i, p)].astype(BF16), vbs[i]))

        n_stages = 5
        for step in range(((len(pairs) - 1) // SB_GROUP) * SB_STAGGER + n_stages):
            for n, (i, p) in enumerate(pairs):
                s = step - (n // SB_GROUP) * SB_STAGGER
                if 0 <= s < n_stages:
                    stage(s, i, p)
        return (*accs, *runs)

    def q_block(qi, carry):
        q0 = pl.multiple_of(qi * tq, tq)
        q2 = q_ref[pl.ds(q0, tq), :] * ATTN_SCALE
        qms = (jnp.where(lane < HEAD_DIM, q2, 0.0).astype(BF16),
               jnp.where(lane >= HEAD_DIM, q2, 0.0).astype(BF16))
        state = (jnp.zeros((tq, LANES), F32), jnp.zeros((tq, LANES), F32),
                 jnp.zeros((tq, 1), F32), jnp.zeros((tq, 1), F32))
        order = list(reversed(range(k_per_q)))
        state = k_span([q0 + d * tk for d in order], qms, state,
                       [((col + d * tk) < row)[d * tk:] for d in order], [d * tk for d in order])

        def earlier(jj, state):
            k0s = [pl.multiple_of((qi - 1 - jj) * tq + d * tk, tk) for d in order]
            return k_span(k0s, qms, state, [None] * k_per_q, [0] * k_per_q)

        state = lax.fori_loop(0, qi, earlier, state)
        o_ref[pl.ds(q0, tq), :] = jnp.where(lane < HEAD_DIM, state[0], state[1]).astype(o_ref.dtype)
        return carry

    lax.fori_loop(0, nq, q_block, 0)


def _sb(q_b, k_b, v_b, batch, seq):
    n = q_b.shape[0]
    spec = pl.BlockSpec((seq, LANES), lambda b, p: (b, p))
    return pl.pallas_call(
        _sb_kernel,
        grid=(batch, SB_WIDTH // LANES),
        in_specs=[spec, spec, spec],
        out_specs=spec,
        out_shape=jax.ShapeDtypeStruct((n, SB_WIDTH), BF16),
        compiler_params=_params("parallel", "parallel"),
        name="stick_breaking",
    )(q_b, k_b, v_b)


def _merge_kernel(ya_ref, yb_ref, ga_ref, gb_ref, x_ref, pa_ref, pb_ref, wo_ref, h_ref):
    m = (ga_ref[...].astype(F32) * _dot(ya_ref[...], pa_ref[...])
         + gb_ref[...].astype(F32) * _dot(yb_ref[...], pb_ref[...]))
    h_ref[...] = x_ref[...] + _dot(m.astype(BF16), wo_ref[...])


def _merge(y_a, y_b, g_a, g_b, x2, p_a, p_b, w_o):
    n, d = x2.shape
    tm = DENSE_TILE
    row = lambda w: pl.BlockSpec((tm, w), lambda i: (i, 0))
    full = lambda a: pl.BlockSpec(a.shape, lambda i: (0, 0))
    return pl.pallas_call(
        _merge_kernel,
        grid=(n // tm,),
        in_specs=[row(SWA_WIDTH), row(SB_WIDTH), row(d), row(d), row(d), full(p_a), full(p_b), full(w_o)],
        out_specs=row(d),
        out_shape=jax.ShapeDtypeStruct((n, d), F32),
        compiler_params=_params("parallel"),
        name="merge",
    )(y_a, y_b, g_a, g_b, x2, p_a, p_b, w_o)


EXT_EXPERT = 0
EXT_GATE = TOP_K
GATE_PARTS = 3


def _router_kernel(h_ref, nw_ref, rw_ref, rb_ref, hx_ref, meta_ref, meta_t_ref, trun_ref, cnt_ref, run_ref):
    tm, d = h_ref.shape

    @pl.when(pl.program_id(0) == 0)
    def _():
        run_ref[...] = jnp.zeros_like(run_ref)

    hn = _rms(h_ref[...], nw_ref[...])
    hn_hi, hn_lo = _split_bf16(hn)
    both = _dot(hn_hi, rw_ref[...])
    logits = (both[:, :N_EXPERTS] + both[:, N_EXPERTS:] + _dot(hn_lo, rw_ref[:, :N_EXPERTS])
              + rb_ref[...])
    lane = lax.broadcasted_iota(jnp.int32, (tm, N_EXPERTS), 1).astype(F32)
    work = logits
    tops, idxs, hots = [], [], []
    for _ in range(TOP_K):
        m = jnp.max(work, axis=-1, keepdims=True)
        idx = jnp.min(jnp.where(work == m, lane, float(N_EXPERTS)), axis=-1, keepdims=True)
        hot = lane == idx
        work = jnp.where(hot, -jnp.inf, work)
        tops.append(m)
        idxs.append(idx)
        hots.append(hot)
    exps = [jnp.exp(t - tops[0]) for t in tops]
    den = exps[0] + exps[1] + exps[2] + exps[3]
    chosen = jnp.where(hots[0] | hots[1] | hots[2] | hots[3], 1.0, 0.0)
    r = lax.broadcasted_iota(jnp.int32, (tm, tm), 0)
    c = lax.broadcasted_iota(jnp.int32, (tm, tm), 1)
    before = jnp.where(c < r, 1.0, 0.0).astype(BF16)
    run = run_ref[...]
    tile_cnt = jnp.sum(chosen, axis=0, keepdims=True)
    units = jnp.ceil(tile_cnt / SUBLANES)
    earlier = jnp.where(lax.broadcasted_iota(jnp.int32, (N_EXPERTS, N_EXPERTS), 0)
                        < lax.broadcasted_iota(jnp.int32, (N_EXPERTS, N_EXPERTS), 1), 1.0, 0.0)
    group_row = SUBLANES * _dot(jnp.broadcast_to(units, (SUBLANES, N_EXPERTS)).astype(BF16),
                                earlier.astype(BF16))[0:1]
    place = _dot(before, chosen.astype(BF16)) + group_row
    mlane = lax.broadcasted_iota(jnp.int32, (tm, LANES), 1)
    meta = jnp.full((tm, LANES), -1.0, F32)
    ext = jnp.zeros((tm, LANES), F32)
    for k in range(TOP_K):
        row_k = jnp.sum(jnp.where(hots[k], place, 0.0), axis=-1, keepdims=True)
        meta = jnp.where(mlane == k, row_k, meta)
        ext = jnp.where(mlane == EXT_EXPERT + k, idxs[k], ext)
        rest = exps[k] / den
        for part in range(GATE_PARTS):
            piece = rest.astype(BF16).astype(F32)
            ext = jnp.where(mlane == EXT_GATE + GATE_PARTS * k + part, piece, ext)
            rest = rest - piece
    meta_ref[...] = meta
    meta_t_ref[0] = meta.T[:SUBLANES]
    hx_ref[:, :d] = hn_hi
    hx_ref[:, d:] = ext.astype(BF16)
    trun_ref[0] = run
    run = run + units * SUBLANES
    run_ref[...] = run
    cnt_ref[...] = run


def _router(h1, norm_w, router_w, router_b):
    n, d = h1.shape
    tm = TOKEN_TILE
    tiles = n // tm
    return pl.pallas_call(
        _router_kernel,
        grid=(tiles,),
        in_specs=[pl.BlockSpec((tm, d), lambda i: (i, 0)),
                  pl.BlockSpec((1, d), lambda i: (0, 0)),
                  pl.BlockSpec((d, 2 * N_EXPERTS), lambda i: (0, 0)),
                  pl.BlockSpec((1, N_EXPERTS), lambda i: (0, 0))],
        out_specs=[pl.BlockSpec((tm, d + LANES), lambda i: (i, 0)),
                   pl.BlockSpec((tm, LANES), lambda i: (i, 0)),
                   pl.BlockSpec((1, SUBLANES, tm), lambda i: (i, 0, 0)),
                   pl.BlockSpec((1, 1, N_EXPERTS), lambda i: (i, 0, 0)),
                   pl.BlockSpec((1, N_EXPERTS), lambda i: (0, 0))],
        out_shape=[jax.ShapeDtypeStruct((n, d + LANES), BF16),
                   jax.ShapeDtypeStruct((n, LANES), F32),
                   jax.ShapeDtypeStruct((tiles, SUBLANES, tm), F32),
                   jax.ShapeDtypeStruct((tiles, 1, N_EXPERTS), F32),
                   jax.ShapeDtypeStruct((1, N_EXPERTS), F32)],
        scratch_shapes=[pltpu.VMEM((1, N_EXPERTS), F32)],
        compiler_params=_params("arbitrary"),
        name="router",
    )(h1, norm_w.reshape(1, d), jnp.concatenate(_split_bf16(router_w), axis=1),
      router_b.reshape(1, N_EXPERTS))


def _routing_tables(tile_run, counts, n):
    tiles = tile_run.shape[0]
    counts = counts.reshape(N_EXPERTS).astype(jnp.int32)
    run = tile_run.reshape(tiles, N_EXPERTS).astype(jnp.int32)
    tile_len = jnp.concatenate([run[1:], counts[None]], axis=0) - run
    loc = jnp.cumsum(tile_len, axis=1) - tile_len
    padded = (counts + FFN_BLOCK - 1) // FFN_BLOCK * FFN_BLOCK
    seg_end = jnp.cumsum(padded)
    seg_start = seg_end - padded
    dst0 = seg_start[None, :] + run
    zero0 = seg_start + counts
    nzero = (padded - counts) // SUBLANES
    n_real = seg_end[-1] // FFN_BLOCK
    max_blocks = (n * TOP_K + tiles * N_EXPERTS * (SUBLANES - 1)
                  + N_EXPERTS * (FFN_BLOCK - 1)) // FFN_BLOCK
    blk_exp = jnp.minimum(jnp.sum(jnp.arange(max_blocks, dtype=jnp.int32)[:, None] * FFN_BLOCK
                                  >= seg_end[None, :], axis=1), N_EXPERTS - 1).astype(jnp.int32)
    ncopy = [jnp.sum(tile_len // CHUNK, axis=1)]
    ncopy += [jnp.sum((tile_len // rows) % 2, axis=1) for rows in _copy_sizes()[1:]]
    ncopy = jnp.stack(ncopy, axis=1).astype(jnp.int32)
    return dict(dst0=dst0.reshape(-1), tile_len=tile_len.reshape(-1), loc=loc.reshape(-1),
                ncopy=ncopy.reshape(-1), zero0=zero0, nzero=nzero, blk_exp=blk_exp,
                n_real=n_real.reshape(1).astype(jnp.int32), cap=max_blocks * FFN_BLOCK)


def _copy_sizes():
    sizes, rows = [CHUNK], CHUNK // 2
    while rows >= SUBLANES:
        sizes.append(rows)
        rows //= 2
    return sizes


def _wait_copies(t, ncopy_ref, make_copy):
    sizes = _copy_sizes()
    for c, rows in enumerate(sizes):
        def body(_, carry, rows=rows):
            make_copy(0, 0, rows).wait()
            return carry

        lax.fori_loop(0, ncopy_ref[t * len(sizes) + c], body, 0)


def _start_copies(t, tbl_ref, len_ref, loc_ref, make_copy):
    def one(seg_row, loc_row, rows):
        make_copy(pl.multiple_of(seg_row, SUBLANES), pl.multiple_of(loc_row, SUBLANES), rows).start()

    for e in range(N_EXPERTS):
        idx = t * N_EXPERTS + e
        seg_row = tbl_ref[idx]
        loc_row = loc_ref[idx]
        length = len_ref[idx]
        full = length // CHUNK

        def body(c, carry):
            one(seg_row + c * CHUNK, loc_row + c * CHUNK, CHUNK)
            return carry

        lax.fori_loop(0, full, body, 0)
        done = full * CHUNK
        rows = CHUNK // 2
        while rows >= SUBLANES:
            @pl.when((length & rows) != 0)
            def _(done=done, rows=rows):
                one(seg_row + done, loc_row + done, rows)
            done = done + (length & rows)
            rows //= 2


def _dispatch_kernel(dst0_ref, len_ref, loc_ref, ncopy_ref, zero0_ref, nzero_ref, nreal_ref,
                     hx_ref, rows_t_ref, xb_ref, srt_ref, zero_ref, sem):
    t = pl.program_id(0)
    tm = hx_ref.shape[0]
    rows_t = rows_t_ref[0]
    slot = t % 2
    part = SORTED_ROWS // SORT_PARTS
    for c in range(SORT_PARTS):
        prow = (lax.broadcasted_iota(jnp.int32, (part, tm), 0) + c * part).astype(F32)
        onehot = jnp.zeros((part, tm), F32)
        for k in range(TOP_K):
            onehot = jnp.where(prow == rows_t[k:k + 1, :], 1.0, onehot)
        srt_ref[slot, c * part:(c + 1) * part] = _dot(onehot.astype(BF16), hx_ref[...])

    def copy_from(s):
        def copy(seg_row, loc_row, rows):
            return pltpu.make_async_copy(srt_ref.at[s, pl.ds(loc_row, rows)],
                                         xb_ref.at[pl.ds(seg_row, rows)], sem.at[s])
        return copy

    @pl.when(t > 0)
    def _():
        _wait_copies(t - 1, ncopy_ref, copy_from(1 - slot))

    _start_copies(t, dst0_ref, len_ref, loc_ref, copy_from(slot))

    @pl.when(t == pl.num_programs(0) - 1)
    def _():
        _wait_copies(t, ncopy_ref, copy_from(slot))
        zero_ref[...] = jnp.zeros_like(zero_ref)
        for op in ("start", "wait"):
            for e in range(N_EXPERTS):
                def body(c, carry):
                    row0 = pl.multiple_of(zero0_ref[e] + c * SUBLANES, SUBLANES)
                    cp = pltpu.make_async_copy(zero_ref.at[pl.ds(0, SUBLANES)],
                                               xb_ref.at[pl.ds(row0, SUBLANES)], sem.at[0])
                    cp.start() if op == "start" else cp.wait()
                    return carry

                lax.fori_loop(0, nzero_ref[e], body, 0)

            def tail(b, carry):
                row0 = pl.multiple_of(b * FFN_BLOCK, FFN_BLOCK)
                cp = pltpu.make_async_copy(zero_ref, xb_ref.at[pl.ds(row0, FFN_BLOCK)], sem.at[0])
                cp.start() if op == "start" else cp.wait()
                return carry

            lax.fori_loop(nreal_ref[0], xb_ref.shape[0] // FFN_BLOCK, tail, 0)


def _dispatch(hx, meta_t, tbl):
    n, width = hx.shape
    tm = TOKEN_TILE
    grid_spec = pltpu.PrefetchScalarGridSpec(
        num_scalar_prefetch=7,
        grid=(n // tm,),
        in_specs=[pl.BlockSpec((tm, width), lambda i, *_: (i, 0)),
                  pl.BlockSpec((1, SUBLANES, tm), lambda i, *_: (i, 0, 0))],
        out_specs=pl.BlockSpec(memory_space=pl.ANY),
        scratch_shapes=[pltpu.VMEM((2, SORTED_ROWS, width), F32),
                        pltpu.VMEM((FFN_BLOCK, width), F32),
                        pltpu.SemaphoreType.DMA((2,))],
    )
    return pl.pallas_call(
        _dispatch_kernel,
        grid_spec=grid_spec,
        out_shape=jax.ShapeDtypeStruct((tbl["cap"], width), F32),
        compiler_params=_params("arbitrary"),
        name="dispatch",
    )(tbl["dst0"], tbl["tile_len"], tbl["loc"], tbl["ncopy"], tbl["zero0"], tbl["nzero"], tbl["n_real"],
      hx, meta_t)


def _ffn_kernel(exp_ref, nreal_ref, x_ref, wgu_ref, bgu_ref, wd_ref, bd_ref, y_ref, wgu_bf, wd_bf):
    b = pl.program_id(0)
    d_ff, d = wd_ref.shape[1], wd_ref.shape[2]

    @pl.when(b >= nreal_ref[0])
    def _():
        y_ref[...] = jnp.zeros_like(y_ref)

    @pl.when((b == 0) | (exp_ref[b] != exp_ref[jnp.maximum(b - 1, 0)]))
    def _():
        wgu_bf[...] = wgu_ref[0].astype(BF16)
        wd_bf[...] = wd_ref[0].astype(BF16)

    @pl.when(b < nreal_ref[0])
    def _():
        x = x_ref[...]
        ext = x[:, d:]
        expert = exp_ref[b].astype(F32)
        gate = jnp.zeros((x.shape[0], 1), F32)
        for k in range(TOP_K):
            g0 = EXT_GATE + GATE_PARTS * k
            g = ext[:, g0:g0 + 1] + ext[:, g0 + 1:g0 + 2] + ext[:, g0 + 2:g0 + 3]
            gate = gate + jnp.where(ext[:, EXT_EXPERT + k:EXT_EXPERT + k + 1] == expert, g, 0.0)
        gu = _dot(x[:, :d].astype(BF16), wgu_bf[...]) + bgu_ref[0]
        g = jnp.minimum(gu[:, :d_ff], SWIGLU_LIMIT)
        u = jnp.clip(gu[:, d_ff:], -SWIGLU_LIMIT, SWIGLU_LIMIT)
        act = (u + 1.0) * (g * jax.nn.sigmoid(SWIGLU_ALPHA * g))
        y_ref[...] = gate * (_dot(act.astype(BF16), wd_bf[...]) + bd_ref[0])


def _ffn(xb, tbl, w_gate_up, b_gate_up, w_down, b_down):
    cap, width = xb.shape
    e, d, two_ff = w_gate_up.shape
    d_ff = two_ff // 2
    grid_spec = pltpu.PrefetchScalarGridSpec(
        num_scalar_prefetch=2,
        grid=(cap // FFN_BLOCK,),
        in_specs=[pl.BlockSpec((FFN_BLOCK, width), lambda b, ex, nr: (b, 0)),
                  pl.BlockSpec((1, d, two_ff), lambda b, ex, nr: (ex[b], 0, 0)),
                  pl.BlockSpec((1, 1, two_ff), lambda b, ex, nr: (ex[b], 0, 0)),
                  pl.BlockSpec((1, d_ff, d), lambda b, ex, nr: (ex[b], 0, 0)),
                  pl.BlockSpec((1, 1, d), lambda b, ex, nr: (ex[b], 0, 0))],
        out_specs=pl.BlockSpec((FFN_BLOCK, d), lambda b, ex, nr: (b, 0)),
        scratch_shapes=[pltpu.VMEM((d, two_ff), BF16), pltpu.VMEM((d_ff, d), BF16)],
    )
    return pl.pallas_call(
        _ffn_kernel,
        grid_spec=grid_spec,
        out_shape=jax.ShapeDtypeStruct((cap, d), F32),
        compiler_params=_params("arbitrary"),
        name="expert_ffn",
    )(tbl["blk_exp"], tbl["n_real"], xb, w_gate_up,
      b_gate_up.reshape(e, 1, two_ff), w_down, b_down.reshape(e, 1, d))


def _combine_kernel(dst0_ref, len_ref, loc_ref, ncopy_ref, meta_ref, h_ref, y_ref, o_ref, srt_ref, sem):
    t = pl.program_id(0)
    tm = h_ref.shape[0]

    @pl.when(t == 0)
    def _():
        srt_ref[...] = jnp.zeros_like(srt_ref)

    slot = t % 2

    def copy_into(s):
        def copy(seg_row, loc_row, rows):
            return pltpu.make_async_copy(y_ref.at[pl.ds(seg_row, rows)],
                                         srt_ref.at[s, pl.ds(loc_row, rows)], sem.at[s])
        return copy

    @pl.when(t == 0)
    def _():
        _start_copies(t, dst0_ref, len_ref, loc_ref, copy_into(slot))

    @pl.when(t + 1 < pl.num_programs(0))
    def _():
        _start_copies(t + 1, dst0_ref, len_ref, loc_ref, copy_into(1 - slot))

    meta = meta_ref[...]
    rows = [meta[:, k:k + 1] for k in range(TOP_K)]
    part = SORTED_ROWS // SORT_PARTS

    def onehot_part(c):
        pcol = (lax.broadcasted_iota(jnp.int32, (tm, part), 1) + c * part).astype(F32)
        onehot = jnp.zeros((tm, part), F32)
        for k in range(TOP_K):
            onehot = jnp.where(pcol == rows[k], 1.0, onehot)
        return onehot.astype(BF16)

    first = onehot_part(0)
    _wait_copies(t, ncopy_ref, copy_into(slot))
    out = h_ref[...]
    for c in range(SORT_PARTS):
        onehot = first if c == 0 else onehot_part(c)
        out = out + _dot(onehot, srt_ref[slot, c * part:(c + 1) * part].astype(BF16))
    o_ref[...] = out


def _combine(meta, tbl, h1, yb):
    n, d = h1.shape
    tm = TOKEN_TILE
    grid_spec = pltpu.PrefetchScalarGridSpec(
        num_scalar_prefetch=4,
        grid=(n // tm,),
        in_specs=[pl.BlockSpec((tm, LANES), lambda i, *_: (i, 0)),
                  pl.BlockSpec((tm, d), lambda i, *_: (i, 0)),
                  pl.BlockSpec(memory_space=pl.ANY)],
        out_specs=pl.BlockSpec((tm, d), lambda i, *_: (i, 0)),
        scratch_shapes=[pltpu.VMEM((2, SORTED_ROWS, d), F32), pltpu.SemaphoreType.DMA((2,))],
    )
    return pl.pallas_call(
        _combine_kernel,
        grid_spec=grid_spec,
        out_shape=jax.ShapeDtypeStruct((n, d), F32),
        compiler_params=_params("arbitrary"),
        name="combine",
    )(tbl["dst0"], tbl["tile_len"], tbl["loc"], tbl["ncopy"], meta, h1, yb)


def kernel(x, attn_norm_w, w_in, q_norm_w, k_norm_w, sinks, w_proj_swa, w_proj_sb, w_out,
           ffn_norm_w, router_w, router_b, w_gate_up, b_gate_up, w_down, b_down):
    batch, seq, d = x.shape
    n = batch * seq
    h = x.reshape(n, d)
    for layer in range(attn_norm_w.shape[0]):
        q_a, kv_a, q_b, k_b, v_b, g_a, g_b = _in_proj(h, attn_norm_w[layer], w_in[layer].astype(BF16))
        y_a = _swa(q_a, kv_a, q_norm_w[layer], k_norm_w[layer], sinks[layer], batch, seq)
        y_b = _sb(q_b, k_b, v_b, batch, seq)
        h1 = _merge(y_a, y_b, g_a, g_b, h, w_proj_swa[layer].astype(BF16),
                    w_proj_sb[layer].astype(BF16), w_out[layer].astype(BF16))
        hx, meta, meta_t, tile_run, counts = _router(h1, ffn_norm_w[layer], router_w[layer], router_b[layer])
        tbl = _routing_tables(tile_run, counts, n)
        xb = _dispatch(hx, meta_t, tbl)
        yb = _ffn(xb, tbl, w_gate_up[layer], b_gate_up[layer], w_down[layer], b_down[layer])
        h = _combine(meta, tbl, h1, yb)
    return h.reshape(batch, seq, d)
```

```python
import jax
import jax.numpy as jnp
from jax import lax
from jax.experimental import pallas as pl
from jax.experimental.pallas import tpu as pltpu

HEAD_DIM = 64
SWA_HEADS = 8
SWA_KV_HEADS = 2
SWA_GROUP = SWA_HEADS // SWA_KV_HEADS
BLOCK = 128
SB_HEADS = 8
SWA_WIDTH = SWA_HEADS * HEAD_DIM
SWA_KV_WIDTH = SWA_KV_HEADS * HEAD_DIM
SB_WIDTH = SB_HEADS * HEAD_DIM
N_EXPERTS = 32
TOP_K = 4
SWIGLU_LIMIT = 7.0
SWIGLU_ALPHA = 1.702
NORM_EPS = 1e-5
ATTN_SCALE = HEAD_DIM ** -0.5
SIGN_BIT = -2 ** 31

LANES = 128
SUBLANES = 8
DENSE_TILE = 1024
TOKEN_TILE = 512
FFN_BLOCK = 512
CHUNK = 32
SORTED_ROWS = -(-(TOKEN_TILE * TOP_K + N_EXPERTS * (SUBLANES - 1)) // LANES) * LANES
SWA_BLOCKS = 8
SORT_PARTS = 2
SB_Q_ROWS = 1024
SB_K_ROWS = 256
SB_GROUP = 1
SB_STAGGER = 1
VMEM_LIMIT = 56 * 1024 * 1024

F32 = jnp.float32
BF16 = jnp.bfloat16
NEG_BIG = -1e30


def _dot(a, b):
    return jnp.dot(a, b, preferred_element_type=F32)


def _dot_nt(a, b):
    return lax.dot_general(a, b, (((1,), (1,)), ((), ())), preferred_element_type=F32)


def _sigmoid(x):
    return jax.nn.sigmoid(x)


def _rms(x, w):
    return x * lax.rsqrt(jnp.mean(x * x, axis=-1, keepdims=True) + NORM_EPS) * w


def _params(*sem):
    return pltpu.CompilerParams(dimension_semantics=sem, vmem_limit_bytes=VMEM_LIMIT)


def _in_proj_kernel(x_ref, nw_ref, w_ref, qa_ref, kva_ref, qb_ref, kb_ref, vb_ref, ga_ref, gb_ref):
    xn = _rms(x_ref[...], nw_ref[...]).astype(BF16)
    off = 0
    for ref, gate in ((qa_ref, False), (kva_ref, False), (qb_ref, False), (kb_ref, False),
                      (vb_ref, False), (ga_ref, True), (gb_ref, True)):
        width = ref.shape[1]
        y = _dot(xn, w_ref[:, off:off + width])
        if gate:
            y = _sigmoid(y)
        ref[...] = y.astype(ref.dtype)
        off += width


def _in_proj(x2, norm_w, w_in):
    n, d = x2.shape
    widths = (SWA_WIDTH, 2 * SWA_KV_WIDTH, SB_WIDTH, SB_WIDTH, SB_WIDTH, d, d)
    tm = DENSE_TILE
    return pl.pallas_call(
        _in_proj_kernel,
        grid=(n // tm,),
        in_specs=[pl.BlockSpec((tm, d), lambda i: (i, 0)),
                  pl.BlockSpec((1, d), lambda i: (0, 0)),
                  pl.BlockSpec(w_in.shape, lambda i: (0, 0))],
        out_specs=[pl.BlockSpec((tm, w), lambda i: (i, 0)) for w in widths],
        out_shape=[jax.ShapeDtypeStruct((n, w), BF16) for w in widths],
        compiler_params=_params("parallel"),
        name="in_proj",
    )(x2, norm_w.reshape(1, d), w_in)


def _swa_kernel(q_ref, kvc_ref, kvp_ref, qnw_ref, knw_ref, sink_ref, o_ref):
    i = pl.program_id(1)
    half = HEAD_DIM

    def group_mean_sq(x):
        w = x.shape[1]
        same_head = (lax.broadcasted_iota(jnp.int32, (w, w), 0) // half
                     == lax.broadcasted_iota(jnp.int32, (w, w), 1) // half)
        avg = jnp.where(same_head, 1.0 / half, 0.0).astype(BF16)
        hi, lo = _split_bf16(x * x)
        return _dot(hi, avg) + _dot(lo, avg)

    q = q_ref[...].astype(F32)
    kv = jnp.concatenate([kvp_ref[...], kvc_ref[...]], axis=0)
    k = kv[:, :SWA_KV_WIDTH].astype(F32)
    v = kv[:, SWA_KV_WIDTH:].astype(F32)
    qn = (q * lax.rsqrt(group_mean_sq(q) + NORM_EPS) * qnw_ref[...] * ATTN_SCALE).astype(BF16)
    kn = k * lax.rsqrt(group_mean_sq(k) + NORM_EPS) * knw_ref[...]
    lane = lax.broadcasted_iota(jnp.int32, kn.shape, 1)
    low = lane < half
    k_swap = pltpu.roll(kn, half, 1)
    v_swap = pltpu.roll(v, half, 1).astype(BF16)
    v_same = v.astype(BF16)
    keys = [[jnp.where(low, kn, 0.0).astype(BF16), jnp.where(low, 0.0, k_swap).astype(BF16)],
            [jnp.where(low, k_swap, 0.0).astype(BF16), jnp.where(low, 0.0, kn).astype(BF16)]]
    vals = [[v_same, v_swap], [v_swap, v_same]]

    row = lax.broadcasted_iota(jnp.int32, (BLOCK, 2 * BLOCK), 0)
    col = lax.broadcasted_iota(jnp.int32, (BLOCK, 2 * BLOCK), 1)
    rel = row + BLOCK - col
    in_window = (rel >= 0) & (rel < BLOCK)
    rel = rel.astype(F32)
    out_lane_low = lax.broadcasted_iota(jnp.int32, (BLOCK, LANES), 1) < half
    for sub in range(SWA_BLOCKS):
        rows = slice(sub * BLOCK, (sub + 1) * BLOCK)
        band = slice(sub * BLOCK, (sub + 2) * BLOCK)
        valid = in_window & ((col >= BLOCK) | (i > 0)) if sub == 0 else in_window
        for blk in range(SWA_HEADS // 2):
            qb = qn[rows, blk * LANES:(blk + 1) * LANES]
            halves = []
            for par in range(2):
                h = 2 * blk + par
                g = h // SWA_GROUP
                slope = 2.0 ** (-8.0 * (h + 1) / SWA_HEADS)
                s = jnp.where(valid, _dot_nt(qb, keys[g][par][band]) - slope * rel, NEG_BIG)
                sink = sink_ref[h]
                m = jnp.maximum(jnp.max(s, axis=-1, keepdims=True), sink)
                p = jnp.exp(s - m)
                den = jnp.sum(p, axis=-1, keepdims=True) + jnp.exp(sink - m)
                halves.append(_dot(p.astype(BF16), vals[g][par][band]) / den)
            o_ref[rows, blk * LANES:(blk + 1) * LANES] = jnp.where(
                out_lane_low, halves[0], halves[1]).astype(o_ref.dtype)


def _swa(q_a, kv_a, q_norm_w, k_norm_w, sinks, batch, seq):
    n = q_a.shape[0]
    rows = SWA_BLOCKS * BLOCK
    nb = seq // rows
    return pl.pallas_call(
        _swa_kernel,
        grid=(batch, nb),
        in_specs=[pl.BlockSpec((rows, SWA_WIDTH), lambda b, i: (b * nb + i, 0)),
                  pl.BlockSpec((rows, 2 * SWA_KV_WIDTH), lambda b, i: (b * nb + i, 0)),
                  pl.BlockSpec((BLOCK, 2 * SWA_KV_WIDTH),
                               lambda b, i: (jnp.maximum((b * nb + i) * SWA_BLOCKS - 1, 0), 0)),
                  pl.BlockSpec((1, SWA_WIDTH), lambda b, i: (0, 0)),
                  pl.BlockSpec((1, SWA_KV_WIDTH), lambda b, i: (0, 0)),
                  pl.BlockSpec(memory_space=pltpu.SMEM)],
        out_specs=pl.BlockSpec((rows, SWA_WIDTH), lambda b, i: (b * nb + i, 0)),
        out_shape=jax.ShapeDtypeStruct((n, SWA_WIDTH), BF16),
        compiler_params=_params("parallel", "parallel"),
        name="swa",
    )(q_a, kv_a, kv_a, jnp.tile(q_norm_w, SWA_HEADS).reshape(1, SWA_WIDTH),
      jnp.tile(k_norm_w, SWA_KV_HEADS).reshape(1, SWA_KV_WIDTH), sinks)


def _split_bf16(x):
    hi = x.astype(BF16)
    lo = (x - hi.astype(F32)).astype(BF16)
    return hi, lo


def _sb_kernel(q_ref, k_ref, v_ref, o_ref):
    seq = q_ref.shape[0]
    tq, tk = SB_Q_ROWS, SB_K_ROWS
    nq = seq // tq
    k_per_q = tq // tk
    lane = lax.broadcasted_iota(jnp.int32, (tq, LANES), 1)
    row = lax.broadcasted_iota(jnp.int32, (tq, tk), 0)
    col = lax.broadcasted_iota(jnp.int32, (tq, tk), 1)
    neg_incl = jnp.where(lax.broadcasted_iota(jnp.int32, (tk, tk), 0)
                         >= lax.broadcasted_iota(jnp.int32, (tk, tk), 1), -1.0, 0.0).astype(BF16)

    def softplus(z):
        neg_abs = lax.bitcast_convert_type(lax.bitcast_convert_type(z, jnp.int32) | SIGN_BIT, F32)
        return jnp.maximum(z, 0.0) + jnp.log(1.0 + jnp.exp(neg_abs))

    def neg_suffix_sum(sp):
        return _dot(sp.astype(BF16), neg_incl)

    def add_from_row(x, r0, delta):
        return x + delta if r0 == 0 else jnp.concatenate([x[:r0], x[r0:] + delta], axis=0)

    def k_span(k0s, qms, state, masks, first_rows):
        pairs = [(i, p) for p in range(2) for i in range(len(k0s))]
        kbs = [k_ref[pl.ds(k0, tk), :] for k0 in k0s]
        vbs = [v_ref[pl.ds(k0, tk), :] for k0 in k0s]
        accs, runs = list(state[:2]), list(state[2:])
        zs, sps, cs, ws = {}, {}, {}, {}

        def stage(s, i, p):
            r0 = first_rows[i]
            if s == 0:
                zs[(i, p)] = _dot_nt(qms[p][r0:], kbs[i])
            elif s == 1:
                sp = softplus(zs[(i, p)])
                sps[(i, p)] = sp if masks[i] is None else jnp.where(masks[i], sp, 0.0)
            elif s == 2:
                cs[(i, p)] = neg_suffix_sum(sps[(i, p)])
            elif s == 3:
                w = jnp.exp(zs[(i, p)] + cs[(i, p)] + runs[p][r0:])
                ws[(i, p)] = w if masks[i] is None else jnp.where(masks[i], w, 0.0)
                runs[p] = add_from_row(runs[p], r0, cs[(i, p)][:, 0:1])
            else:
                accs[p] = add_from_row(accs[p], r0, _dot(ws[(i, p)].astype(BF16), vbs[i]))

        n_stages = 5
        for step in range(((len(pairs) - 1) // SB_GROUP) * SB_STAGGER + n_stages):
            for n, (i, p) in enumerate(pairs):
                s = step - (n // SB_GROUP) * SB_STAGGER
                if 0 <= s < n_stages:
                    stage(s, i, p)
        return (*accs, *runs)

    def q_block(qi, carry):
        q0 = pl.multiple_of(qi * tq, tq)
        q2 = q_ref[pl.ds(q0, tq), :] * ATTN_SCALE
        qms = (jnp.where(lane < HEAD_DIM, q2, 0.0).astype(BF16),
               jnp.where(lane >= HEAD_DIM, q2, 0.0).astype(BF16))
        state = (jnp.zeros((tq, LANES), F32), jnp.zeros((tq, LANES), F32),
                 jnp.zeros((tq, 1), F32), jnp.zeros((tq, 1), F32))
        order = list(reversed(range(k_per_q)))
        state = k_span([q0 + d * tk for d in order], qms, state,
                       [((col + d * tk) < row)[d * tk:] for d in order], [d * tk for d in order])

        def earlier(jj, state):
            k0s = [pl.multiple_of((qi - 1 - jj) * tq + d * tk, tk) for d in order]
            return k_span(k0s, qms, state, [None] * k_per_q, [0] * k_per_q)

        state = lax.fori_loop(0, qi, earlier, state)
        o_ref[pl.ds(q0, tq), :] = jnp.where(lane < HEAD_DIM, state[0], state[1]).astype(o_ref.dtype)
        return carry

    lax.fori_loop(0, nq, q_block, 0)


def _sb(q_b, k_b, v_b, batch, seq):
    n = q_b.shape[0]
    spec = pl.BlockSpec((seq, LANES), lambda b, p: (b, p))
    return pl.pallas_call(
        _sb_kernel,
        grid=(batch, SB_WIDTH // LANES),
        in_specs=[spec, spec, spec],
        out_specs=spec,
        out_shape=jax.ShapeDtypeStruct((n, SB_WIDTH), BF16),
        compiler_params=_params("parallel", "parallel"),
        name="stick_breaking",
    )(q_b, k_b, v_b)


def _merge_kernel(ya_ref, yb_ref, ga_ref, gb_ref, x_ref, pa_ref, pb_ref, wo_ref, h_ref):
    m = (ga_ref[...].astype(F32) * _dot(ya_ref[...], pa_ref[...])
         + gb_ref[...].astype(F32) * _dot(yb_ref[...], pb_ref[...]))
    h_ref[...] = x_ref[...] + _dot(m.astype(BF16), wo_ref[...])


def _merge(y_a, y_b, g_a, g_b, x2, p_a, p_b, w_o):
    n, d = x2.shape
    tm = DENSE_TILE
    row = lambda w: pl.BlockSpec((tm, w), lambda i: (i, 0))
    full = lambda a: pl.BlockSpec(a.shape, lambda i: (0, 0))
    return pl.pallas_call(
        _merge_kernel,
        grid=(n // tm,),
        in_specs=[row(SWA_WIDTH), row(SB_WIDTH), row(d), row(d), row(d), full(p_a), full(p_b), full(w_o)],
        out_specs=row(d),
        out_shape=jax.ShapeDtypeStruct((n, d), F32),
        compiler_params=_params("parallel"),
        name="merge",
    )(y_a, y_b, g_a, g_b, x2, p_a, p_b, w_o)


EXT_EXPERT = 0
EXT_GATE = TOP_K
GATE_PARTS = 3


def _router_kernel(h_ref, nw_ref, rw_ref, rb_ref, hx_ref, meta_ref, meta_t_ref, trun_ref, cnt_ref, run_ref):
    tm, d = h_ref.shape

    @pl.when(pl.program_id(0) == 0)
    def _():
        run_ref[...] = jnp.zeros_like(run_ref)

    hn = _rms(h_ref[...], nw_ref[...])
    hn_hi, hn_lo = _split_bf16(hn)
    both = _dot(hn_hi, rw_ref[...])
    logits = (both[:, :N_EXPERTS] + both[:, N_EXPERTS:] + _dot(hn_lo, rw_ref[:, :N_EXPERTS])
              + rb_ref[...])
    lane = lax.broadcasted_iota(jnp.int32, (tm, N_EXPERTS), 1).astype(F32)
    work = logits
    tops, idxs, hots = [], [], []
    for _ in range(TOP_K):
        m = jnp.max(work, axis=-1, keepdims=True)
        idx = jnp.min(jnp.where(work == m, lane, float(N_EXPERTS)), axis=-1, keepdims=True)
        hot = lane == idx
        work = jnp.where(hot, -jnp.inf, work)
        tops.append(m)
        idxs.append(idx)
        hots.append(hot)
    exps = [jnp.exp(t - tops[0]) for t in tops]
    den = exps[0] + exps[1] + exps[2] + exps[3]
    chosen = jnp.where(hots[0] | hots[1] | hots[2] | hots[3], 1.0, 0.0)
    r = lax.broadcasted_iota(jnp.int32, (tm, tm), 0)
    c = lax.broadcasted_iota(jnp.int32, (tm, tm), 1)
    before = jnp.where(c < r, 1.0, 0.0).astype(BF16)
    run = run_ref[...]
    tile_cnt = jnp.sum(chosen, axis=0, keepdims=True)
    units = jnp.ceil(tile_cnt / SUBLANES)
    earlier = jnp.where(lax.broadcasted_iota(jnp.int32, (N_EXPERTS, N_EXPERTS), 0)
                        < lax.broadcasted_iota(jnp.int32, (N_EXPERTS, N_EXPERTS), 1), 1.0, 0.0)
    group_row = SUBLANES * _dot(jnp.broadcast_to(units, (SUBLANES, N_EXPERTS)).astype(BF16),
                                earlier.astype(BF16))[0:1]
    place = _dot(before, chosen.astype(BF16)) + group_row
    mlane = lax.broadcasted_iota(jnp.int32, (tm, LANES), 1)
    meta = jnp.full((tm, LANES), -1.0, F32)
    ext = jnp.zeros((tm, LANES), F32)
    for k in range(TOP_K):
        row_k = jnp.sum(jnp.where(hots[k], place, 0.0), axis=-1, keepdims=True)
        meta = jnp.where(mlane == k, row_k, meta)
        ext = jnp.where(mlane == EXT_EXPERT + k, idxs[k], ext)
        rest = exps[k] / den
        for part in range(GATE_PARTS):
            piece = rest.astype(BF16).astype(F32)
            ext = jnp.where(mlane == EXT_GATE + GATE_PARTS * k + part, piece, ext)
            rest = rest - piece
    meta_ref[...] = meta
    meta_t_ref[0] = meta.T[:SUBLANES]
    hx_ref[:, :d] = hn_hi
    hx_ref[:, d:] = ext.astype(BF16)
    trun_ref[0] = run
    run = run + units * SUBLANES
    run_ref[...] = run
    cnt_ref[...] = run


def _router(h1, norm_w, router_w, router_b):
    n, d = h1.shape
    tm = TOKEN_TILE
    tiles = n // tm
    return pl.pallas_call(
        _router_kernel,
        grid=(tiles,),
        in_specs=[pl.BlockSpec((tm, d), lambda i: (i, 0)),
                  pl.BlockSpec((1, d), lambda i: (0, 0)),
                  pl.BlockSpec((d, 2 * N_EXPERTS), lambda i: (0, 0)),
                  pl.BlockSpec((1, N_EXPERTS), lambda i: (0, 0))],
        out_specs=[pl.BlockSpec((tm, d + LANES), lambda i: (i, 0)),
                   pl.BlockSpec((tm, LANES), lambda i: (i, 0)),
                   pl.BlockSpec((1, SUBLANES, tm), lambda i: (i, 0, 0)),
                   pl.BlockSpec((1, 1, N_EXPERTS), lambda i: (i, 0, 0)),
                   pl.BlockSpec((1, N_EXPERTS), lambda i: (0, 0))],
        out_shape=[jax.ShapeDtypeStruct((n, d + LANES), BF16),
                   jax.ShapeDtypeStruct((n, LANES), F32),
                   jax.ShapeDtypeStruct((tiles, SUBLANES, tm), F32),
                   jax.ShapeDtypeStruct((tiles, 1, N_EXPERTS), F32),
                   jax.ShapeDtypeStruct((1, N_EXPERTS), F32)],
        scratch_shapes=[pltpu.VMEM((1, N_EXPERTS), F32)],
        compiler_params=_params("arbitrary"),
        name="router",
    )(h1, norm_w.reshape(1, d), jnp.concatenate(_split_bf16(router_w), axis=1),
      router_b.reshape(1, N_EXPERTS))


def _routing_tables(tile_run, counts, n):
    tiles = tile_run.shape[0]
    counts = counts.reshape(N_EXPERTS).astype(jnp.int32)
    run = tile_run.reshape(tiles, N_EXPERTS).astype(jnp.int32)
    tile_len = jnp.concatenate([run[1:], counts[None]], axis=0) - run
    loc = jnp.cumsum(tile_len, axis=1) - tile_len
    padded = (counts + FFN_BLOCK - 1) // FFN_BLOCK * FFN_BLOCK
    seg_end = jnp.cumsum(padded)
    seg_start = seg_end - padded
    dst0 = seg_start[None, :] + run
    zero0 = seg_start + counts
    nzero = (padded - counts) // SUBLANES
    n_real = seg_end[-1] // FFN_BLOCK
    max_blocks = (n * TOP_K + tiles * N_EXPERTS * (SUBLANES - 1)
                  + N_EXPERTS * (FFN_BLOCK - 1)) // FFN_BLOCK
    blk_exp = jnp.minimum(jnp.sum(jnp.arange(max_blocks, dtype=jnp.int32)[:, None] * FFN_BLOCK
                                  >= seg_end[None, :], axis=1), N_EXPERTS - 1).astype(jnp.int32)
    ncopy = jnp.stack([jnp.sum((tile_len // rows) % 2, axis=1) for rows in _copy_sizes()],
                      axis=1).astype(jnp.int32)
    return dict(dst0=dst0.reshape(-1), tile_len=tile_len.reshape(-1), loc=loc.reshape(-1),
                ncopy=ncopy.reshape(-1), zero0=zero0, nzero=nzero, blk_exp=blk_exp,
                n_real=n_real.reshape(1).astype(jnp.int32), cap=max_blocks * FFN_BLOCK)


def _copy_sizes():
    sizes, rows = [], TOKEN_TILE
    while rows >= SUBLANES:
        sizes.append(rows)
        rows //= 2
    return sizes


def _wait_copies(t, ncopy_ref, make_copy):
    sizes = _copy_sizes()
    for c, rows in enumerate(sizes):
        def body(_, carry, rows=rows):
            make_copy(0, 0, rows).wait()
            return carry

        lax.fori_loop(0, ncopy_ref[t * len(sizes) + c], body, 0)


def _start_copies(t, tbl_ref, len_ref, loc_ref, make_copy, enabled):
    for e in range(N_EXPERTS):
        idx = t * N_EXPERTS + e
        seg_row = tbl_ref[idx]
        loc_row = loc_ref[idx]
        length = len_ref[idx]
        done = 0
        for rows in _copy_sizes():
            @pl.when(((length & rows) != 0) & enabled)
            def _(done=done, rows=rows):
                make_copy(pl.multiple_of(seg_row + done, SUBLANES),
                          pl.multiple_of(loc_row + done, SUBLANES), rows).start()
            done = done + (length & rows)


def _dispatch_kernel(dst0_ref, len_ref, loc_ref, ncopy_ref, zero0_ref, nzero_ref, nreal_ref,
                     hx_ref, rows_t_ref, xb_ref, srt_ref, zero_ref, sem):
    t = pl.program_id(0)
    tm = hx_ref.shape[0]
    rows_t = rows_t_ref[0]
    slot = t % 2

    def copy_from(s):
        def copy(seg_row, loc_row, rows):
            return pltpu.make_async_copy(srt_ref.at[s, pl.ds(loc_row, rows)],
                                         xb_ref.at[pl.ds(seg_row, rows)], sem.at[s])
        return copy

    _start_copies(jnp.maximum(t - 1, 0), dst0_ref, len_ref, loc_ref, copy_from(1 - slot), t > 0)
    part = SORTED_ROWS // SORT_PARTS
    for c in range(SORT_PARTS):
        prow = (lax.broadcasted_iota(jnp.int32, (part, tm), 0) + c * part).astype(F32)
        onehot = jnp.zeros((part, tm), F32)
        for k in range(TOP_K):
            onehot = jnp.where(prow == rows_t[k:k + 1, :], 1.0, onehot)
        srt_ref[slot, c * part:(c + 1) * part] = _dot(onehot.astype(BF16), hx_ref[...])

    @pl.when(t > 0)
    def _():
        _wait_copies(t - 1, ncopy_ref, copy_from(1 - slot))

    @pl.when(t == pl.num_programs(0) - 1)
    def _():
        _start_copies(t, dst0_ref, len_ref, loc_ref, copy_from(slot), True)
        _wait_copies(t, ncopy_ref, copy_from(slot))
        zero_ref[...] = jnp.zeros_like(zero_ref)
        for op in ("start", "wait"):
            for e in range(N_EXPERTS):
                def body(c, carry):
                    row0 = pl.multiple_of(zero0_ref[e] + c * SUBLANES, SUBLANES)
                    cp = pltpu.make_async_copy(zero_ref.at[pl.ds(0, SUBLANES)],
                                               xb_ref.at[pl.ds(row0, SUBLANES)], sem.at[0])
                    cp.start() if op == "start" else cp.wait()
                    return carry

                lax.fori_loop(0, nzero_ref[e], body, 0)

            def tail(b, carry):
                row0 = pl.multiple_of(b * FFN_BLOCK, FFN_BLOCK)
                cp = pltpu.make_async_copy(zero_ref, xb_ref.at[pl.ds(row0, FFN_BLOCK)], sem.at[0])
                cp.start() if op == "start" else cp.wait()
                return carry

            lax.fori_loop(nreal_ref[0], xb_ref.shape[0] // FFN_BLOCK, tail, 0)


def _dispatch(hx, meta_t, tbl):
    n, width = hx.shape
    tm = TOKEN_TILE
    grid_spec = pltpu.PrefetchScalarGridSpec(
        num_scalar_prefetch=7,
        grid=(n // tm,),
        in_specs=[pl.BlockSpec((tm, width), lambda i, *_: (i, 0)),
                  pl.BlockSpec((1, SUBLANES, tm), lambda i, *_: (i, 0, 0))],
        out_specs=pl.BlockSpec(memory_space=pl.ANY),
        scratch_shapes=[pltpu.VMEM((2, SORTED_ROWS, width), F32),
                        pltpu.VMEM((FFN_BLOCK, width), F32),
                        pltpu.SemaphoreType.DMA((2,))],
    )
    return pl.pallas_call(
        _dispatch_kernel,
        grid_spec=grid_spec,
        out_shape=jax.ShapeDtypeStruct((tbl["cap"], width), F32),
        compiler_params=_params("arbitrary"),
        name="dispatch",
    )(tbl["dst0"], tbl["tile_len"], tbl["loc"], tbl["ncopy"], tbl["zero0"], tbl["nzero"], tbl["n_real"],
      hx, meta_t)


def _ffn_kernel(exp_ref, nreal_ref, x_ref, wgu_ref, bgu_ref, wd_ref, bd_ref, y_ref, wgu_bf, wd_bf):
    b = pl.program_id(0)
    d_ff, d = wd_ref.shape[1], wd_ref.shape[2]

    @pl.when(b >= nreal_ref[0])
    def _():
        y_ref[...] = jnp.zeros_like(y_ref)

    @pl.when((b == 0) | (exp_ref[b] != exp_ref[jnp.maximum(b - 1, 0)]))
    def _():
        wgu_bf[...] = wgu_ref[0].astype(BF16)
        wd_bf[...] = wd_ref[0].astype(BF16)

    @pl.when(b < nreal_ref[0])
    def _():
        x = x_ref[...]
        ext = x[:, d:]
        expert = exp_ref[b].astype(F32)
        gate = jnp.zeros((x.shape[0], 1), F32)
        for k in range(TOP_K):
            g0 = EXT_GATE + GATE_PARTS * k
            g = ext[:, g0:g0 + 1] + ext[:, g0 + 1:g0 + 2] + ext[:, g0 + 2:g0 + 3]
            gate = gate + jnp.where(ext[:, EXT_EXPERT + k:EXT_EXPERT + k + 1] == expert, g, 0.0)
        gu = _dot(x[:, :d].astype(BF16), wgu_bf[...]) + bgu_ref[0]
        g = jnp.minimum(gu[:, :d_ff], SWIGLU_LIMIT)
        u = jnp.clip(gu[:, d_ff:], -SWIGLU_LIMIT, SWIGLU_LIMIT)
        act = (u + 1.0) * (g * jax.nn.sigmoid(SWIGLU_ALPHA * g))
        y_ref[...] = gate * (_dot(act.astype(BF16), wd_bf[...]) + bd_ref[0])


def _ffn(xb, tbl, w_gate_up, b_gate_up, w_down, b_down):
    cap, width = xb.shape
    e, d, two_ff = w_gate_up.shape
    d_ff = two_ff // 2
    grid_spec = pltpu.PrefetchScalarGridSpec(
        num_scalar_prefetch=2,
        grid=(cap // FFN_BLOCK,),
        in_specs=[pl.BlockSpec((FFN_BLOCK, width), lambda b, ex, nr: (b, 0)),
                  pl.BlockSpec((1, d, two_ff), lambda b, ex, nr: (ex[b], 0, 0)),
                  pl.BlockSpec((1, 1, two_ff), lambda b, ex, nr: (ex[b], 0, 0)),
                  pl.BlockSpec((1, d_ff, d), lambda b, ex, nr: (ex[b], 0, 0)),
                  pl.BlockSpec((1, 1, d), lambda b, ex, nr: (ex[b], 0, 0))],
        out_specs=pl.BlockSpec((FFN_BLOCK, d), lambda b, ex, nr: (b, 0)),
        scratch_shapes=[pltpu.VMEM((d, two_ff), BF16), pltpu.VMEM((d_ff, d), BF16)],
    )
    return pl.pallas_call(
        _ffn_kernel,
        grid_spec=grid_spec,
        out_shape=jax.ShapeDtypeStruct((cap, d), F32),
        compiler_params=_params("arbitrary"),
        name="expert_ffn",
    )(tbl["blk_exp"], tbl["n_real"], xb, w_gate_up,
      b_gate_up.reshape(e, 1, two_ff), w_down, b_down.reshape(e, 1, d))


def _combine_kernel(dst0_ref, len_ref, loc_ref, ncopy_ref, meta_ref, h_ref, y_ref, o_ref, srt_ref, sem):
    t = pl.program_id(0)
    tm = h_ref.shape[0]

    @pl.when(t == 0)
    def _():
        srt_ref[...] = jnp.zeros_like(srt_ref)

    slot = t % 2

    def copy_into(s):
        def copy(seg_row, loc_row, rows):
            return pltpu.make_async_copy(y_ref.at[pl.ds(seg_row, rows)],
                                         srt_ref.at[s, pl.ds(loc_row, rows)], sem.at[s])
        return copy

    @pl.when(t == 0)
    def _():
        _start_copies(t, dst0_ref, len_ref, loc_ref, copy_into(slot), True)

    _wait_copies(t, ncopy_ref, copy_into(slot))
    last = pl.num_programs(0) - 1

    meta = meta_ref[...]
    rows = [meta[:, k:k + 1] for k in range(TOP_K)]
    part = SORTED_ROWS // SORT_PARTS

    def onehot_part(c):
        pcol = (lax.broadcasted_iota(jnp.int32, (tm, part), 1) + c * part).astype(F32)
        onehot = jnp.zeros((tm, part), F32)
        for k in range(TOP_K):
            onehot = jnp.where(pcol == rows[k], 1.0, onehot)
        return onehot.astype(BF16)

    _start_copies(jnp.minimum(t + 1, last), dst0_ref, len_ref, loc_ref, copy_into(1 - slot), t < last)
    onehots = [onehot_part(c) for c in range(SORT_PARTS)]
    out = h_ref[...]
    for c in range(SORT_PARTS):
        out = out + _dot(onehots[c], srt_ref[slot, c * part:(c + 1) * part].astype(BF16))
    o_ref[...] = out


def _combine(meta, tbl, h1, yb):
    n, d = h1.shape
    tm = TOKEN_TILE
    grid_spec = pltpu.PrefetchScalarGridSpec(
        num_scalar_prefetch=4,
        grid=(n // tm,),
        in_specs=[pl.BlockSpec((tm, LANES), lambda i, *_: (i, 0)),
                  pl.BlockSpec((tm, d), lambda i, *_: (i, 0)),
                  pl.BlockSpec(memory_space=pl.ANY)],
        out_specs=pl.BlockSpec((tm, d), lambda i, *_: (i, 0)),
        scratch_shapes=[pltpu.VMEM((2, SORTED_ROWS, d), F32), pltpu.SemaphoreType.DMA((2,))],
    )
    return pl.pallas_call(
        _combine_kernel,
        grid_spec=grid_spec,
        out_shape=jax.ShapeDtypeStruct((n, d), F32),
        compiler_params=_params("arbitrary"),
        name="combine",
    )(tbl["dst0"], tbl["tile_len"], tbl["loc"], tbl["ncopy"], meta, h1, yb)


def kernel(x, attn_norm_w, w_in, q_norm_w, k_norm_w, sinks, w_proj_swa, w_proj_sb, w_out,
           ffn_norm_w, router_w, router_b, w_gate_up, b_gate_up, w_down, b_down):
    batch, seq, d = x.shape
    n = batch * seq
    h = x.reshape(n, d)
    for layer in range(attn_norm_w.shape[0]):
        q_a, kv_a, q_b, k_b, v_b, g_a, g_b = _in_proj(h, attn_norm_w[layer], w_in[layer].astype(BF16))
        y_a = _swa(q_a, kv_a, q_norm_w[layer], k_norm_w[layer], sinks[layer], batch, seq)
        y_b = _sb(q_b, k_b, v_b, batch, seq)
        h1 = _merge(y_a, y_b, g_a, g_b, h, w_proj_swa[layer].astype(BF16),
                    w_proj_sb[layer].astype(BF16), w_out[layer].astype(BF16))
        hx, meta, meta_t, tile_run, counts = _router(h1, ffn_norm_w[layer], router_w[layer], router_b[layer])
        tbl = _routing_tables(tile_run, counts, n)
        xb = _dispatch(hx, meta_t, tbl)
        yb = _ffn(xb, tbl, w_gate_up[layer], b_gate_up[layer], w_down[layer], b_down[layer])
        h = _combine(meta, tbl, h1, yb)
    return h.reshape(batch, seq, d)
```

```python
import jax
import jax.numpy as jnp
from jax import lax
from jax.experimental import pallas as pl
from jax.experimental.pallas import tpu as pltpu

HEAD_DIM = 64
SWA_HEADS = 8
SWA_KV_HEADS = 2
SWA_GROUP = SWA_HEADS // SWA_KV_HEADS
BLOCK = 128
SB_HEADS = 8
SWA_WIDTH = SWA_HEADS * HEAD_DIM
SWA_KV_WIDTH = SWA_KV_HEADS * HEAD_DIM
SB_WIDTH = SB_HEADS * HEAD_DIM
N_EXPERTS = 32
TOP_K = 4
SWIGLU_LIMIT = 7.0
SWIGLU_ALPHA = 1.702
NORM_EPS = 1e-5
ATTN_SCALE = HEAD_DIM ** -0.5
SIGN_BIT = -2 ** 31

LANES = 128
SUBLANES = 8
DENSE_TILE = 1024
TOKEN_TILE = 512
FFN_BLOCK = 512
SORTED_ROWS = -(-(TOKEN_TILE * TOP_K + N_EXPERTS * (SUBLANES - 1)) // LANES) * LANES
SWA_BLOCKS = 8
SORT_PARTS = 3
SB_Q_ROWS = 1024
SB_K_ROWS = 256
VMEM_LIMIT = 56 * 1024 * 1024

F32 = jnp.float32
BF16 = jnp.bfloat16
NEG_BIG = -1e30


def _dot(a, b):
    return jnp.dot(a, b, preferred_element_type=F32)


def _dot_nt(a, b):
    return lax.dot_general(a, b, (((1,), (1,)), ((), ())), preferred_element_type=F32)


def _sigmoid(x):
    return jax.nn.sigmoid(x)


def _rms(x, w):
    return x * lax.rsqrt(jnp.mean(x * x, axis=-1, keepdims=True) + NORM_EPS) * w


def _params(*sem):
    return pltpu.CompilerParams(dimension_semantics=sem, vmem_limit_bytes=VMEM_LIMIT)


def _in_proj_kernel(x_ref, nw_ref, w_ref, qa_ref, kva_ref, qb_ref, kb_ref, vb_ref, ga_ref, gb_ref):
    xn = _rms(x_ref[...], nw_ref[...]).astype(BF16)
    off = 0
    for ref, gate in ((qa_ref, False), (kva_ref, False), (qb_ref, False), (kb_ref, False),
                      (vb_ref, False), (ga_ref, True), (gb_ref, True)):
        width = ref.shape[1]
        y = _dot(xn, w_ref[:, off:off + width])
        if gate:
            y = _sigmoid(y)
        ref[...] = y.astype(ref.dtype)
        off += width


def _in_proj(x2, norm_w, w_in):
    n, d = x2.shape
    widths = (SWA_WIDTH, 2 * SWA_KV_WIDTH, SB_WIDTH, SB_WIDTH, SB_WIDTH, d, d)
    tm = DENSE_TILE
    return pl.pallas_call(
        _in_proj_kernel,
        grid=(n // tm,),
        in_specs=[pl.BlockSpec((tm, d), lambda i: (i, 0)),
                  pl.BlockSpec((1, d), lambda i: (0, 0)),
                  pl.BlockSpec(w_in.shape, lambda i: (0, 0))],
        out_specs=[pl.BlockSpec((tm, w), lambda i: (i, 0)) for w in widths],
        out_shape=[jax.ShapeDtypeStruct((n, w), BF16) for w in widths],
        compiler_params=_params("parallel"),
        name="in_proj",
    )(x2, norm_w.reshape(1, d), w_in)


def _swa_kernel(q_ref, kvc_ref, kvp_ref, qnw_ref, knw_ref, sink_ref, o_ref):
    i = pl.program_id(1)
    half = HEAD_DIM

    def group_mean_sq(x):
        w = x.shape[1]
        same_head = (lax.broadcasted_iota(jnp.int32, (w, w), 0) // half
                     == lax.broadcasted_iota(jnp.int32, (w, w), 1) // half)
        avg = jnp.where(same_head, 1.0 / half, 0.0).astype(BF16)
        hi, lo = _split_bf16(x * x)
        return _dot(hi, avg) + _dot(lo, avg)

    q = q_ref[...].astype(F32)
    kv = jnp.concatenate([kvp_ref[...], kvc_ref[...]], axis=0)
    k = kv[:, :SWA_KV_WIDTH].astype(F32)
    v = kv[:, SWA_KV_WIDTH:].astype(F32)
    qn = (q * lax.rsqrt(group_mean_sq(q) + NORM_EPS) * qnw_ref[...] * ATTN_SCALE).astype(BF16)
    kn = k * lax.rsqrt(group_mean_sq(k) + NORM_EPS) * knw_ref[...]
    lane = lax.broadcasted_iota(jnp.int32, kn.shape, 1)
    low = lane < half
    k_swap = pltpu.roll(kn, half, 1)
    v_swap = pltpu.roll(v, half, 1).astype(BF16)
    v_same = v.astype(BF16)
    keys = [[jnp.where(low, kn, 0.0).astype(BF16), jnp.where(low, 0.0, k_swap).astype(BF16)],
            [jnp.where(low, k_swap, 0.0).astype(BF16), jnp.where(low, 0.0, kn).astype(BF16)]]
    vals = [[v_same, v_swap], [v_swap, v_same]]

    row = lax.broadcasted_iota(jnp.int32, (BLOCK, 2 * BLOCK), 0)
    col = lax.broadcasted_iota(jnp.int32, (BLOCK, 2 * BLOCK), 1)
    rel = row + BLOCK - col
    in_window = (rel >= 0) & (rel < BLOCK)
    rel = rel.astype(F32)
    out_lane_low = lax.broadcasted_iota(jnp.int32, (BLOCK, LANES), 1) < half
    for sub in range(SWA_BLOCKS):
        rows = slice(sub * BLOCK, (sub + 1) * BLOCK)
        band = slice(sub * BLOCK, (sub + 2) * BLOCK)
        valid = in_window & ((col >= BLOCK) | (i > 0)) if sub == 0 else in_window
        for blk in range(SWA_HEADS // 2):
            qb = qn[rows, blk * LANES:(blk + 1) * LANES]
            halves = []
            for par in range(2):
                h = 2 * blk + par
                g = h // SWA_GROUP
                slope = 2.0 ** (-8.0 * (h + 1) / SWA_HEADS)
                s = jnp.where(valid, _dot_nt(qb, keys[g][par][band]) - slope * rel, NEG_BIG)
                sink = sink_ref[h]
                m = jnp.maximum(jnp.max(s, axis=-1, keepdims=True), sink)
                p = jnp.exp(s - m)
                den = jnp.sum(p, axis=-1, keepdims=True) + jnp.exp(sink - m)
                halves.append(_dot(p.astype(BF16), vals[g][par][band]) / den)
            o_ref[rows, blk * LANES:(blk + 1) * LANES] = jnp.where(
                out_lane_low, halves[0], halves[1]).astype(o_ref.dtype)


def _swa(q_a, kv_a, q_norm_w, k_norm_w, sinks, batch, seq):
    n = q_a.shape[0]
    rows = SWA_BLOCKS * BLOCK
    nb = seq // rows
    return pl.pallas_call(
        _swa_kernel,
        grid=(batch, nb),
        in_specs=[pl.BlockSpec((rows, SWA_WIDTH), lambda b, i: (b * nb + i, 0)),
                  pl.BlockSpec((rows, 2 * SWA_KV_WIDTH), lambda b, i: (b * nb + i, 0)),
                  pl.BlockSpec((BLOCK, 2 * SWA_KV_WIDTH),
                               lambda b, i: (jnp.maximum((b * nb + i) * SWA_BLOCKS - 1, 0), 0)),
                  pl.BlockSpec((1, SWA_WIDTH), lambda b, i: (0, 0)),
                  pl.BlockSpec((1, SWA_KV_WIDTH), lambda b, i: (0, 0)),
                  pl.BlockSpec(memory_space=pltpu.SMEM)],
        out_specs=pl.BlockSpec((rows, SWA_WIDTH), lambda b, i: (b * nb + i, 0)),
        out_shape=jax.ShapeDtypeStruct((n, SWA_WIDTH), BF16),
        compiler_params=_params("parallel", "parallel"),
        name="swa",
    )(q_a, kv_a, kv_a, jnp.tile(q_norm_w, SWA_HEADS).reshape(1, SWA_WIDTH),
      jnp.tile(k_norm_w, SWA_KV_HEADS).reshape(1, SWA_KV_WIDTH), sinks)


def _split_bf16(x):
    hi = x.astype(BF16)
    lo = (x - hi.astype(F32)).astype(BF16)
    return hi, lo


def _sb_kernel(q_ref, k_ref, v_ref, o_ref):
    seq = q_ref.shape[0]
    tq, tk = SB_Q_ROWS, SB_K_ROWS
    nq = seq // tq
    k_per_q = tq // tk
    lane = lax.broadcasted_iota(jnp.int32, (tq, LANES), 1)
    row = lax.broadcasted_iota(jnp.int32, (tq, tk), 0)
    col = lax.broadcasted_iota(jnp.int32, (tq, tk), 1)
    neg_incl = jnp.where(lax.broadcasted_iota(jnp.int32, (tk, tk), 0)
                         >= lax.broadcasted_iota(jnp.int32, (tk, tk), 1), -1.0, 0.0).astype(BF16)

    def softplus(z):
        neg_abs = lax.bitcast_convert_type(lax.bitcast_convert_type(z, jnp.int32) | SIGN_BIT, F32)
        return jnp.maximum(z, 0.0) + jnp.log(1.0 + jnp.exp(neg_abs))

    def neg_suffix_sum(sp):
        return _dot(sp.astype(BF16), neg_incl)

    def add_from_row(x, r0, delta):
        return x + delta if r0 == 0 else jnp.concatenate([x[:r0], x[r0:] + delta], axis=0)

    def k_span(k0s, qms, state, masks, first_rows):
        pairs = [(i, p) for p in range(2) for i in range(len(k0s))]
        kbs = [k_ref[pl.ds(k0, tk), :] for k0 in k0s]
        vbs = [v_ref[pl.ds(k0, tk), :] for k0 in k0s]
        accs, runs = list(state[:2]), list(state[2:])
        zs, sps, cs, ws = {}, {}, {}, {}

        def stage(s, i, p):
            r0 = first_rows[i]
            if s == 0:
                zs[(i, p)] = _dot_nt(qms[p][r0:], kbs[i])
            elif s == 1:
                sp = softplus(zs[(i, p)])
                sps[(i, p)] = sp if masks[i] is None else jnp.where(masks[i], sp, 0.0)
            elif s == 2:
                cs[(i, p)] = neg_suffix_sum(sps[(i, p)])
            elif s == 3:
                w = jnp.exp(zs[(i, p)] + cs[(i, p)] + runs[p][r0:])
                ws[(i, p)] = w if masks[i] is None else jnp.where(masks[i], w, 0.0)
                runs[p] = add_from_row(runs[p], r0, cs[(i, p)][:, 0:1])
            else:
                accs[p] = add_from_row(accs[p], r0, _dot(ws[(i, p)].astype(BF16), vbs[i]))

        n_stages = 5
        for step in range(len(pairs) - 1 + n_stages):
            for n, (i, p) in enumerate(pairs):
                s = step - n
                if 0 <= s < n_stages:
                    stage(s, i, p)
        return (*accs, *runs)

    def q_block(qi, carry):
        q0 = pl.multiple_of(qi * tq, tq)
        q2 = q_ref[pl.ds(q0, tq), :] * ATTN_SCALE
        qms = (jnp.where(lane < HEAD_DIM, q2, 0.0).astype(BF16),
               jnp.where(lane >= HEAD_DIM, q2, 0.0).astype(BF16))
        state = (jnp.zeros((tq, LANES), F32), jnp.zeros((tq, LANES), F32),
                 jnp.zeros((tq, 1), F32), jnp.zeros((tq, 1), F32))
        order = list(reversed(range(k_per_q)))
        state = k_span([q0 + d * tk for d in order], qms, state,
                       [((col + d * tk) < row)[d * tk:] for d in order], [d * tk for d in order])

        def earlier(jj, state):
            k0s = [pl.multiple_of((qi - 1 - jj) * tq + d * tk, tk) for d in order]
            return k_span(k0s, qms, state, [None] * k_per_q, [0] * k_per_q)

        state = lax.fori_loop(0, qi, earlier, state)
        o_ref[pl.ds(q0, tq), :] = jnp.where(lane < HEAD_DIM, state[0], state[1]).astype(o_ref.dtype)
        return carry

    lax.fori_loop(0, nq, q_block, 0)


def _sb(q_b, k_b, v_b, batch, seq):
    n = q_b.shape[0]
    spec = pl.BlockSpec((seq, LANES), lambda b, p: (b, p))
    return pl.pallas_call(
        _sb_kernel,
        grid=(batch, SB_WIDTH // LANES),
        in_specs=[spec, spec, spec],
        out_specs=spec,
        out_shape=jax.ShapeDtypeStruct((n, SB_WIDTH), BF16),
        compiler_params=_params("parallel", "parallel"),
        name="stick_breaking",
    )(q_b, k_b, v_b)


def _merge_kernel(ya_ref, yb_ref, ga_ref, gb_ref, x_ref, pa_ref, pb_ref, wo_ref, h_ref):
    m = (ga_ref[...].astype(F32) * _dot(ya_ref[...], pa_ref[...])
         + gb_ref[...].astype(F32) * _dot(yb_ref[...], pb_ref[...]))
    h_ref[...] = x_ref[...] + _dot(m.astype(BF16), wo_ref[...])


def _merge(y_a, y_b, g_a, g_b, x2, p_a, p_b, w_o):
    n, d = x2.shape
    tm = DENSE_TILE
    row = lambda w: pl.BlockSpec((tm, w), lambda i: (i, 0))
    full = lambda a: pl.BlockSpec(a.shape, lambda i: (0, 0))
    return pl.pallas_call(
        _merge_kernel,
        grid=(n // tm,),
        in_specs=[row(SWA_WIDTH), row(SB_WIDTH), row(d), row(d), row(d), full(p_a), full(p_b), full(w_o)],
        out_specs=row(d),
        out_shape=jax.ShapeDtypeStruct((n, d), F32),
        compiler_params=_params("parallel"),
        name="merge",
    )(y_a, y_b, g_a, g_b, x2, p_a, p_b, w_o)


EXT_EXPERT = 0
EXT_GATE = TOP_K
GATE_PARTS = 3


def _router_kernel(h_ref, nw_ref, rw_ref, rb_ref, hx_ref, meta_ref, meta_t_ref, trun_ref, cnt_ref, run_ref):
    tm, d = h_ref.shape

    @pl.when(pl.program_id(0) == 0)
    def _():
        run_ref[...] = jnp.zeros_like(run_ref)

    hn = _rms(h_ref[...], nw_ref[...])
    hn_hi, hn_lo = _split_bf16(hn)
    both = _dot(hn_hi, rw_ref[...])
    logits = (both[:, :N_EXPERTS] + both[:, N_EXPERTS:] + _dot(hn_lo, rw_ref[:, :N_EXPERTS])
              + rb_ref[...])
    lane = lax.broadcasted_iota(jnp.int32, (tm, N_EXPERTS), 1).astype(F32)
    work = logits
    tops, idxs, hots = [], [], []
    for _ in range(TOP_K):
        m = jnp.max(work, axis=-1, keepdims=True)
        idx = jnp.min(jnp.where(work == m, lane, float(N_EXPERTS)), axis=-1, keepdims=True)
        hot = lane == idx
        work = jnp.where(hot, -jnp.inf, work)
        tops.append(m)
        idxs.append(idx)
        hots.append(hot)
    exps = [jnp.exp(t - tops[0]) for t in tops]
    den = exps[0] + exps[1] + exps[2] + exps[3]
    chosen = jnp.where(hots[0] | hots[1] | hots[2] | hots[3], 1.0, 0.0)
    r = lax.broadcasted_iota(jnp.int32, (tm, tm), 0)
    c = lax.broadcasted_iota(jnp.int32, (tm, tm), 1)
    before = jnp.where(c < r, 1.0, 0.0).astype(BF16)
    run = run_ref[...]
    tile_cnt = jnp.sum(chosen, axis=0, keepdims=True)
    units = jnp.ceil(tile_cnt / SUBLANES)
    earlier = jnp.where(lax.broadcasted_iota(jnp.int32, (N_EXPERTS, N_EXPERTS), 0)
                        < lax.broadcasted_iota(jnp.int32, (N_EXPERTS, N_EXPERTS), 1), 1.0, 0.0)
    group_row = SUBLANES * _dot(jnp.broadcast_to(units, (SUBLANES, N_EXPERTS)).astype(BF16),
                                earlier.astype(BF16))[0:1]
    place = _dot(before, chosen.astype(BF16)) + group_row
    mlane = lax.broadcasted_iota(jnp.int32, (tm, LANES), 1)
    meta = jnp.full((tm, LANES), -1.0, F32)
    ext = jnp.zeros((tm, LANES), F32)
    for k in range(TOP_K):
        row_k = jnp.sum(jnp.where(hots[k], place, 0.0), axis=-1, keepdims=True)
        meta = jnp.where(mlane == k, row_k, meta)
        ext = jnp.where(mlane == EXT_EXPERT + k, idxs[k], ext)
        rest = exps[k] / den
        for part in range(GATE_PARTS):
            piece = rest.astype(BF16).astype(F32)
            ext = jnp.where(mlane == EXT_GATE + GATE_PARTS * k + part, piece, ext)
            rest = rest - piece
    meta_ref[...] = meta
    meta_t_ref[0] = meta.T[:SUBLANES]
    hx_ref[:, :d] = hn_hi
    hx_ref[:, d:] = ext.astype(BF16)
    trun_ref[0] = run
    run = run + units * SUBLANES
    run_ref[...] = run
    cnt_ref[...] = run


def _router(h1, norm_w, router_w, router_b):
    n, d = h1.shape
    tm = TOKEN_TILE
    tiles = n // tm
    return pl.pallas_call(
        _router_kernel,
        grid=(tiles,),
        in_specs=[pl.BlockSpec((tm, d), lambda i: (i, 0)),
                  pl.BlockSpec((1, d), lambda i: (0, 0)),
                  pl.BlockSpec((d, 2 * N_EXPERTS), lambda i: (0, 0)),
                  pl.BlockSpec((1, N_EXPERTS), lambda i: (0, 0))],
        out_specs=[pl.BlockSpec((tm, d + LANES), lambda i: (i, 0)),
                   pl.BlockSpec((tm, LANES), lambda i: (i, 0)),
                   pl.BlockSpec((1, SUBLANES, tm), lambda i: (i, 0, 0)),
                   pl.BlockSpec((1, 1, N_EXPERTS), lambda i: (i, 0, 0)),
                   pl.BlockSpec((1, N_EXPERTS), lambda i: (0, 0))],
        out_shape=[jax.ShapeDtypeStruct((n, d + LANES), BF16),
                   jax.ShapeDtypeStruct((n, LANES), F32),
                   jax.ShapeDtypeStruct((tiles, SUBLANES, tm), F32),
                   jax.ShapeDtypeStruct((tiles, 1, N_EXPERTS), F32),
                   jax.ShapeDtypeStruct((1, N_EXPERTS), F32)],
        scratch_shapes=[pltpu.VMEM((1, N_EXPERTS), F32)],
        compiler_params=_params("arbitrary"),
        name="router",
    )(h1, norm_w.reshape(1, d), jnp.concatenate(_split_bf16(router_w), axis=1),
      router_b.reshape(1, N_EXPERTS))


def _routing_tables(tile_run, counts, n):
    tiles = tile_run.shape[0]
    counts = counts.reshape(N_EXPERTS).astype(jnp.int32)
    run = tile_run.reshape(tiles, N_EXPERTS).astype(jnp.int32)
    tile_len = jnp.concatenate([run[1:], counts[None]], axis=0) - run
    loc = jnp.cumsum(tile_len, axis=1) - tile_len
    padded = (counts + FFN_BLOCK - 1) // FFN_BLOCK * FFN_BLOCK
    seg_end = jnp.cumsum(padded)
    seg_start = seg_end - padded
    dst0 = seg_start[None, :] + run
    zero0 = seg_start + counts
    nzero = (padded - counts) // SUBLANES
    n_real = seg_end[-1] // FFN_BLOCK
    max_blocks = (n * TOP_K + tiles * N_EXPERTS * (SUBLANES - 1)
                  + N_EXPERTS * (FFN_BLOCK - 1)) // FFN_BLOCK
    blk_exp = jnp.minimum(jnp.sum(jnp.arange(max_blocks, dtype=jnp.int32)[:, None] * FFN_BLOCK
                                  >= seg_end[None, :], axis=1), N_EXPERTS - 1).astype(jnp.int32)
    ncopy = jnp.stack([jnp.sum((tile_len // rows) % 2, axis=1) for rows in _copy_sizes()],
                      axis=1).astype(jnp.int32)
    return dict(dst0=dst0.reshape(-1), tile_len=tile_len.reshape(-1), loc=loc.reshape(-1),
                ncopy=ncopy.reshape(-1), zero0=zero0, nzero=nzero, blk_exp=blk_exp,
                n_real=n_real.reshape(1).astype(jnp.int32), cap=max_blocks * FFN_BLOCK)


def _copy_sizes():
    sizes, rows = [], TOKEN_TILE
    while rows >= SUBLANES:
        sizes.append(rows)
        rows //= 2
    return sizes


def _wait_copies(t, ncopy_ref, make_copy):
    sizes = _copy_sizes()
    for c, rows in enumerate(sizes):
        def body(_, carry, rows=rows):
            make_copy(0, 0, rows).wait()
            return carry

        lax.fori_loop(0, ncopy_ref[t * len(sizes) + c], body, 0)


def _start_copies(t, tbl_ref, len_ref, loc_ref, make_copy, enabled):
    for e in range(N_EXPERTS):
        idx = t * N_EXPERTS + e
        seg_row = tbl_ref[idx]
        loc_row = loc_ref[idx]
        length = len_ref[idx]
        done = 0
        for rows in _copy_sizes():
            @pl.when(((length & rows) != 0) & enabled)
            def _(done=done, rows=rows):
                make_copy(pl.multiple_of(seg_row + done, SUBLANES),
                          pl.multiple_of(loc_row + done, SUBLANES), rows).start()
            done = done + (length & rows)


def _dispatch_kernel(dst0_ref, len_ref, loc_ref, ncopy_ref, zero0_ref, nzero_ref, nreal_ref,
                     hx_ref, rows_t_ref, xb_ref, srt_ref, zero_ref, sem):
    t = pl.program_id(0)
    tm = hx_ref.shape[0]
    rows_t = rows_t_ref[0]
    slot = t % 2

    def copy_from(s):
        def copy(seg_row, loc_row, rows):
            return pltpu.make_async_copy(srt_ref.at[s, pl.ds(loc_row, rows)],
                                         xb_ref.at[pl.ds(seg_row, rows)], sem.at[s])
        return copy

    _start_copies(jnp.maximum(t - 1, 0), dst0_ref, len_ref, loc_ref, copy_from(1 - slot), t > 0)
    part = SORTED_ROWS // SORT_PARTS
    for c in range(SORT_PARTS):
        prow = (lax.broadcasted_iota(jnp.int32, (part, tm), 0) + c * part).astype(F32)
        onehot = jnp.zeros((part, tm), F32)
        for k in range(TOP_K):
            onehot = jnp.where(prow == rows_t[k:k + 1, :], 1.0, onehot)
        srt_ref[slot, c * part:(c + 1) * part] = _dot(onehot.astype(BF16), hx_ref[...])

    @pl.when(t > 0)
    def _():
        _wait_copies(t - 1, ncopy_ref, copy_from(1 - slot))

    @pl.when(t == pl.num_programs(0) - 1)
    def _():
        _start_copies(t, dst0_ref, len_ref, loc_ref, copy_from(slot), True)
        _wait_copies(t, ncopy_ref, copy_from(slot))
        zero_ref[...] = jnp.zeros_like(zero_ref)
        for op in ("start", "wait"):
            for e in range(N_EXPERTS):
                def body(c, carry):
                    row0 = pl.multiple_of(zero0_ref[e] + c * SUBLANES, SUBLANES)
                    cp = pltpu.make_async_copy(zero_ref.at[pl.ds(0, SUBLANES)],
                                               xb_ref.at[pl.ds(row0, SUBLANES)], sem.at[0])
                    cp.start() if op == "start" else cp.wait()
                    return carry

                lax.fori_loop(0, nzero_ref[e], body, 0)

            def tail(b, carry):
                row0 = pl.multiple_of(b * FFN_BLOCK, FFN_BLOCK)
                cp = pltpu.make_async_copy(zero_ref, xb_ref.at[pl.ds(row0, FFN_BLOCK)], sem.at[0])
                cp.start() if op == "start" else cp.wait()
                return carry

            lax.fori_loop(nreal_ref[0], xb_ref.shape[0] // FFN_BLOCK, tail, 0)


def _dispatch(hx, meta_t, tbl):
    n, width = hx.shape
    tm = TOKEN_TILE
    grid_spec = pltpu.PrefetchScalarGridSpec(
        num_scalar_prefetch=7,
        grid=(n // tm,),
        in_specs=[pl.BlockSpec((tm, width), lambda i, *_: (i, 0)),
                  pl.BlockSpec((1, SUBLANES, tm), lambda i, *_: (i, 0, 0))],
        out_specs=pl.BlockSpec(memory_space=pl.ANY),
        scratch_shapes=[pltpu.VMEM((2, SORTED_ROWS, width), F32),
                        pltpu.VMEM((FFN_BLOCK, width), F32),
                        pltpu.SemaphoreType.DMA((2,))],
    )
    return pl.pallas_call(
        _dispatch_kernel,
        grid_spec=grid_spec,
        out_shape=jax.ShapeDtypeStruct((tbl["cap"], width), F32),
        compiler_params=_params("arbitrary"),
        name="dispatch",
    )(tbl["dst0"], tbl["tile_len"], tbl["loc"], tbl["ncopy"], tbl["zero0"], tbl["nzero"], tbl["n_real"],
      hx, meta_t)


def _ffn_kernel(exp_ref, nreal_ref, x_ref, wgu_ref, bgu_ref, wd_ref, bd_ref, y_ref, wgu_bf, wd_bf):
    b = pl.program_id(0)
    d_ff, d = wd_ref.shape[1], wd_ref.shape[2]

    @pl.when(b >= nreal_ref[0])
    def _():
        y_ref[...] = jnp.zeros_like(y_ref)

    @pl.when((b == 0) | (exp_ref[b] != exp_ref[jnp.maximum(b - 1, 0)]))
    def _():
        wgu_bf[...] = wgu_ref[0].astype(BF16)
        wd_bf[...] = wd_ref[0].astype(BF16)

    @pl.when(b < nreal_ref[0])
    def _():
        x = x_ref[...]
        ext = x[:, d:]
        expert = exp_ref[b].astype(F32)
        gate = jnp.zeros((x.shape[0], 1), F32)
        for k in range(TOP_K):
            g0 = EXT_GATE + GATE_PARTS * k
            g = ext[:, g0:g0 + 1] + ext[:, g0 + 1:g0 + 2] + ext[:, g0 + 2:g0 + 3]
            gate = gate + jnp.where(ext[:, EXT_EXPERT + k:EXT_EXPERT + k + 1] == expert, g, 0.0)
        gu = _dot(x[:, :d].astype(BF16), wgu_bf[...]) + bgu_ref[0]
        g = jnp.minimum(gu[:, :d_ff], SWIGLU_LIMIT)
        u = jnp.clip(gu[:, d_ff:], -SWIGLU_LIMIT, SWIGLU_LIMIT)
        act = (u + 1.0) * (g * jax.nn.sigmoid(SWIGLU_ALPHA * g))
        y_ref[...] = gate * (_dot(act.astype(BF16), wd_bf[...]) + bd_ref[0])


def _ffn(xb, tbl, w_gate_up, b_gate_up, w_down, b_down):
    cap, width = xb.shape
    e, d, two_ff = w_gate_up.shape
    d_ff = two_ff // 2
    grid_spec = pltpu.PrefetchScalarGridSpec(
        num_scalar_prefetch=2,
        grid=(cap // FFN_BLOCK,),
        in_specs=[pl.BlockSpec((FFN_BLOCK, width), lambda b, ex, nr: (b, 0)),
                  pl.BlockSpec((1, d, two_ff), lambda b, ex, nr: (ex[b], 0, 0)),
                  pl.BlockSpec((1, 1, two_ff), lambda b, ex, nr: (ex[b], 0, 0)),
                  pl.BlockSpec((1, d_ff, d), lambda b, ex, nr: (ex[b], 0, 0)),
                  pl.BlockSpec((1, 1, d), lambda b, ex, nr: (ex[b], 0, 0))],
        out_specs=pl.BlockSpec((FFN_BLOCK, d), lambda b, ex, nr: (b, 0)),
        scratch_shapes=[pltpu.VMEM((d, two_ff), BF16), pltpu.VMEM((d_ff, d), BF16)],
    )
    return pl.pallas_call(
        _ffn_kernel,
        grid_spec=grid_spec,
        out_shape=jax.ShapeDtypeStruct((cap, d), F32),
        compiler_params=_params("arbitrary"),
        name="expert_ffn",
    )(tbl["blk_exp"], tbl["n_real"], xb, w_gate_up,
      b_gate_up.reshape(e, 1, two_ff), w_down, b_down.reshape(e, 1, d))


def _combine_kernel(dst0_ref, len_ref, loc_ref, ncopy_ref, meta_ref, h_ref, y_ref, o_ref, srt_ref, sem):
    t = pl.program_id(0)
    tm = h_ref.shape[0]

    @pl.when(t == 0)
    def _():
        srt_ref[...] = jnp.zeros_like(srt_ref)

    slot = t % 2

    def copy_into(s):
        def copy(seg_row, loc_row, rows):
            return pltpu.make_async_copy(y_ref.at[pl.ds(seg_row, rows)],
                                         srt_ref.at[s, pl.ds(loc_row, rows)], sem.at[s])
        return copy

    @pl.when(t == 0)
    def _():
        _start_copies(t, dst0_ref, len_ref, loc_ref, copy_into(slot), True)

    _wait_copies(t, ncopy_ref, copy_into(slot))
    last = pl.num_programs(0) - 1

    meta = meta_ref[...]
    rows = [meta[:, k:k + 1] for k in range(TOP_K)]
    part = SORTED_ROWS // SORT_PARTS

    def onehot_part(c):
        pcol = (lax.broadcasted_iota(jnp.int32, (tm, part), 1) + c * part).astype(F32)
        onehot = jnp.zeros((tm, part), F32)
        for k in range(TOP_K):
            onehot = jnp.where(pcol == rows[k], 1.0, onehot)
        return onehot.astype(BF16)

    _start_copies(jnp.minimum(t + 1, last), dst0_ref, len_ref, loc_ref, copy_into(1 - slot), t < last)
    onehots = [onehot_part(c) for c in range(SORT_PARTS)]
    out = h_ref[...]
    for c in range(SORT_PARTS):
        out = out + _dot(onehots[c], srt_ref[slot, c * part:(c + 1) * part].astype(BF16))
    o_ref[...] = out


def _combine(meta, tbl, h1, yb):
    n, d = h1.shape
    tm = TOKEN_TILE
    grid_spec = pltpu.PrefetchScalarGridSpec(
        num_scalar_prefetch=4,
        grid=(n // tm,),
        in_specs=[pl.BlockSpec((tm, LANES), lambda i, *_: (i, 0)),
                  pl.BlockSpec((tm, d), lambda i, *_: (i, 0)),
                  pl.BlockSpec(memory_space=pl.ANY)],
        out_specs=pl.BlockSpec((tm, d), lambda i, *_: (i, 0)),
        scratch_shapes=[pltpu.VMEM((2, SORTED_ROWS, d), F32), pltpu.SemaphoreType.DMA((2,))],
    )
    return pl.pallas_call(
        _combine_kernel,
        grid_spec=grid_spec,
        out_shape=jax.ShapeDtypeStruct((n, d), F32),
        compiler_params=_params("arbitrary"),
        name="combine",
    )(tbl["dst0"], tbl["tile_len"], tbl["loc"], tbl["ncopy"], meta, h1, yb)


def kernel(x, attn_norm_w, w_in, q_norm_w, k_norm_w, sinks, w_proj_swa, w_proj_sb, w_out,
           ffn_norm_w, router_w, router_b, w_gate_up, b_gate_up, w_down, b_down):
    batch, seq, d = x.shape
    n = batch * seq
    h = x.reshape(n, d)
    for layer in range(attn_norm_w.shape[0]):
        q_a, kv_a, q_b, k_b, v_b, g_a, g_b = _in_proj(h, attn_norm_w[layer], w_in[layer].astype(BF16))
        y_a = _swa(q_a, kv_a, q_norm_w[layer], k_norm_w[layer], sinks[layer], batch, seq)
        y_b = _sb(q_b, k_b, v_b, batch, seq)
        h1 = _merge(y_a, y_b, g_a, g_b, h, w_proj_swa[layer].astype(BF16),
                    w_proj_sb[layer].astype(BF16), w_out[layer].astype(BF16))
        hx, meta, meta_t, tile_run, counts = _router(h1, ffn_norm_w[layer], router_w[layer], router_b[layer])
        tbl = _routing_tables(tile_run, counts, n)
        xb = _dispatch(hx, meta_t, tbl)
        yb = _ffn(xb, tbl, w_gate_up[layer], b_gate_up[layer], w_down[layer], b_down[layer])
        h = _combine(meta, tbl, h1, yb)
    return h.reshape(batch, seq, d)
```

```python
import jax
import jax.numpy as jnp
from jax import lax
from jax.experimental import pallas as pl
from jax.experimental.pallas import tpu as pltpu

HEAD_DIM = 64
SWA_HEADS = 8
SWA_KV_HEADS = 2
SWA_GROUP = SWA_HEADS // SWA_KV_HEADS
BLOCK = 128
SB_HEADS = 8
SWA_WIDTH = SWA_HEADS * HEAD_DIM
SWA_KV_WIDTH = SWA_KV_HEADS * HEAD_DIM
SB_WIDTH = SB_HEADS * HEAD_DIM
N_EXPERTS = 32
TOP_K = 4
SWIGLU_LIMIT = 7.0
SWIGLU_ALPHA = 1.702
NORM_EPS = 1e-5
ATTN_SCALE = HEAD_DIM ** -0.5
SIGN_BIT = -2 ** 31

LANES = 128
SUBLANES = 8
DENSE_TILE = 1024
TOKEN_TILE = 512
FFN_BLOCK = 512
SORTED_ROWS = -(-(TOKEN_TILE * TOP_K + N_EXPERTS * (SUBLANES - 1)) // LANES) * LANES
SWA_BLOCKS = 8
SORT_PARTS = 3
SB_Q_ROWS = 1024
SB_K_ROWS = 256
VMEM_LIMIT = 56 * 1024 * 1024

F32 = jnp.float32
BF16 = jnp.bfloat16
NEG_BIG = -1e30


def _dot(a, b):
    return jnp.dot(a, b, preferred_element_type=F32)


def _dot_nt(a, b):
    return lax.dot_general(a, b, (((1,), (1,)), ((), ())), preferred_element_type=F32)


def _sigmoid(x):
    return jax.nn.sigmoid(x)


def _rms(x, w):
    return x * lax.rsqrt(jnp.mean(x * x, axis=-1, keepdims=True) + NORM_EPS) * w


def _params(*sem):
    return pltpu.CompilerParams(dimension_semantics=sem, vmem_limit_bytes=VMEM_LIMIT)


def _in_proj_kernel(x_ref, nw_ref, w_ref, qa_ref, kva_ref, qb_ref, kb_ref, vb_ref, ga_ref, gb_ref):
    xn = _rms(x_ref[...], nw_ref[...]).astype(BF16)
    off = 0
    for ref, gate in ((qa_ref, False), (kva_ref, False), (qb_ref, False), (kb_ref, False),
                      (vb_ref, False), (ga_ref, True), (gb_ref, True)):
        width = ref.shape[1]
        y = _dot(xn, w_ref[:, off:off + width])
        if gate:
            y = _sigmoid(y)
        ref[...] = y.astype(ref.dtype)
        off += width


def _in_proj(x2, norm_w, w_in):
    n, d = x2.shape
    widths = (SWA_WIDTH, 2 * SWA_KV_WIDTH, SB_WIDTH, SB_WIDTH, SB_WIDTH, d, d)
    tm = DENSE_TILE
    return pl.pallas_call(
        _in_proj_kernel,
        grid=(n // tm,),
        in_specs=[pl.BlockSpec((tm, d), lambda i: (i, 0)),
                  pl.BlockSpec((1, d), lambda i: (0, 0)),
                  pl.BlockSpec(w_in.shape, lambda i: (0, 0))],
        out_specs=[pl.BlockSpec((tm, w), lambda i: (i, 0)) for w in widths],
        out_shape=[jax.ShapeDtypeStruct((n, w), BF16) for w in widths],
        compiler_params=_params("parallel"),
        name="in_proj",
    )(x2, norm_w.reshape(1, d), w_in)


def _swa_kernel(q_ref, kvc_ref, kvp_ref, qnw_ref, knw_ref, sink_ref, o_ref):
    i = pl.program_id(1)
    half = HEAD_DIM

    def group_mean_sq(x):
        w = x.shape[1]
        same_head = (lax.broadcasted_iota(jnp.int32, (w, w), 0) // half
                     == lax.broadcasted_iota(jnp.int32, (w, w), 1) // half)
        avg = jnp.where(same_head, 1.0 / half, 0.0).astype(BF16)
        hi, lo = _split_bf16(x * x)
        return _dot(hi, avg) + _dot(lo, avg)

    q = q_ref[...].astype(F32)
    kv = jnp.concatenate([kvp_ref[...], kvc_ref[...]], axis=0)
    k = kv[:, :SWA_KV_WIDTH].astype(F32)
    v = kv[:, SWA_KV_WIDTH:].astype(F32)
    qn = (q * lax.rsqrt(group_mean_sq(q) + NORM_EPS) * qnw_ref[...] * ATTN_SCALE).astype(BF16)
    kn = k * lax.rsqrt(group_mean_sq(k) + NORM_EPS) * knw_ref[...]
    lane = lax.broadcasted_iota(jnp.int32, kn.shape, 1)
    low = lane < half
    k_swap = pltpu.roll(kn, half, 1)
    v_swap = pltpu.roll(v, half, 1).astype(BF16)
    v_same = v.astype(BF16)
    keys = [[jnp.where(low, kn, 0.0).astype(BF16), jnp.where(low, 0.0, k_swap).astype(BF16)],
            [jnp.where(low, k_swap, 0.0).astype(BF16), jnp.where(low, 0.0, kn).astype(BF16)]]
    vals = [[v_same, v_swap], [v_swap, v_same]]

    row = lax.broadcasted_iota(jnp.int32, (BLOCK, 2 * BLOCK), 0)
    col = lax.broadcasted_iota(jnp.int32, (BLOCK, 2 * BLOCK), 1)
    rel = row + BLOCK - col
    in_window = (rel >= 0) & (rel < BLOCK)
    rel = rel.astype(F32)
    out_lane_low = lax.broadcasted_iota(jnp.int32, (BLOCK, LANES), 1) < half
    for sub in range(SWA_BLOCKS):
        rows = slice(sub * BLOCK, (sub + 1) * BLOCK)
        band = slice(sub * BLOCK, (sub + 2) * BLOCK)
        valid = in_window & ((col >= BLOCK) | (i > 0)) if sub == 0 else in_window
        for blk in range(SWA_HEADS // 2):
            qb = qn[rows, blk * LANES:(blk + 1) * LANES]
            halves = []
            for par in range(2):
                h = 2 * blk + par
                g = h // SWA_GROUP
                slope = 2.0 ** (-8.0 * (h + 1) / SWA_HEADS)
                s = jnp.where(valid, _dot_nt(qb, keys[g][par][band]) - slope * rel, NEG_BIG)
                sink = sink_ref[h]
                m = jnp.maximum(jnp.max(s, axis=-1, keepdims=True), sink)
                p = jnp.exp(s - m)
                den = jnp.sum(p, axis=-1, keepdims=True) + jnp.exp(sink - m)
                halves.append(_dot(p.astype(BF16), vals[g][par][band]) / den)
            o_ref[rows, blk * LANES:(blk + 1) * LANES] = jnp.where(
                out_lane_low, halves[0], halves[1]).astype(o_ref.dtype)


def _swa(q_a, kv_a, q_norm_w, k_norm_w, sinks, batch, seq):
    n = q_a.shape[0]
    rows = SWA_BLOCKS * BLOCK
    nb = seq // rows
    return pl.pallas_call(
        _swa_kernel,
        grid=(batch, nb),
        in_specs=[pl.BlockSpec((rows, SWA_WIDTH), lambda b, i: (b * nb + i, 0)),
                  pl.BlockSpec((rows, 2 * SWA_KV_WIDTH), lambda b, i: (b * nb + i, 0)),
                  pl.BlockSpec((BLOCK, 2 * SWA_KV_WIDTH),
                               lambda b, i: (jnp.maximum((b * nb + i) * SWA_BLOCKS - 1, 0), 0)),
                  pl.BlockSpec((1, SWA_WIDTH), lambda b, i: (0, 0)),
                  pl.BlockSpec((1, SWA_KV_WIDTH), lambda b, i: (0, 0)),
                  pl.BlockSpec(memory_space=pltpu.SMEM)],
        out_specs=pl.BlockSpec((rows, SWA_WIDTH), lambda b, i: (b * nb + i, 0)),
        out_shape=jax.ShapeDtypeStruct((n, SWA_WIDTH), BF16),
        compiler_params=_params("parallel", "parallel"),
        name="swa",
    )(q_a, kv_a, kv_a, jnp.tile(q_norm_w, SWA_HEADS).reshape(1, SWA_WIDTH),
      jnp.tile(k_norm_w, SWA_KV_HEADS).reshape(1, SWA_KV_WIDTH), sinks)


def _split_bf16(x):
    hi = x.astype(BF16)
    lo = (x - hi.astype(F32)).astype(BF16)
    return hi, lo


def _sb_kernel(q_ref, k_ref, v_ref, o_ref):
    seq = q_ref.shape[0]
    tq, tk = SB_Q_ROWS, SB_K_ROWS
    nq = seq // tq
    k_per_q = tq // tk
    lane = lax.broadcasted_iota(jnp.int32, (tq, LANES), 1)
    row = lax.broadcasted_iota(jnp.int32, (tq, tk), 0)
    col = lax.broadcasted_iota(jnp.int32, (tq, tk), 1)
    neg_later = jnp.where(lax.broadcasted_iota(jnp.int32, (tk, tk), 0)
                          > lax.broadcasted_iota(jnp.int32, (tk, tk), 1), -1.0, 0.0).astype(BF16)

    def softplus(z):
        neg_abs = lax.bitcast_convert_type(lax.bitcast_convert_type(z, jnp.int32) | SIGN_BIT, F32)
        return jnp.maximum(z, 0.0) + jnp.log(1.0 + jnp.exp(neg_abs))

    def neg_suffix_sum(sp):
        return _dot(sp.astype(BF16), neg_later)

    def add_from_row(x, r0, delta):
        return x + delta if r0 == 0 else jnp.concatenate([x[:r0], x[r0:] + delta], axis=0)

    def k_span(k0s, qms, state, masks, first_rows):
        pairs = [(i, p) for p in range(2) for i in range(len(k0s))]
        kbs = [k_ref[pl.ds(k0, tk), :] for k0 in k0s]
        vbs = [v_ref[pl.ds(k0, tk), :] for k0 in k0s]
        accs, runs = list(state[:2]), list(state[2:])
        zs, sps, cs, ws, firsts = {}, {}, {}, {}, {}

        def stage(s, i, p):
            r0 = first_rows[i]
            if s == 0:
                zs[(i, p)] = _dot_nt(qms[p][r0:], kbs[i])
            elif s == 1:
                sp = softplus(zs[(i, p)])
                zs[(i, p)] = zs[(i, p)] - sp
                sp = sp if masks[i] is None else jnp.where(masks[i], sp, 0.0)
                sps[(i, p)] = sp
                firsts[(i, p)] = sp[:, 0:1]
            elif s == 2:
                cs[(i, p)] = neg_suffix_sum(sps[(i, p)])
            elif s == 3:
                w = jnp.exp(zs[(i, p)] + cs[(i, p)] + runs[p][r0:])
                ws[(i, p)] = w if masks[i] is None else jnp.where(masks[i], w, 0.0)
                runs[p] = add_from_row(runs[p], r0, cs[(i, p)][:, 0:1] - firsts[(i, p)])
            else:
                accs[p] = add_from_row(accs[p], r0, _dot(ws[(i, p)].astype(BF16), vbs[i]))

        n_stages = 5
        for step in range(len(pairs) - 1 + n_stages):
            for n, (i, p) in enumerate(pairs):
                s = step - n
                if 0 <= s < n_stages:
                    stage(s, i, p)
        return (*accs, *runs)

    def q_block(qi, carry):
        q0 = pl.multiple_of(qi * tq, tq)
        q2 = q_ref[pl.ds(q0, tq), :] * ATTN_SCALE
        qms = (jnp.where(lane < HEAD_DIM, q2, 0.0).astype(BF16),
               jnp.where(lane >= HEAD_DIM, q2, 0.0).astype(BF16))
        state = (jnp.zeros((tq, LANES), F32), jnp.zeros((tq, LANES), F32),
                 jnp.zeros((tq, 1), F32), jnp.zeros((tq, 1), F32))
        order = list(reversed(range(k_per_q)))
        state = k_span([q0 + d * tk for d in order], qms, state,
                       [((col + d * tk) < row)[d * tk:] for d in order], [d * tk for d in order])

        def earlier(jj, state):
            k0s = [pl.multiple_of((qi - 1 - jj) * tq + d * tk, tk) for d in order]
            return k_span(k0s, qms, state, [None] * k_per_q, [0] * k_per_q)

        state = lax.fori_loop(0, qi, earlier, state)
        o_ref[pl.ds(q0, tq), :] = jnp.where(lane < HEAD_DIM, state[0], state[1]).astype(o_ref.dtype)
        return carry

    lax.fori_loop(0, nq, q_block, 0)


def _sb(q_b, k_b, v_b, batch, seq):
    n = q_b.shape[0]
    spec = pl.BlockSpec((seq, LANES), lambda b, p: (b, p))
    return pl.pallas_call(
        _sb_kernel,
        grid=(batch, SB_WIDTH // LANES),
        in_specs=[spec, spec, spec],
        out_specs=spec,
        out_shape=jax.ShapeDtypeStruct((n, SB_WIDTH), BF16),
        compiler_params=_params("parallel", "parallel"),
        name="stick_breaking",
    )(q_b, k_b, v_b)


def _merge_kernel(ya_ref, yb_ref, ga_ref, gb_ref, x_ref, pa_ref, pb_ref, wo_ref, h_ref):
    m = (ga_ref[...].astype(F32) * _dot(ya_ref[...], pa_ref[...])
         + gb_ref[...].astype(F32) * _dot(yb_ref[...], pb_ref[...]))
    h_ref[...] = x_ref[...] + _dot(m.astype(BF16), wo_ref[...])


def _merge(y_a, y_b, g_a, g_b, x2, p_a, p_b, w_o):
    n, d = x2.shape
    tm = DENSE_TILE
    row = lambda w: pl.BlockSpec((tm, w), lambda i: (i, 0))
    full = lambda a: pl.BlockSpec(a.shape, lambda i: (0, 0))
    return pl.pallas_call(
        _merge_kernel,
        grid=(n // tm,),
        in_specs=[row(SWA_WIDTH), row(SB_WIDTH), row(d), row(d), row(d), full(p_a), full(p_b), full(w_o)],
        out_specs=row(d),
        out_shape=jax.ShapeDtypeStruct((n, d), F32),
        compiler_params=_params("parallel"),
        name="merge",
    )(y_a, y_b, g_a, g_b, x2, p_a, p_b, w_o)


EXT_EXPERT = 0
EXT_GATE = TOP_K
GATE_PARTS = 3


def _router_kernel(h_ref, nw_ref, rw_ref, rb_ref, hx_ref, meta_ref, meta_t_ref, trun_ref, cnt_ref, run_ref):
    tm, d = h_ref.shape

    @pl.when(pl.program_id(0) == 0)
    def _():
        run_ref[...] = jnp.zeros_like(run_ref)

    hn = _rms(h_ref[...], nw_ref[...])
    hn_hi, hn_lo = _split_bf16(hn)
    both = _dot(hn_hi, rw_ref[...])
    logits = (both[:, :N_EXPERTS] + both[:, N_EXPERTS:] + _dot(hn_lo, rw_ref[:, :N_EXPERTS])
              + rb_ref[...])
    lane = lax.broadcasted_iota(jnp.int32, (tm, N_EXPERTS), 1).astype(F32)
    work = logits
    tops, idxs, hots = [], [], []
    for _ in range(TOP_K):
        m = jnp.max(work, axis=-1, keepdims=True)
        idx = jnp.min(jnp.where(work == m, lane, float(N_EXPERTS)), axis=-1, keepdims=True)
        hot = lane == idx
        work = jnp.where(hot, -jnp.inf, work)
        tops.append(m)
        idxs.append(idx)
        hots.append(hot)
    exps = [jnp.exp(t - tops[0]) for t in tops]
    den = exps[0] + exps[1] + exps[2] + exps[3]
    chosen = jnp.where(hots[0] | hots[1] | hots[2] | hots[3], 1.0, 0.0)
    r = lax.broadcasted_iota(jnp.int32, (tm, tm), 0)
    c = lax.broadcasted_iota(jnp.int32, (tm, tm), 1)
    before = jnp.where(c < r, 1.0, 0.0).astype(BF16)
    run = run_ref[...]
    tile_cnt = jnp.sum(chosen, axis=0, keepdims=True)
    units = jnp.ceil(tile_cnt / SUBLANES)
    earlier = jnp.where(lax.broadcasted_iota(jnp.int32, (N_EXPERTS, N_EXPERTS), 0)
                        < lax.broadcasted_iota(jnp.int32, (N_EXPERTS, N_EXPERTS), 1), 1.0, 0.0)
    group_row = SUBLANES * _dot(jnp.broadcast_to(units, (SUBLANES, N_EXPERTS)).astype(BF16),
                                earlier.astype(BF16))[0:1]
    place = _dot(before, chosen.astype(BF16)) + group_row
    mlane = lax.broadcasted_iota(jnp.int32, (tm, LANES), 1)
    meta = jnp.full((tm, LANES), -1.0, F32)
    ext = jnp.zeros((tm, LANES), F32)
    for k in range(TOP_K):
        row_k = jnp.sum(jnp.where(hots[k], place, 0.0), axis=-1, keepdims=True)
        meta = jnp.where(mlane == k, row_k, meta)
        ext = jnp.where(mlane == EXT_EXPERT + k, idxs[k], ext)
        rest = exps[k] / den
        for part in range(GATE_PARTS):
            piece = rest.astype(BF16).astype(F32)
            ext = jnp.where(mlane == EXT_GATE + GATE_PARTS * k + part, piece, ext)
            rest = rest - piece
    meta_ref[...] = meta
    meta_t_ref[0] = meta.T[:SUBLANES]
    hx_ref[:, :d] = hn_hi
    hx_ref[:, d:] = ext.astype(BF16)
    trun_ref[0] = run
    run = run + units * SUBLANES
    run_ref[...] = run
    cnt_ref[...] = run


def _router(h1, norm_w, router_w, router_b):
    n, d = h1.shape
    tm = TOKEN_TILE
    tiles = n // tm
    return pl.pallas_call(
        _router_kernel,
        grid=(tiles,),
        in_specs=[pl.BlockSpec((tm, d), lambda i: (i, 0)),
                  pl.BlockSpec((1, d), lambda i: (0, 0)),
                  pl.BlockSpec((d, 2 * N_EXPERTS), lambda i: (0, 0)),
                  pl.BlockSpec((1, N_EXPERTS), lambda i: (0, 0))],
        out_specs=[pl.BlockSpec((tm, d + LANES), lambda i: (i, 0)),
                   pl.BlockSpec((tm, LANES), lambda i: (i, 0)),
                   pl.BlockSpec((1, SUBLANES, tm), lambda i: (i, 0, 0)),
                   pl.BlockSpec((1, 1, N_EXPERTS), lambda i: (i, 0, 0)),
                   pl.BlockSpec((1, N_EXPERTS), lambda i: (0, 0))],
        out_shape=[jax.ShapeDtypeStruct((n, d + LANES), BF16),
                   jax.ShapeDtypeStruct((n, LANES), F32),
                   jax.ShapeDtypeStruct((tiles, SUBLANES, tm), F32),
                   jax.ShapeDtypeStruct((tiles, 1, N_EXPERTS), F32),
                   jax.ShapeDtypeStruct((1, N_EXPERTS), F32)],
        scratch_shapes=[pltpu.VMEM((1, N_EXPERTS), F32)],
        compiler_params=_params("arbitrary"),
        name="router",
    )(h1, norm_w.reshape(1, d), jnp.concatenate(_split_bf16(router_w), axis=1),
      router_b.reshape(1, N_EXPERTS))


def _routing_tables(tile_run, counts, n):
    tiles = tile_run.shape[0]
    counts = counts.reshape(N_EXPERTS).astype(jnp.int32)
    run = tile_run.reshape(tiles, N_EXPERTS).astype(jnp.int32)
    tile_len = jnp.concatenate([run[1:], counts[None]], axis=0) - run
    loc = jnp.cumsum(tile_len, axis=1) - tile_len
    padded = (counts + FFN_BLOCK - 1) // FFN_BLOCK * FFN_BLOCK
    seg_end = jnp.cumsum(padded)
    seg_start = seg_end - padded
    dst0 = seg_start[None, :] + run
    zero0 = seg_start + counts
    nzero = (padded - counts) // SUBLANES
    n_real = seg_end[-1] // FFN_BLOCK
    max_blocks = (n * TOP_K + tiles * N_EXPERTS * (SUBLANES - 1)
                  + N_EXPERTS * (FFN_BLOCK - 1)) // FFN_BLOCK
    blk_exp = jnp.minimum(jnp.sum(jnp.arange(max_blocks, dtype=jnp.int32)[:, None] * FFN_BLOCK
                                  >= seg_end[None, :], axis=1), N_EXPERTS - 1).astype(jnp.int32)
    ncopy = jnp.stack([jnp.sum((tile_len // rows) % 2, axis=1) for rows in _copy_sizes()],
                      axis=1).astype(jnp.int32)
    return dict(dst0=dst0.reshape(-1), tile_len=tile_len.reshape(-1), loc=loc.reshape(-1),
                ncopy=ncopy.reshape(-1), zero0=zero0, nzero=nzero, blk_exp=blk_exp,
                n_real=n_real.reshape(1).astype(jnp.int32), cap=max_blocks * FFN_BLOCK)


def _copy_sizes():
    sizes, rows = [], TOKEN_TILE
    while rows >= SUBLANES:
        sizes.append(rows)
        rows //= 2
    return sizes


def _wait_copies(t, ncopy_ref, make_copy):
    sizes = _copy_sizes()
    for c, rows in enumerate(sizes):
        def body(_, carry, rows=rows):
            make_copy(0, 0, rows).wait()
            return carry

        lax.fori_loop(0, ncopy_ref[t * len(sizes) + c], body, 0)


def _start_copies(t, tbl_ref, len_ref, loc_ref, make_copy, enabled):
    for e in range(N_EXPERTS):
        idx = t * N_EXPERTS + e
        seg_row = tbl_ref[idx]
        loc_row = loc_ref[idx]
        length = len_ref[idx]
        done = 0
        for rows in _copy_sizes():
            @pl.when(((length & rows) != 0) & enabled)
            def _(done=done, rows=rows):
                make_copy(pl.multiple_of(seg_row + done, SUBLANES),
                          pl.multiple_of(loc_row + done, SUBLANES), rows).start()
            done = done + (length & rows)


def _dispatch_kernel(dst0_ref, len_ref, loc_ref, ncopy_ref, zero0_ref, nzero_ref, nreal_ref,
                     hx_ref, rows_t_ref, xb_ref, srt_ref, zero_ref, sem):
    t = pl.program_id(0)
    tm = hx_ref.shape[0]
    rows_t = rows_t_ref[0]
    slot = t % 2

    def copy_from(s):
        def copy(seg_row, loc_row, rows):
            return pltpu.make_async_copy(srt_ref.at[s, pl.ds(loc_row, rows)],
                                         xb_ref.at[pl.ds(seg_row, rows)], sem.at[s])
        return copy

    _start_copies(jnp.maximum(t - 1, 0), dst0_ref, len_ref, loc_ref, copy_from(1 - slot), t > 0)
    part = SORTED_ROWS // SORT_PARTS
    for c in range(SORT_PARTS):
        prow = (lax.broadcasted_iota(jnp.int32, (part, tm), 0) + c * part).astype(F32)
        onehot = jnp.zeros((part, tm), F32)
        for k in range(TOP_K):
            onehot = jnp.where(prow == rows_t[k:k + 1, :], 1.0, onehot)
        srt_ref[slot, c * part:(c + 1) * part] = _dot(onehot.astype(BF16), hx_ref[...])

    @pl.when(t > 0)
    def _():
        _wait_copies(t - 1, ncopy_ref, copy_from(1 - slot))

    @pl.when(t == pl.num_programs(0) - 1)
    def _():
        _start_copies(t, dst0_ref, len_ref, loc_ref, copy_from(slot), True)
        _wait_copies(t, ncopy_ref, copy_from(slot))
        zero_ref[...] = jnp.zeros_like(zero_ref)
        for op in ("start", "wait"):
            for e in range(N_EXPERTS):
                def body(c, carry):
                    row0 = pl.multiple_of(zero0_ref[e] + c * SUBLANES, SUBLANES)
                    cp = pltpu.make_async_copy(zero_ref.at[pl.ds(0, SUBLANES)],
                                               xb_ref.at[pl.ds(row0, SUBLANES)], sem.at[0])
                    cp.start() if op == "start" else cp.wait()
                    return carry

                lax.fori_loop(0, nzero_ref[e], body, 0)

            def tail(b, carry):
                row0 = pl.multiple_of(b * FFN_BLOCK, FFN_BLOCK)
                cp = pltpu.make_async_copy(zero_ref, xb_ref.at[pl.ds(row0, FFN_BLOCK)], sem.at[0])
                cp.start() if op == "start" else cp.wait()
                return carry

            lax.fori_loop(nreal_ref[0], xb_ref.shape[0] // FFN_BLOCK, tail, 0)


def _dispatch(hx, meta_t, tbl):
    n, width = hx.shape
    tm = TOKEN_TILE
    grid_spec = pltpu.PrefetchScalarGridSpec(
        num_scalar_prefetch=7,
        grid=(n // tm,),
        in_specs=[pl.BlockSpec((tm, width), lambda i, *_: (i, 0)),
                  pl.BlockSpec((1, SUBLANES, tm), lambda i, *_: (i, 0, 0))],
        out_specs=pl.BlockSpec(memory_space=pl.ANY),
        scratch_shapes=[pltpu.VMEM((2, SORTED_ROWS, width), F32),
                        pltpu.VMEM((FFN_BLOCK, width), F32),
                        pltpu.SemaphoreType.DMA((2,))],
    )
    return pl.pallas_call(
        _dispatch_kernel,
        grid_spec=grid_spec,
        out_shape=jax.ShapeDtypeStruct((tbl["cap"], width), F32),
        compiler_params=_params("arbitrary"),
        name="dispatch",
    )(tbl["dst0"], tbl["tile_len"], tbl["loc"], tbl["ncopy"], tbl["zero0"], tbl["nzero"], tbl["n_real"],
      hx, meta_t)


def _ffn_kernel(exp_ref, nreal_ref, x_ref, wgu_ref, bgu_ref, wd_ref, bd_ref, y_ref, wgu_bf, wd_bf):
    b = pl.program_id(0)
    d_ff, d = wd_ref.shape[1], wd_ref.shape[2]

    @pl.when(b >= nreal_ref[0])
    def _():
        y_ref[...] = jnp.zeros_like(y_ref)

    @pl.when((b == 0) | (exp_ref[b] != exp_ref[jnp.maximum(b - 1, 0)]))
    def _():
        wgu_bf[...] = wgu_ref[0].astype(BF16)
        wd_bf[...] = wd_ref[0].astype(BF16)

    @pl.when(b < nreal_ref[0])
    def _():
        x = x_ref[...]
        ext = x[:, d:]
        expert = exp_ref[b].astype(F32)
        gate = jnp.zeros((x.shape[0], 1), F32)
        for k in range(TOP_K):
            g0 = EXT_GATE + GATE_PARTS * k
            g = ext[:, g0:g0 + 1] + ext[:, g0 + 1:g0 + 2] + ext[:, g0 + 2:g0 + 3]
            gate = gate + jnp.where(ext[:, EXT_EXPERT + k:EXT_EXPERT + k + 1] == expert, g, 0.0)
        gu = _dot(x[:, :d].astype(BF16), wgu_bf[...]) + bgu_ref[0]
        g = jnp.minimum(gu[:, :d_ff], SWIGLU_LIMIT)
        u = jnp.clip(gu[:, d_ff:], -SWIGLU_LIMIT, SWIGLU_LIMIT)
        act = (u + 1.0) * (g * jax.nn.sigmoid(SWIGLU_ALPHA * g))
        y_ref[...] = gate * (_dot(act.astype(BF16), wd_bf[...]) + bd_ref[0])


def _ffn(xb, tbl, w_gate_up, b_gate_up, w_down, b_down):
    cap, width = xb.shape
    e, d, two_ff = w_gate_up.shape
    d_ff = two_ff // 2
    grid_spec = pltpu.PrefetchScalarGridSpec(
        num_scalar_prefetch=2,
        grid=(cap // FFN_BLOCK,),
        in_specs=[pl.BlockSpec((FFN_BLOCK, width), lambda b, ex, nr: (b, 0)),
                  pl.BlockSpec((1, d, two_ff), lambda b, ex, nr: (ex[b], 0, 0)),
                  pl.BlockSpec((1, 1, two_ff), lambda b, ex, nr: (ex[b], 0, 0)),
                  pl.BlockSpec((1, d_ff, d), lambda b, ex, nr: (ex[b], 0, 0)),
                  pl.BlockSpec((1, 1, d), lambda b, ex, nr: (ex[b], 0, 0))],
        out_specs=pl.BlockSpec((FFN_BLOCK, d), lambda b, ex, nr: (b, 0)),
        scratch_shapes=[pltpu.VMEM((d, two_ff), BF16), pltpu.VMEM((d_ff, d), BF16)],
    )
    return pl.pallas_call(
        _ffn_kernel,
        grid_spec=grid_spec,
        out_shape=jax.ShapeDtypeStruct((cap, d), F32),
        compiler_params=_params("arbitrary"),
        name="expert_ffn",
    )(tbl["blk_exp"], tbl["n_real"], xb, w_gate_up,
      b_gate_up.reshape(e, 1, two_ff), w_down, b_down.reshape(e, 1, d))


def _combine_kernel(dst0_ref, len_ref, loc_ref, ncopy_ref, meta_ref, h_ref, y_ref, o_ref, srt_ref, sem):
    t = pl.program_id(0)
    tm = h_ref.shape[0]

    @pl.when(t == 0)
    def _():
        srt_ref[...] = jnp.zeros_like(srt_ref)

    slot = t % 2

    def copy_into(s):
        def copy(seg_row, loc_row, rows):
            return pltpu.make_async_copy(y_ref.at[pl.ds(seg_row, rows)],
                                         srt_ref.at[s, pl.ds(loc_row, rows)], sem.at[s])
        return copy

    @pl.when(t == 0)
    def _():
        _start_copies(t, dst0_ref, len_ref, loc_ref, copy_into(slot), True)

    _wait_copies(t, ncopy_ref, copy_into(slot))
    last = pl.num_programs(0) - 1

    meta = meta_ref[...]
    rows = [meta[:, k:k + 1] for k in range(TOP_K)]
    part = SORTED_ROWS // SORT_PARTS

    def onehot_part(c):
        pcol = (lax.broadcasted_iota(jnp.int32, (tm, part), 1) + c * part).astype(F32)
        onehot = jnp.zeros((tm, part), F32)
        for k in range(TOP_K):
            onehot = jnp.where(pcol == rows[k], 1.0, onehot)
        return onehot.astype(BF16)

    _start_copies(jnp.minimum(t + 1, last), dst0_ref, len_ref, loc_ref, copy_into(1 - slot), t < last)
    onehots = [onehot_part(c) for c in range(SORT_PARTS)]
    out = h_ref[...]
    for c in range(SORT_PARTS):
        out = out + _dot(onehots[c], srt_ref[slot, c * part:(c + 1) * part].astype(BF16))
    o_ref[...] = out


def _combine(meta, tbl, h1, yb):
    n, d = h1.shape
    tm = TOKEN_TILE
    grid_spec = pltpu.PrefetchScalarGridSpec(
        num_scalar_prefetch=4,
        grid=(n // tm,),
        in_specs=[pl.BlockSpec((tm, LANES), lambda i, *_: (i, 0)),
                  pl.BlockSpec((tm, d), lambda i, *_: (i, 0)),
                  pl.BlockSpec(memory_space=pl.ANY)],
        out_specs=pl.BlockSpec((tm, d), lambda i, *_: (i, 0)),
        scratch_shapes=[pltpu.VMEM((2, SORTED_ROWS, d), F32), pltpu.SemaphoreType.DMA((2,))],
    )
    return pl.pallas_call(
        _combine_kernel,
        grid_spec=grid_spec,
        out_shape=jax.ShapeDtypeStruct((n, d), F32),
        compiler_params=_params("arbitrary"),
        name="combine",
    )(tbl["dst0"], tbl["tile_len"], tbl["loc"], tbl["ncopy"], meta, h1, yb)


def kernel(x, attn_norm_w, w_in, q_norm_w, k_norm_w, sinks, w_proj_swa, w_proj_sb, w_out,
           ffn_norm_w, router_w, router_b, w_gate_up, b_gate_up, w_down, b_down):
    batch, seq, d = x.shape
    n = batch * seq
    h = x.reshape(n, d)
    for layer in range(attn_norm_w.shape[0]):
        q_a, kv_a, q_b, k_b, v_b, g_a, g_b = _in_proj(h, attn_norm_w[layer], w_in[layer].astype(BF16))
        y_a = _swa(q_a, kv_a, q_norm_w[layer], k_norm_w[layer], sinks[layer], batch, seq)
        y_b = _sb(q_b, k_b, v_b, batch, seq)
        h1 = _merge(y_a, y_b, g_a, g_b, h, w_proj_swa[layer].astype(BF16),
                    w_proj_sb[layer].astype(BF16), w_out[layer].astype(BF16))
        hx, meta, meta_t, tile_run, counts = _router(h1, ffn_norm_w[layer], router_w[layer], router_b[layer])
        tbl = _routing_tables(tile_run, counts, n)
        xb = _dispatch(hx, meta_t, tbl)
        yb = _ffn(xb, tbl, w_gate_up[layer], b_gate_up[layer], w_down[layer], b_down[layer])
        h = _combine(meta, tbl, h1, yb)
    return h.reshape(batch, seq, d)
```

```python
import jax
import jax.numpy as jnp
from jax import lax
from jax.experimental import pallas as pl
from jax.experimental.pallas import tpu as pltpu

HEAD_DIM = 64
SWA_HEADS = 8
SWA_KV_HEADS = 2
SWA_GROUP = SWA_HEADS // SWA_KV_HEADS
BLOCK = 128
SB_HEADS = 8
SWA_WIDTH = SWA_HEADS * HEAD_DIM
SWA_KV_WIDTH = SWA_KV_HEADS * HEAD_DIM
SB_WIDTH = SB_HEADS * HEAD_DIM
N_EXPERTS = 32
TOP_K = 4
SWIGLU_LIMIT = 7.0
SWIGLU_ALPHA = 1.702
NORM_EPS = 1e-5
ATTN_SCALE = HEAD_DIM ** -0.5
SIGN_BIT = -2 ** 31

LANES = 128
SUBLANES = 8
DENSE_TILE = 1024
TOKEN_TILE = 512
FFN_BLOCK = 512
SORTED_ROWS = -(-(TOKEN_TILE * TOP_K + N_EXPERTS * (SUBLANES - 1)) // LANES) * LANES
SWA_BLOCKS = 8
SORT_PARTS = 3
SB_Q_ROWS = 1024
SB_K_ROWS = 256
VMEM_LIMIT = 56 * 1024 * 1024

F32 = jnp.float32
BF16 = jnp.bfloat16
NEG_BIG = -1e30


def _dot(a, b):
    return jnp.dot(a, b, preferred_element_type=F32)


def _dot_nt(a, b):
    return lax.dot_general(a, b, (((1,), (1,)), ((), ())), preferred_element_type=F32)


def _sigmoid(x):
    return jax.nn.sigmoid(x)


def _rms(x, w):
    return x * lax.rsqrt(jnp.mean(x * x, axis=-1, keepdims=True) + NORM_EPS) * w


def _params(*sem):
    return pltpu.CompilerParams(dimension_semantics=sem, vmem_limit_bytes=VMEM_LIMIT)


def _in_proj_kernel(x_ref, nw_ref, w_ref, qa_ref, kva_ref, qb_ref, kb_ref, vb_ref, ga_ref, gb_ref):
    xn = _rms(x_ref[...], nw_ref[...]).astype(BF16)
    off = 0
    for ref, gate in ((qa_ref, False), (kva_ref, False), (qb_ref, False), (kb_ref, False),
                      (vb_ref, False), (ga_ref, True), (gb_ref, True)):
        width = ref.shape[1]
        y = _dot(xn, w_ref[:, off:off + width])
        if gate:
            y = _sigmoid(y)
        ref[...] = y.astype(ref.dtype)
        off += width


def _in_proj(x2, norm_w, w_in):
    n, d = x2.shape
    widths = (SWA_WIDTH, 2 * SWA_KV_WIDTH, SB_WIDTH, SB_WIDTH, SB_WIDTH, d, d)
    tm = DENSE_TILE
    return pl.pallas_call(
        _in_proj_kernel,
        grid=(n // tm,),
        in_specs=[pl.BlockSpec((tm, d), lambda i: (i, 0)),
                  pl.BlockSpec((1, d), lambda i: (0, 0)),
                  pl.BlockSpec(w_in.shape, lambda i: (0, 0))],
        out_specs=[pl.BlockSpec((tm, w), lambda i: (i, 0)) for w in widths],
        out_shape=[jax.ShapeDtypeStruct((n, w), BF16) for w in widths],
        compiler_params=_params("parallel"),
        name="in_proj",
    )(x2, norm_w.reshape(1, d), w_in)


def _swa_kernel(q_ref, kvc_ref, kvp_ref, qnw_ref, knw_ref, sink_ref, o_ref):
    i = pl.program_id(1)
    half = HEAD_DIM

    def group_mean_sq(x):
        w = x.shape[1]
        same_head = (lax.broadcasted_iota(jnp.int32, (w, w), 0) // half
                     == lax.broadcasted_iota(jnp.int32, (w, w), 1) // half)
        avg = jnp.where(same_head, 1.0 / half, 0.0).astype(BF16)
        hi, lo = _split_bf16(x * x)
        return _dot(hi, avg) + _dot(lo, avg)

    q = q_ref[...].astype(F32)
    kv = jnp.concatenate([kvp_ref[...], kvc_ref[...]], axis=0)
    k = kv[:, :SWA_KV_WIDTH].astype(F32)
    v = kv[:, SWA_KV_WIDTH:].astype(F32)
    qn = (q * lax.rsqrt(group_mean_sq(q) + NORM_EPS) * qnw_ref[...] * ATTN_SCALE).astype(BF16)
    kn = k * lax.rsqrt(group_mean_sq(k) + NORM_EPS) * knw_ref[...]
    lane = lax.broadcasted_iota(jnp.int32, kn.shape, 1)
    low = lane < half
    k_swap = pltpu.roll(kn, half, 1)
    v_swap = pltpu.roll(v, half, 1).astype(BF16)
    v_same = v.astype(BF16)
    keys = [[jnp.where(low, kn, 0.0).astype(BF16), jnp.where(low, 0.0, k_swap).astype(BF16)],
            [jnp.where(low, k_swap, 0.0).astype(BF16), jnp.where(low, 0.0, kn).astype(BF16)]]
    vals = [[v_same, v_swap], [v_swap, v_same]]

    row = lax.broadcasted_iota(jnp.int32, (BLOCK, 2 * BLOCK), 0)
    col = lax.broadcasted_iota(jnp.int32, (BLOCK, 2 * BLOCK), 1)
    rel = row + BLOCK - col
    in_window = (rel >= 0) & (rel < BLOCK)
    rel = rel.astype(F32)
    out_lane_low = lax.broadcasted_iota(jnp.int32, (BLOCK, LANES), 1) < half
    for sub in range(SWA_BLOCKS):
        rows = slice(sub * BLOCK, (sub + 1) * BLOCK)
        band = slice(sub * BLOCK, (sub + 2) * BLOCK)
        valid = in_window & ((col >= BLOCK) | (i > 0)) if sub == 0 else in_window
        for blk in range(SWA_HEADS // 2):
            qb = qn[rows, blk * LANES:(blk + 1) * LANES]
            halves = []
            for par in range(2):
                h = 2 * blk + par
                g = h // SWA_GROUP
                slope = 2.0 ** (-8.0 * (h + 1) / SWA_HEADS)
                s = jnp.where(valid, _dot_nt(qb, keys[g][par][band]) - slope * rel, NEG_BIG)
                sink = sink_ref[h]
                m = jnp.maximum(jnp.max(s, axis=-1, keepdims=True), sink)
                p = jnp.exp(s - m)
                den = jnp.sum(p, axis=-1, keepdims=True) + jnp.exp(sink - m)
                halves.append(_dot(p.astype(BF16), vals[g][par][band]) / den)
            o_ref[rows, blk * LANES:(blk + 1) * LANES] = jnp.where(
                out_lane_low, halves[0], halves[1]).astype(o_ref.dtype)


def _swa(q_a, kv_a, q_norm_w, k_norm_w, sinks, batch, seq):
    n = q_a.shape[0]
    rows = SWA_BLOCKS * BLOCK
    nb = seq // rows
    return pl.pallas_call(
        _swa_kernel,
        grid=(batch, nb),
        in_specs=[pl.BlockSpec((rows, SWA_WIDTH), lambda b, i: (b * nb + i, 0)),
                  pl.BlockSpec((rows, 2 * SWA_KV_WIDTH), lambda b, i: (b * nb + i, 0)),
                  pl.BlockSpec((BLOCK, 2 * SWA_KV_WIDTH),
                               lambda b, i: (jnp.maximum((b * nb + i) * SWA_BLOCKS - 1, 0), 0)),
                  pl.BlockSpec((1, SWA_WIDTH), lambda b, i: (0, 0)),
                  pl.BlockSpec((1, SWA_KV_WIDTH), lambda b, i: (0, 0)),
                  pl.BlockSpec(memory_space=pltpu.SMEM)],
        out_specs=pl.BlockSpec((rows, SWA_WIDTH), lambda b, i: (b * nb + i, 0)),
        out_shape=jax.ShapeDtypeStruct((n, SWA_WIDTH), BF16),
        compiler_params=_params("parallel", "parallel"),
        name="swa",
    )(q_a, kv_a, kv_a, jnp.tile(q_norm_w, SWA_HEADS).reshape(1, SWA_WIDTH),
      jnp.tile(k_norm_w, SWA_KV_HEADS).reshape(1, SWA_KV_WIDTH), sinks)


def _split_bf16(x):
    hi = x.astype(BF16)
    lo = (x - hi.astype(F32)).astype(BF16)
    return hi, lo


def _sb_kernel(q_ref, k_ref, v_ref, o_ref):
    seq = q_ref.shape[0]
    tq, tk = SB_Q_ROWS, SB_K_ROWS
    nq = seq // tq
    k_per_q = tq // tk
    lane = lax.broadcasted_iota(jnp.int32, (tq, LANES), 1)
    row = lax.broadcasted_iota(jnp.int32, (tq, tk), 0)
    col = lax.broadcasted_iota(jnp.int32, (tq, tk), 1)
    neg_later = jnp.where(lax.broadcasted_iota(jnp.int32, (tk, tk), 0)
                          > lax.broadcasted_iota(jnp.int32, (tk, tk), 1), -1.0, 0.0).astype(BF16)

    def softplus(z):
        neg_abs = lax.bitcast_convert_type(lax.bitcast_convert_type(z, jnp.int32) | SIGN_BIT, F32)
        return jnp.maximum(z, 0.0) + jnp.log(1.0 + jnp.exp(neg_abs))

    def neg_suffix_sum(sp):
        return _dot(sp.astype(BF16), neg_later)

    def add_from_row(x, r0, delta):
        return x + delta if r0 == 0 else jnp.concatenate([x[:r0], x[r0:] + delta], axis=0)

    def k_span(k0s, qms, state, masks, first_rows):
        pairs = [(i, p) for p in range(2) for i in range(len(k0s))]
        kbs = [k_ref[pl.ds(k0, tk), :] for k0 in k0s]
        vbs = [v_ref[pl.ds(k0, tk), :] for k0 in k0s]
        accs, runs = list(state[:2]), list(state[2:])
        zs, sps, cs, ws, firsts = {}, {}, {}, {}, {}

        def stage(s, i, p):
            r0 = first_rows[i]
            if s == 0:
                zs[(i, p)] = _dot_nt(qms[p][r0:], kbs[i])
            elif s == 1:
                z = zs[(i, p)]
                sp = softplus(z)
                log_beta = z - sp
                zs[(i, p)] = log_beta
                firsts[(i, p)] = (z[:, 0:1] if masks[i] is None
                                  else jnp.where(masks[i][:, 0:1], z[:, 0:1], log_beta[:, 0:1]))
                sps[(i, p)] = sp if masks[i] is None else jnp.where(masks[i], sp, 0.0)
            elif s == 2:
                cs[(i, p)] = neg_suffix_sum(sps[(i, p)])
            elif s == 3:
                expo = zs[(i, p)] + cs[(i, p)] + runs[p][r0:]
                w = jnp.exp(expo)
                ws[(i, p)] = w if masks[i] is None else jnp.where(masks[i], w, 0.0)
                new_run = expo[:, 0:1] - firsts[(i, p)]
                runs[p] = new_run if r0 == 0 else jnp.concatenate([runs[p][:r0], new_run], axis=0)
            else:
                accs[p] = add_from_row(accs[p], r0, _dot(ws[(i, p)].astype(BF16), vbs[i]))

        n_stages = 5
        for step in range(len(pairs) - 1 + n_stages):
            for n, (i, p) in enumerate(pairs):
                s = step - n
                if 0 <= s < n_stages:
                    stage(s, i, p)
        return (*accs, *runs)

    def q_block(qi, carry):
        q0 = pl.multiple_of(qi * tq, tq)
        q2 = q_ref[pl.ds(q0, tq), :] * ATTN_SCALE
        qms = (jnp.where(lane < HEAD_DIM, q2, 0.0).astype(BF16),
               jnp.where(lane >= HEAD_DIM, q2, 0.0).astype(BF16))
        state = (jnp.zeros((tq, LANES), F32), jnp.zeros((tq, LANES), F32),
                 jnp.zeros((tq, 1), F32), jnp.zeros((tq, 1), F32))
        order = list(reversed(range(k_per_q)))
        state = k_span([q0 + d * tk for d in order], qms, state,
                       [((col + d * tk) < row)[d * tk:] for d in order], [d * tk for d in order])

        def earlier(jj, state):
            k0s = [pl.multiple_of((qi - 1 - jj) * tq + d * tk, tk) for d in order]
            return k_span(k0s, qms, state, [None] * k_per_q, [0] * k_per_q)

        state = lax.fori_loop(0, qi, earlier, state)
        o_ref[pl.ds(q0, tq), :] = jnp.where(lane < HEAD_DIM, state[0], state[1]).astype(o_ref.dtype)
        return carry

    lax.fori_loop(0, nq, q_block, 0)


def _sb(q_b, k_b, v_b, batch, seq):
    n = q_b.shape[0]
    spec = pl.BlockSpec((seq, LANES), lambda b, p: (b, p))
    return pl.pallas_call(
        _sb_kernel,
        grid=(batch, SB_WIDTH // LANES),
        in_specs=[spec, spec, spec],
        out_specs=spec,
        out_shape=jax.ShapeDtypeStruct((n, SB_WIDTH), BF16),
        compiler_params=_params("parallel", "parallel"),
        name="stick_breaking",
    )(q_b, k_b, v_b)


def _merge_kernel(ya_ref, yb_ref, ga_ref, gb_ref, x_ref, pa_ref, pb_ref, wo_ref, h_ref):
    m = (ga_ref[...].astype(F32) * _dot(ya_ref[...], pa_ref[...])
         + gb_ref[...].astype(F32) * _dot(yb_ref[...], pb_ref[...]))
    h_ref[...] = x_ref[...] + _dot(m.astype(BF16), wo_ref[...])


def _merge(y_a, y_b, g_a, g_b, x2, p_a, p_b, w_o):
    n, d = x2.shape
    tm = DENSE_TILE
    row = lambda w: pl.BlockSpec((tm, w), lambda i: (i, 0))
    full = lambda a: pl.BlockSpec(a.shape, lambda i: (0, 0))
    return pl.pallas_call(
        _merge_kernel,
        grid=(n // tm,),
        in_specs=[row(SWA_WIDTH), row(SB_WIDTH), row(d), row(d), row(d), full(p_a), full(p_b), full(w_o)],
        out_specs=row(d),
        out_shape=jax.ShapeDtypeStruct((n, d), F32),
        compiler_params=_params("parallel"),
        name="merge",
    )(y_a, y_b, g_a, g_b, x2, p_a, p_b, w_o)


EXT_EXPERT = 0
EXT_GATE = TOP_K
GATE_PARTS = 3


def _router_kernel(h_ref, nw_ref, rw_ref, rb_ref, hx_ref, meta_ref, meta_t_ref, trun_ref, cnt_ref, run_ref):
    tm, d = h_ref.shape

    @pl.when(pl.program_id(0) == 0)
    def _():
        run_ref[...] = jnp.zeros_like(run_ref)

    hn = _rms(h_ref[...], nw_ref[...])
    hn_hi, hn_lo = _split_bf16(hn)
    both = _dot(hn_hi, rw_ref[...])
    logits = (both[:, :N_EXPERTS] + both[:, N_EXPERTS:] + _dot(hn_lo, rw_ref[:, :N_EXPERTS])
              + rb_ref[...])
    lane = lax.broadcasted_iota(jnp.int32, (tm, N_EXPERTS), 1).astype(F32)
    work = logits
    tops, idxs, hots = [], [], []
    for _ in range(TOP_K):
        m = jnp.max(work, axis=-1, keepdims=True)
        idx = jnp.min(jnp.where(work == m, lane, float(N_EXPERTS)), axis=-1, keepdims=True)
        hot = lane == idx
        work = jnp.where(hot, -jnp.inf, work)
        tops.append(m)
        idxs.append(idx)
        hots.append(hot)
    exps = [jnp.exp(t - tops[0]) for t in tops]
    den = exps[0] + exps[1] + exps[2] + exps[3]
    chosen = jnp.where(hots[0] | hots[1] | hots[2] | hots[3], 1.0, 0.0)
    r = lax.broadcasted_iota(jnp.int32, (tm, tm), 0)
    c = lax.broadcasted_iota(jnp.int32, (tm, tm), 1)
    before = jnp.where(c < r, 1.0, 0.0).astype(BF16)
    run = run_ref[...]
    tile_cnt = jnp.sum(chosen, axis=0, keepdims=True)
    units = jnp.ceil(tile_cnt / SUBLANES)
    earlier = jnp.where(lax.broadcasted_iota(jnp.int32, (N_EXPERTS, N_EXPERTS), 0)
                        < lax.broadcasted_iota(jnp.int32, (N_EXPERTS, N_EXPERTS), 1), 1.0, 0.0)
    group_row = SUBLANES * _dot(jnp.broadcast_to(units, (SUBLANES, N_EXPERTS)).astype(BF16),
                                earlier.astype(BF16))[0:1]
    place = _dot(before, chosen.astype(BF16)) + group_row
    mlane = lax.broadcasted_iota(jnp.int32, (tm, LANES), 1)
    meta = jnp.full((tm, LANES), -1.0, F32)
    ext = jnp.zeros((tm, LANES), F32)
    for k in range(TOP_K):
        row_k = jnp.sum(jnp.where(hots[k], place, 0.0), axis=-1, keepdims=True)
        meta = jnp.where(mlane == k, row_k, meta)
        ext = jnp.where(mlane == EXT_EXPERT + k, idxs[k], ext)
        rest = exps[k] / den
        for part in range(GATE_PARTS):
            piece = rest.astype(BF16).astype(F32)
            ext = jnp.where(mlane == EXT_GATE + GATE_PARTS * k + part, piece, ext)
            rest = rest - piece
    meta_ref[...] = meta
    meta_t_ref[0] = meta.T[:SUBLANES]
    hx_ref[:, :d] = hn_hi
    hx_ref[:, d:] = ext.astype(BF16)
    trun_ref[0] = run
    run = run + units * SUBLANES
    run_ref[...] = run
    cnt_ref[...] = run


def _router(h1, norm_w, router_w, router_b):
    n, d = h1.shape
    tm = TOKEN_TILE
    tiles = n // tm
    return pl.pallas_call(
        _router_kernel,
        grid=(tiles,),
        in_specs=[pl.BlockSpec((tm, d), lambda i: (i, 0)),
                  pl.BlockSpec((1, d), lambda i: (0, 0)),
                  pl.BlockSpec((d, 2 * N_EXPERTS), lambda i: (0, 0)),
                  pl.BlockSpec((1, N_EXPERTS), lambda i: (0, 0))],
        out_specs=[pl.BlockSpec((tm, d + LANES), lambda i: (i, 0)),
                   pl.BlockSpec((tm, LANES), lambda i: (i, 0)),
                   pl.BlockSpec((1, SUBLANES, tm), lambda i: (i, 0, 0)),
                   pl.BlockSpec((1, 1, N_EXPERTS), lambda i: (i, 0, 0)),
                   pl.BlockSpec((1, N_EXPERTS), lambda i: (0, 0))],
        out_shape=[jax.ShapeDtypeStruct((n, d + LANES), BF16),
                   jax.ShapeDtypeStruct((n, LANES), F32),
                   jax.ShapeDtypeStruct((tiles, SUBLANES, tm), F32),
                   jax.ShapeDtypeStruct((tiles, 1, N_EXPERTS), F32),
                   jax.ShapeDtypeStruct((1, N_EXPERTS), F32)],
        scratch_shapes=[pltpu.VMEM((1, N_EXPERTS), F32)],
        compiler_params=_params("arbitrary"),
        name="router",
    )(h1, norm_w.reshape(1, d), jnp.concatenate(_split_bf16(router_w), axis=1),
      router_b.reshape(1, N_EXPERTS))


def _routing_tables(tile_run, counts, n):
    tiles = tile_run.shape[0]
    counts = counts.reshape(N_EXPERTS).astype(jnp.int32)
    run = tile_run.reshape(tiles, N_EXPERTS).astype(jnp.int32)
    tile_len = jnp.concatenate([run[1:], counts[None]], axis=0) - run
    loc = jnp.cumsum(tile_len, axis=1) - tile_len
    padded = (counts + FFN_BLOCK - 1) // FFN_BLOCK * FFN_BLOCK
    seg_end = jnp.cumsum(padded)
    seg_start = seg_end - padded
    dst0 = seg_start[None, :] + run
    zero0 = seg_start + counts
    nzero = (padded - counts) // SUBLANES
    n_real = seg_end[-1] // FFN_BLOCK
    max_blocks = (n * TOP_K + tiles * N_EXPERTS * (SUBLANES - 1)
                  + N_EXPERTS * (FFN_BLOCK - 1)) // FFN_BLOCK
    blk_exp = jnp.minimum(jnp.sum(jnp.arange(max_blocks, dtype=jnp.int32)[:, None] * FFN_BLOCK
                                  >= seg_end[None, :], axis=1), N_EXPERTS - 1).astype(jnp.int32)
    ncopy = jnp.stack([jnp.sum((tile_len // rows) % 2, axis=1) for rows in _copy_sizes()],
                      axis=1).astype(jnp.int32)
    return dict(dst0=dst0.reshape(-1), tile_len=tile_len.reshape(-1), loc=loc.reshape(-1),
                ncopy=ncopy.reshape(-1), zero0=zero0, nzero=nzero, blk_exp=blk_exp,
                n_real=n_real.reshape(1).astype(jnp.int32), cap=max_blocks * FFN_BLOCK)


def _copy_sizes():
    sizes, rows = [], TOKEN_TILE
    while rows >= SUBLANES:
        sizes.append(rows)
        rows //= 2
    return sizes


def _wait_copies(t, ncopy_ref, make_copy):
    sizes = _copy_sizes()
    for c, rows in enumerate(sizes):
        def body(_, carry, rows=rows):
            make_copy(0, 0, rows).wait()
            return carry

        lax.fori_loop(0, ncopy_ref[t * len(sizes) + c], body, 0)


def _start_copies(t, tbl_ref, len_ref, loc_ref, make_copy, enabled):
    for e in range(N_EXPERTS):
        idx = t * N_EXPERTS + e
        seg_row = tbl_ref[idx]
        loc_row = loc_ref[idx]
        length = len_ref[idx]
        done = 0
        for rows in _copy_sizes():
            @pl.when(((length & rows) != 0) & enabled)
            def _(done=done, rows=rows):
                make_copy(pl.multiple_of(seg_row + done, SUBLANES),
                          pl.multiple_of(loc_row + done, SUBLANES), rows).start()
            done = done + (length & rows)


def _dispatch_kernel(dst0_ref, len_ref, loc_ref, ncopy_ref, zero0_ref, nzero_ref, nreal_ref,
                     hx_ref, rows_t_ref, xb_ref, srt_ref, zero_ref, sem):
    t = pl.program_id(0)
    tm = hx_ref.shape[0]
    rows_t = rows_t_ref[0]
    slot = t % 2

    def copy_from(s):
        def copy(seg_row, loc_row, rows):
            return pltpu.make_async_copy(srt_ref.at[s, pl.ds(loc_row, rows)],
                                         xb_ref.at[pl.ds(seg_row, rows)], sem.at[s])
        return copy

    _start_copies(jnp.maximum(t - 1, 0), dst0_ref, len_ref, loc_ref, copy_from(1 - slot), t > 0)
    part = SORTED_ROWS // SORT_PARTS
    for c in range(SORT_PARTS):
        prow = (lax.broadcasted_iota(jnp.int32, (part, tm), 0) + c * part).astype(F32)
        onehot = jnp.zeros((part, tm), F32)
        for k in range(TOP_K):
            onehot = jnp.where(prow == rows_t[k:k + 1, :], 1.0, onehot)
        srt_ref[slot, c * part:(c + 1) * part] = _dot(onehot.astype(BF16), hx_ref[...])

    @pl.when(t > 0)
    def _():
        _wait_copies(t - 1, ncopy_ref, copy_from(1 - slot))

    @pl.when(t == pl.num_programs(0) - 1)
    def _():
        _start_copies(t, dst0_ref, len_ref, loc_ref, copy_from(slot), True)
        _wait_copies(t, ncopy_ref, copy_from(slot))
        zero_ref[...] = jnp.zeros_like(zero_ref)
        for op in ("start", "wait"):
            for e in range(N_EXPERTS):
                def body(c, carry):
                    row0 = pl.multiple_of(zero0_ref[e] + c * SUBLANES, SUBLANES)
                    cp = pltpu.make_async_copy(zero_ref.at[pl.ds(0, SUBLANES)],
                                               xb_ref.at[pl.ds(row0, SUBLANES)], sem.at[0])
                    cp.start() if op == "start" else cp.wait()
                    return carry

                lax.fori_loop(0, nzero_ref[e], body, 0)

            def tail(b, carry):
                row0 = pl.multiple_of(b * FFN_BLOCK, FFN_BLOCK)
                cp = pltpu.make_async_copy(zero_ref, xb_ref.at[pl.ds(row0, FFN_BLOCK)], sem.at[0])
                cp.start() if op == "start" else cp.wait()
                return carry

            lax.fori_loop(nreal_ref[0], xb_ref.shape[0] // FFN_BLOCK, tail, 0)


def _dispatch(hx, meta_t, tbl):
    n, width = hx.shape
    tm = TOKEN_TILE
    grid_spec = pltpu.PrefetchScalarGridSpec(
        num_scalar_prefetch=7,
        grid=(n // tm,),
        in_specs=[pl.BlockSpec((tm, width), lambda i, *_: (i, 0)),
                  pl.BlockSpec((1, SUBLANES, tm), lambda i, *_: (i, 0, 0))],
        out_specs=pl.BlockSpec(memory_space=pl.ANY),
        scratch_shapes=[pltpu.VMEM((2, SORTED_ROWS, width), F32),
                        pltpu.VMEM((FFN_BLOCK, width), F32),
                        pltpu.SemaphoreType.DMA((2,))],
    )
    return pl.pallas_call(
        _dispatch_kernel,
        grid_spec=grid_spec,
        out_shape=jax.ShapeDtypeStruct((tbl["cap"], width), F32),
        compiler_params=_params("arbitrary"),
        name="dispatch",
    )(tbl["dst0"], tbl["tile_len"], tbl["loc"], tbl["ncopy"], tbl["zero0"], tbl["nzero"], tbl["n_real"],
      hx, meta_t)


def _ffn_kernel(exp_ref, nreal_ref, x_ref, wgu_ref, bgu_ref, wd_ref, bd_ref, y_ref, wgu_bf, wd_bf):
    b = pl.program_id(0)
    d_ff, d = wd_ref.shape[1], wd_ref.shape[2]

    @pl.when(b >= nreal_ref[0])
    def _():
        y_ref[...] = jnp.zeros_like(y_ref)

    @pl.when((b == 0) | (exp_ref[b] != exp_ref[jnp.maximum(b - 1, 0)]))
    def _():
        wgu_bf[...] = wgu_ref[0].astype(BF16)
        wd_bf[...] = wd_ref[0].astype(BF16)

    @pl.when(b < nreal_ref[0])
    def _():
        x = x_ref[...]
        ext = x[:, d:]
        expert = exp_ref[b].astype(F32)
        gate = jnp.zeros((x.shape[0], 1), F32)
        for k in range(TOP_K):
            g0 = EXT_GATE + GATE_PARTS * k
            g = ext[:, g0:g0 + 1] + ext[:, g0 + 1:g0 + 2] + ext[:, g0 + 2:g0 + 3]
            gate = gate + jnp.where(ext[:, EXT_EXPERT + k:EXT_EXPERT + k + 1] == expert, g, 0.0)
        gu = _dot(x[:, :d].astype(BF16), wgu_bf[...]) + bgu_ref[0]
        g = jnp.minimum(gu[:, :d_ff], SWIGLU_LIMIT)
        u = jnp.clip(gu[:, d_ff:], -SWIGLU_LIMIT, SWIGLU_LIMIT)
        act = (u + 1.0) * (g * jax.nn.sigmoid(SWIGLU_ALPHA * g))
        y_ref[...] = gate * (_dot(act.astype(BF16), wd_bf[...]) + bd_ref[0])


def _ffn(xb, tbl, w_gate_up, b_gate_up, w_down, b_down):
    cap, width = xb.shape
    e, d, two_ff = w_gate_up.shape
    d_ff = two_ff // 2
    grid_spec = pltpu.PrefetchScalarGridSpec(
        num_scalar_prefetch=2,
        grid=(cap // FFN_BLOCK,),
        in_specs=[pl.BlockSpec((FFN_BLOCK, width), lambda b, ex, nr: (b, 0)),
                  pl.BlockSpec((1, d, two_ff), lambda b, ex, nr: (ex[b], 0, 0)),
                  pl.BlockSpec((1, 1, two_ff), lambda b, ex, nr: (ex[b], 0, 0)),
                  pl.BlockSpec((1, d_ff, d), lambda b, ex, nr: (ex[b], 0, 0)),
                  pl.BlockSpec((1, 1, d), lambda b, ex, nr: (ex[b], 0, 0))],
        out_specs=pl.BlockSpec((FFN_BLOCK, d), lambda b, ex, nr: (b, 0)),
        scratch_shapes=[pltpu.VMEM((d, two_ff), BF16), pltpu.VMEM((d_ff, d), BF16)],
    )
    return pl.pallas_call(
        _ffn_kernel,
        grid_spec=grid_spec,
        out_shape=jax.ShapeDtypeStruct((cap, d), F32),
        compiler_params=_params("arbitrary"),
        name="expert_ffn",
    )(tbl["blk_exp"], tbl["n_real"], xb, w_gate_up,
      b_gate_up.reshape(e, 1, two_ff), w_down, b_down.reshape(e, 1, d))


def _combine_kernel(dst0_ref, len_ref, loc_ref, ncopy_ref, meta_ref, h_ref, y_ref, o_ref, srt_ref, sem):
    t = pl.program_id(0)
    tm = h_ref.shape[0]

    @pl.when(t == 0)
    def _():
        srt_ref[...] = jnp.zeros_like(srt_ref)

    slot = t % 2

    def copy_into(s):
        def copy(seg_row, loc_row, rows):
            return pltpu.make_async_copy(y_ref.at[pl.ds(seg_row, rows)],
                                         srt_ref.at[s, pl.ds(loc_row, rows)], sem.at[s])
        return copy

    @pl.when(t == 0)
    def _():
        _start_copies(t, dst0_ref, len_ref, loc_ref, copy_into(slot), True)

    _wait_copies(t, ncopy_ref, copy_into(slot))
    last = pl.num_programs(0) - 1

    meta = meta_ref[...]
    rows = [meta[:, k:k + 1] for k in range(TOP_K)]
    part = SORTED_ROWS // SORT_PARTS

    def onehot_part(c):
        pcol = (lax.broadcasted_iota(jnp.int32, (tm, part), 1) + c * part).astype(F32)
        onehot = jnp.zeros((tm, part), F32)
        for k in range(TOP_K):
            onehot = jnp.where(pcol == rows[k], 1.0, onehot)
        return onehot.astype(BF16)

    _start_copies(jnp.minimum(t + 1, last), dst0_ref, len_ref, loc_ref, copy_into(1 - slot), t < last)
    onehots = [onehot_part(c) for c in range(SORT_PARTS)]
    out = h_ref[...]
    for c in range(SORT_PARTS):
        out = out + _dot(onehots[c], srt_ref[slot, c * part:(c + 1) * part].astype(BF16))
    o_ref[...] = out


def _combine(meta, tbl, h1, yb):
    n, d = h1.shape
    tm = TOKEN_TILE
    grid_spec = pltpu.PrefetchScalarGridSpec(
        num_scalar_prefetch=4,
        grid=(n // tm,),
        in_specs=[pl.BlockSpec((tm, LANES), lambda i, *_: (i, 0)),
                  pl.BlockSpec((tm, d), lambda i, *_: (i, 0)),
                  pl.BlockSpec(memory_space=pl.ANY)],
        out_specs=pl.BlockSpec((tm, d), lambda i, *_: (i, 0)),
        scratch_shapes=[pltpu.VMEM((2, SORTED_ROWS, d), F32), pltpu.SemaphoreType.DMA((2,))],
    )
    return pl.pallas_call(
        _combine_kernel,
        grid_spec=grid_spec,
        out_shape=jax.ShapeDtypeStruct((n, d), F32),
        compiler_params=_params("arbitrary"),
        name="combine",
    )(tbl["dst0"], tbl["tile_len"], tbl["loc"], tbl["ncopy"], meta, h1, yb)


def kernel(x, attn_norm_w, w_in, q_norm_w, k_norm_w, sinks, w_proj_swa, w_proj_sb, w_out,
           ffn_norm_w, router_w, router_b, w_gate_up, b_gate_up, w_down, b_down):
    batch, seq, d = x.shape
    n = batch * seq
    h = x.reshape(n, d)
    for layer in range(attn_norm_w.shape[0]):
        q_a, kv_a, q_b, k_b, v_b, g_a, g_b = _in_proj(h, attn_norm_w[layer], w_in[layer].astype(BF16))
        y_a = _swa(q_a, kv_a, q_norm_w[layer], k_norm_w[layer], sinks[layer], batch, seq)
        y_b = _sb(q_b, k_b, v_b, batch, seq)
        h1 = _merge(y_a, y_b, g_a, g_b, h, w_proj_swa[layer].astype(BF16),
                    w_proj_sb[layer].astype(BF16), w_out[layer].astype(BF16))
        hx, meta, meta_t, tile_run, counts = _router(h1, ffn_norm_w[layer], router_w[layer], router_b[layer])
        tbl = _routing_tables(tile_run, counts, n)
        xb = _dispatch(hx, meta_t, tbl)
        yb = _ffn(xb, tbl, w_gate_up[layer], b_gate_up[layer], w_down[layer], b_down[layer])
        h = _combine(meta, tbl, h1, yb)
    return h.reshape(batch, seq, d)
```

```python
import jax
import jax.numpy as jnp
from jax import lax
from jax.experimental import pallas as pl
from jax.experimental.pallas import tpu as pltpu

HEAD_DIM = 64
SWA_HEADS = 8
SWA_KV_HEADS = 2
SWA_GROUP = SWA_HEADS // SWA_KV_HEADS
BLOCK = 128
SB_HEADS = 8
SWA_WIDTH = SWA_HEADS * HEAD_DIM
SWA_KV_WIDTH = SWA_KV_HEADS * HEAD_DIM
SB_WIDTH = SB_HEADS * HEAD_DIM
N_EXPERTS = 32
TOP_K = 4
SWIGLU_LIMIT = 7.0
SWIGLU_ALPHA = 1.702
NORM_EPS = 1e-5
ATTN_SCALE = HEAD_DIM ** -0.5
SIGN_BIT = -2 ** 31
LOG2E = 1.4426950408889634

LANES = 128
SUBLANES = 8
DENSE_TILE = 1024
TOKEN_TILE = 512
FFN_BLOCK = 512
SORTED_ROWS = -(-(TOKEN_TILE * TOP_K + N_EXPERTS * (SUBLANES - 1)) // LANES) * LANES
SWA_BLOCKS = 8
SORT_PARTS = 3
SB_Q_ROWS = 1024
SB_K_ROWS = 256
VMEM_LIMIT = 56 * 1024 * 1024

F32 = jnp.float32
BF16 = jnp.bfloat16
NEG_BIG = -1e30


def _dot(a, b):
    return jnp.dot(a, b, preferred_element_type=F32)


def _dot_nt(a, b):
    return lax.dot_general(a, b, (((1,), (1,)), ((), ())), preferred_element_type=F32)


def _sigmoid(x):
    return jax.nn.sigmoid(x)


def _rms(x, w):
    return x * lax.rsqrt(jnp.mean(x * x, axis=-1, keepdims=True) + NORM_EPS) * w


def _params(*sem):
    return pltpu.CompilerParams(dimension_semantics=sem, vmem_limit_bytes=VMEM_LIMIT)


def _in_proj_kernel(x_ref, nw_ref, w_ref, qa_ref, kva_ref, qb_ref, kb_ref, vb_ref, ga_ref, gb_ref):
    xn = _rms(x_ref[...], nw_ref[...]).astype(BF16)
    off = 0
    for ref, gate in ((qa_ref, False), (kva_ref, False), (qb_ref, False), (kb_ref, False),
                      (vb_ref, False), (ga_ref, True), (gb_ref, True)):
        width = ref.shape[1]
        y = _dot(xn, w_ref[:, off:off + width])
        if gate:
            y = _sigmoid(y)
        if ref is qb_ref:
            y = y * (ATTN_SCALE * LOG2E)
        ref[...] = y.astype(ref.dtype)
        off += width


def _in_proj(x2, norm_w, w_in):
    n, d = x2.shape
    widths = (SWA_WIDTH, 2 * SWA_KV_WIDTH, SB_WIDTH, SB_WIDTH, SB_WIDTH, d, d)
    tm = DENSE_TILE
    return pl.pallas_call(
        _in_proj_kernel,
        grid=(n // tm,),
        in_specs=[pl.BlockSpec((tm, d), lambda i: (i, 0)),
                  pl.BlockSpec((1, d), lambda i: (0, 0)),
                  pl.BlockSpec(w_in.shape, lambda i: (0, 0))],
        out_specs=[pl.BlockSpec((tm, w), lambda i: (i, 0)) for w in widths],
        out_shape=[jax.ShapeDtypeStruct((n, w), BF16) for w in widths],
        compiler_params=_params("parallel"),
        name="in_proj",
    )(x2, norm_w.reshape(1, d), w_in)


def _swa_kernel(q_ref, kvc_ref, kvp_ref, qnw_ref, knw_ref, sink_ref, o_ref):
    i = pl.program_id(1)
    half = HEAD_DIM

    def group_mean_sq(x):
        w = x.shape[1]
        same_head = (lax.broadcasted_iota(jnp.int32, (w, w), 0) // half
                     == lax.broadcasted_iota(jnp.int32, (w, w), 1) // half)
        avg = jnp.where(same_head, 1.0 / half, 0.0).astype(BF16)
        hi, lo = _split_bf16(x * x)
        return _dot(hi, avg) + _dot(lo, avg)

    q = q_ref[...].astype(F32)
    kv = jnp.concatenate([kvp_ref[...], kvc_ref[...]], axis=0)
    k = kv[:, :SWA_KV_WIDTH].astype(F32)
    v = kv[:, SWA_KV_WIDTH:].astype(F32)
    qn = (q * lax.rsqrt(group_mean_sq(q) + NORM_EPS) * qnw_ref[...] * ATTN_SCALE).astype(BF16)
    kn = k * lax.rsqrt(group_mean_sq(k) + NORM_EPS) * knw_ref[...]
    lane = lax.broadcasted_iota(jnp.int32, kn.shape, 1)
    low = lane < half
    k_swap = pltpu.roll(kn, half, 1)
    v_swap = pltpu.roll(v, half, 1).astype(BF16)
    v_same = v.astype(BF16)
    keys = [[jnp.where(low, kn, 0.0).astype(BF16), jnp.where(low, 0.0, k_swap).astype(BF16)],
            [jnp.where(low, k_swap, 0.0).astype(BF16), jnp.where(low, 0.0, kn).astype(BF16)]]
    vals = [[v_same, v_swap], [v_swap, v_same]]

    row = lax.broadcasted_iota(jnp.int32, (BLOCK, 2 * BLOCK), 0)
    col = lax.broadcasted_iota(jnp.int32, (BLOCK, 2 * BLOCK), 1)
    rel = row + BLOCK - col
    in_window = (rel >= 0) & (rel < BLOCK)
    rel = rel.astype(F32)
    out_lane_low = lax.broadcasted_iota(jnp.int32, (BLOCK, LANES), 1) < half
    for sub in range(SWA_BLOCKS):
        rows = slice(sub * BLOCK, (sub + 1) * BLOCK)
        band = slice(sub * BLOCK, (sub + 2) * BLOCK)
        valid = in_window & ((col >= BLOCK) | (i > 0)) if sub == 0 else in_window
        for blk in range(SWA_HEADS // 2):
            qb = qn[rows, blk * LANES:(blk + 1) * LANES]
            halves = []
            for par in range(2):
                h = 2 * blk + par
                g = h // SWA_GROUP
                slope = 2.0 ** (-8.0 * (h + 1) / SWA_HEADS)
                s = jnp.where(valid, _dot_nt(qb, keys[g][par][band]) - slope * rel, NEG_BIG)
                sink = sink_ref[h]
                m = jnp.maximum(jnp.max(s, axis=-1, keepdims=True), sink)
                p = jnp.exp(s - m)
                den = jnp.sum(p, axis=-1, keepdims=True) + jnp.exp(sink - m)
                halves.append(_dot(p.astype(BF16), vals[g][par][band]) / den)
            o_ref[rows, blk * LANES:(blk + 1) * LANES] = jnp.where(
                out_lane_low, halves[0], halves[1]).astype(o_ref.dtype)


def _swa(q_a, kv_a, q_norm_w, k_norm_w, sinks, batch, seq):
    n = q_a.shape[0]
    rows = SWA_BLOCKS * BLOCK
    nb = seq // rows
    return pl.pallas_call(
        _swa_kernel,
        grid=(batch, nb),
        in_specs=[pl.BlockSpec((rows, SWA_WIDTH), lambda b, i: (b * nb + i, 0)),
                  pl.BlockSpec((rows, 2 * SWA_KV_WIDTH), lambda b, i: (b * nb + i, 0)),
                  pl.BlockSpec((BLOCK, 2 * SWA_KV_WIDTH),
                               lambda b, i: (jnp.maximum((b * nb + i) * SWA_BLOCKS - 1, 0), 0)),
                  pl.BlockSpec((1, SWA_WIDTH), lambda b, i: (0, 0)),
                  pl.BlockSpec((1, SWA_KV_WIDTH), lambda b, i: (0, 0)),
                  pl.BlockSpec(memory_space=pltpu.SMEM)],
        out_specs=pl.BlockSpec((rows, SWA_WIDTH), lambda b, i: (b * nb + i, 0)),
        out_shape=jax.ShapeDtypeStruct((n, SWA_WIDTH), BF16),
        compiler_params=_params("parallel", "parallel"),
        name="swa",
    )(q_a, kv_a, kv_a, jnp.tile(q_norm_w, SWA_HEADS).reshape(1, SWA_WIDTH),
      jnp.tile(k_norm_w, SWA_KV_HEADS).reshape(1, SWA_KV_WIDTH), sinks)


def _split_bf16(x):
    hi = x.astype(BF16)
    lo = (x - hi.astype(F32)).astype(BF16)
    return hi, lo


def _sb_kernel(q_ref, k_ref, v_ref, o_ref):
    seq = q_ref.shape[0]
    tq, tk = SB_Q_ROWS, SB_K_ROWS
    nq = seq // tq
    k_per_q = tq // tk
    lane = lax.broadcasted_iota(jnp.int32, (tq, LANES), 1)
    row = lax.broadcasted_iota(jnp.int32, (tq, tk), 0)
    col = lax.broadcasted_iota(jnp.int32, (tq, tk), 1)
    neg_later = jnp.where(lax.broadcasted_iota(jnp.int32, (tk, tk), 0)
                          > lax.broadcasted_iota(jnp.int32, (tk, tk), 1), -1.0, 0.0).astype(BF16)

    def softplus(z):
        neg_abs = lax.bitcast_convert_type(lax.bitcast_convert_type(z, jnp.int32) | SIGN_BIT, F32)
        return jnp.maximum(z, 0.0) + jnp.log(1.0 + jnp.exp2(neg_abs)) * LOG2E

    def neg_suffix_sum(sp):
        return _dot(sp.astype(BF16), neg_later)

    def add_from_row(x, r0, delta):
        return x + delta if r0 == 0 else jnp.concatenate([x[:r0], x[r0:] + delta], axis=0)

    def k_span(k0s, qms, state, masks, first_rows):
        pairs = [(i, p) for p in range(2) for i in range(len(k0s))]
        kbs = [k_ref[pl.ds(k0, tk), :] for k0 in k0s]
        vbs = [v_ref[pl.ds(k0, tk), :] for k0 in k0s]
        accs, runs = list(state[:2]), list(state[2:])
        zs, sps, cs, ws, firsts = {}, {}, {}, {}, {}

        def stage(s, i, p):
            r0 = first_rows[i]
            if s == 0:
                zs[(i, p)] = _dot_nt(qms[p][r0:], kbs[i])
            elif s == 1:
                z = zs[(i, p)]
                sp = softplus(z)
                log_beta = z - sp
                zs[(i, p)] = log_beta
                firsts[(i, p)] = (z[:, 0:1] if masks[i] is None
                                  else jnp.where(masks[i][:, 0:1], z[:, 0:1], log_beta[:, 0:1]))
                sps[(i, p)] = sp if masks[i] is None else jnp.where(masks[i], sp, 0.0)
            elif s == 2:
                cs[(i, p)] = neg_suffix_sum(sps[(i, p)])
            elif s == 3:
                expo = zs[(i, p)] + cs[(i, p)] + runs[p][r0:]
                w = jnp.exp2(expo)
                ws[(i, p)] = w if masks[i] is None else jnp.where(masks[i], w, 0.0)
                new_run = expo[:, 0:1] - firsts[(i, p)]
                runs[p] = new_run if r0 == 0 else jnp.concatenate([runs[p][:r0], new_run], axis=0)
            else:
                accs[p] = add_from_row(accs[p], r0, _dot(ws[(i, p)].astype(BF16), vbs[i]))

        n_stages = 5
        for step in range(len(pairs) - 1 + n_stages):
            for n, (i, p) in enumerate(pairs):
                s = step - n
                if 0 <= s < n_stages:
                    stage(s, i, p)
        return (*accs, *runs)

    def q_block(qi, carry):
        q0 = pl.multiple_of(qi * tq, tq)
        q2 = q_ref[pl.ds(q0, tq), :]
        qms = (jnp.where(lane < HEAD_DIM, q2, 0.0).astype(BF16),
               jnp.where(lane >= HEAD_DIM, q2, 0.0).astype(BF16))
        state = (jnp.zeros((tq, LANES), F32), jnp.zeros((tq, LANES), F32),
                 jnp.zeros((tq, 1), F32), jnp.zeros((tq, 1), F32))
        order = list(reversed(range(k_per_q)))
        state = k_span([q0 + d * tk for d in order], qms, state,
                       [((col + d * tk) < row)[d * tk:] for d in order], [d * tk for d in order])

        def earlier(jj, state):
            k0s = [pl.multiple_of((qi - 1 - jj) * tq + d * tk, tk) for d in order]
            return k_span(k0s, qms, state, [None] * k_per_q, [0] * k_per_q)

        state = lax.fori_loop(0, qi, earlier, state)
        o_ref[pl.ds(q0, tq), :] = jnp.where(lane < HEAD_DIM, state[0], state[1]).astype(o_ref.dtype)
        return carry

    lax.fori_loop(0, nq, q_block, 0)


def _sb(q_b, k_b, v_b, batch, seq):
    n = q_b.shape[0]
    spec = pl.BlockSpec((seq, LANES), lambda b, p: (b, p))
    return pl.pallas_call(
        _sb_kernel,
        grid=(batch, SB_WIDTH // LANES),
        in_specs=[spec, spec, spec],
        out_specs=spec,
        out_shape=jax.ShapeDtypeStruct((n, SB_WIDTH), BF16),
        compiler_params=_params("parallel", "parallel"),
        name="stick_breaking",
    )(q_b, k_b, v_b)


def _merge_kernel(ya_ref, yb_ref, ga_ref, gb_ref, x_ref, pa_ref, pb_ref, wo_ref, h_ref):
    m = (ga_ref[...].astype(F32) * _dot(ya_ref[...], pa_ref[...])
         + gb_ref[...].astype(F32) * _dot(yb_ref[...], pb_ref[...]))
    h_ref[...] = x_ref[...] + _dot(m.astype(BF16), wo_ref[...])


def _merge(y_a, y_b, g_a, g_b, x2, p_a, p_b, w_o):
    n, d = x2.shape
    tm = DENSE_TILE
    row = lambda w: pl.BlockSpec((tm, w), lambda i: (i, 0))
    full = lambda a: pl.BlockSpec(a.shape, lambda i: (0, 0))
    return pl.pallas_call(
        _merge_kernel,
        grid=(n // tm,),
        in_specs=[row(SWA_WIDTH), row(SB_WIDTH), row(d), row(d), row(d), full(p_a), full(p_b), full(w_o)],
        out_specs=row(d),
        out_shape=jax.ShapeDtypeStruct((n, d), F32),
        compiler_params=_params("parallel"),
        name="merge",
    )(y_a, y_b, g_a, g_b, x2, p_a, p_b, w_o)


EXT_EXPERT = 0
EXT_GATE = TOP_K
GATE_PARTS = 3


def _router_kernel(h_ref, nw_ref, rw_ref, rb_ref, hx_ref, meta_ref, meta_t_ref, trun_ref, cnt_ref, run_ref):
    tm, d = h_ref.shape

    @pl.when(pl.program_id(0) == 0)
    def _():
        run_ref[...] = jnp.zeros_like(run_ref)

    hn = _rms(h_ref[...], nw_ref[...])
    hn_hi, hn_lo = _split_bf16(hn)
    both = _dot(hn_hi, rw_ref[...])
    logits = (both[:, :N_EXPERTS] + both[:, N_EXPERTS:] + _dot(hn_lo, rw_ref[:, :N_EXPERTS])
              + rb_ref[...])
    lane = lax.broadcasted_iota(jnp.int32, (tm, N_EXPERTS), 1).astype(F32)
    work = logits
    tops, idxs, hots = [], [], []
    for _ in range(TOP_K):
        m = jnp.max(work, axis=-1, keepdims=True)
        idx = jnp.min(jnp.where(work == m, lane, float(N_EXPERTS)), axis=-1, keepdims=True)
        hot = lane == idx
        work = jnp.where(hot, -jnp.inf, work)
        tops.append(m)
        idxs.append(idx)
        hots.append(hot)
    exps = [jnp.exp(t - tops[0]) for t in tops]
    den = exps[0] + exps[1] + exps[2] + exps[3]
    chosen = jnp.where(hots[0] | hots[1] | hots[2] | hots[3], 1.0, 0.0)
    r = lax.broadcasted_iota(jnp.int32, (tm, tm), 0)
    c = lax.broadcasted_iota(jnp.int32, (tm, tm), 1)
    before = jnp.where(c < r, 1.0, 0.0).astype(BF16)
    run = run_ref[...]
    tile_cnt = jnp.sum(chosen, axis=0, keepdims=True)
    units = jnp.ceil(tile_cnt / SUBLANES)
    earlier = jnp.where(lax.broadcasted_iota(jnp.int32, (N_EXPERTS, N_EXPERTS), 0)
                        < lax.broadcasted_iota(jnp.int32, (N_EXPERTS, N_EXPERTS), 1), 1.0, 0.0)
    group_row = SUBLANES * _dot(jnp.broadcast_to(units, (SUBLANES, N_EXPERTS)).astype(BF16),
                                earlier.astype(BF16))[0:1]
    place = _dot(before, chosen.astype(BF16)) + group_row
    mlane = lax.broadcasted_iota(jnp.int32, (tm, LANES), 1)
    meta = jnp.full((tm, LANES), -1.0, F32)
    ext = jnp.zeros((tm, LANES), F32)
    for k in range(TOP_K):
        row_k = jnp.sum(jnp.where(hots[k], place, 0.0), axis=-1, keepdims=True)
        meta = jnp.where(mlane == k, row_k, meta)
        ext = jnp.where(mlane == EXT_EXPERT + k, idxs[k], ext)
        rest = exps[k] / den
        for part in range(GATE_PARTS):
            piece = rest.astype(BF16).astype(F32)
            ext = jnp.where(mlane == EXT_GATE + GATE_PARTS * k + part, piece, ext)
            rest = rest - piece
    meta_ref[...] = meta
    meta_t_ref[0] = meta.T[:SUBLANES]
    hx_ref[:, :d] = hn_hi
    hx_ref[:, d:] = ext.astype(BF16)
    trun_ref[0] = run
    run = run + units * SUBLANES
    run_ref[...] = run
    cnt_ref[...] = run


def _router(h1, norm_w, router_w, router_b):
    n, d = h1.shape
    tm = TOKEN_TILE
    tiles = n // tm
    return pl.pallas_call(
        _router_kernel,
        grid=(tiles,),
        in_specs=[pl.BlockSpec((tm, d), lambda i: (i, 0)),
                  pl.BlockSpec((1, d), lambda i: (0, 0)),
                  pl.BlockSpec((d, 2 * N_EXPERTS), lambda i: (0, 0)),
                  pl.BlockSpec((1, N_EXPERTS), lambda i: (0, 0))],
        out_specs=[pl.BlockSpec((tm, d + LANES), lambda i: (i, 0)),
                   pl.BlockSpec((tm, LANES), lambda i: (i, 0)),
                   pl.BlockSpec((1, SUBLANES, tm), lambda i: (i, 0, 0)),
                   pl.BlockSpec((1, 1, N_EXPERTS), lambda i: (i, 0, 0)),
                   pl.BlockSpec((1, N_EXPERTS), lambda i: (0, 0))],
        out_shape=[jax.ShapeDtypeStruct((n, d + LANES), BF16),
                   jax.ShapeDtypeStruct((n, LANES), F32),
                   jax.ShapeDtypeStruct((tiles, SUBLANES, tm), F32),
                   jax.ShapeDtypeStruct((tiles, 1, N_EXPERTS), F32),
                   jax.ShapeDtypeStruct((1, N_EXPERTS), F32)],
        scratch_shapes=[pltpu.VMEM((1, N_EXPERTS), F32)],
        compiler_params=_params("arbitrary"),
        name="router",
    )(h1, norm_w.reshape(1, d), jnp.concatenate(_split_bf16(router_w), axis=1),
      router_b.reshape(1, N_EXPERTS))


def _routing_tables(tile_run, counts, n):
    tiles = tile_run.shape[0]
    counts = counts.reshape(N_EXPERTS).astype(jnp.int32)
    run = tile_run.reshape(tiles, N_EXPERTS).astype(jnp.int32)
    tile_len = jnp.concatenate([run[1:], counts[None]], axis=0) - run
    loc = jnp.cumsum(tile_len, axis=1) - tile_len
    padded = (counts + FFN_BLOCK - 1) // FFN_BLOCK * FFN_BLOCK
    seg_end = jnp.cumsum(padded)
    seg_start = seg_end - padded
    dst0 = seg_start[None, :] + run
    zero0 = seg_start + counts
    nzero = (padded - counts) // SUBLANES
    n_real = seg_end[-1] // FFN_BLOCK
    max_blocks = (n * TOP_K + tiles * N_EXPERTS * (SUBLANES - 1)
                  + N_EXPERTS * (FFN_BLOCK - 1)) // FFN_BLOCK
    blk_exp = jnp.minimum(jnp.sum(jnp.arange(max_blocks, dtype=jnp.int32)[:, None] * FFN_BLOCK
                                  >= seg_end[None, :], axis=1), N_EXPERTS - 1).astype(jnp.int32)
    ncopy = jnp.stack([jnp.sum((tile_len // rows) % 2, axis=1) for rows in _copy_sizes()],
                      axis=1).astype(jnp.int32)
    return dict(dst0=dst0.reshape(-1), tile_len=tile_len.reshape(-1), loc=loc.reshape(-1),
                ncopy=ncopy.reshape(-1), zero0=zero0, nzero=nzero, blk_exp=blk_exp,
                n_real=n_real.reshape(1).astype(jnp.int32), cap=max_blocks * FFN_BLOCK)


def _copy_sizes():
    sizes, rows = [], TOKEN_TILE
    while rows >= SUBLANES:
        sizes.append(rows)
        rows //= 2
    return sizes


def _wait_copies(t, ncopy_ref, make_copy):
    sizes = _copy_sizes()
    for c, rows in enumerate(sizes):
        def body(_, carry, rows=rows):
            make_copy(0, 0, rows).wait()
            return carry

        lax.fori_loop(0, ncopy_ref[t * len(sizes) + c], body, 0)


def _start_copies(t, tbl_ref, len_ref, loc_ref, make_copy, enabled):
    for e in range(N_EXPERTS):
        idx = t * N_EXPERTS + e
        seg_row = tbl_ref[idx]
        loc_row = loc_ref[idx]
        length = len_ref[idx]
        done = 0
        for rows in _copy_sizes():
            @pl.when(((length & rows) != 0) & enabled)
            def _(done=done, rows=rows):
                make_copy(pl.multiple_of(seg_row + done, SUBLANES),
                          pl.multiple_of(loc_row + done, SUBLANES), rows).start()
            done = done + (length & rows)


def _dispatch_kernel(dst0_ref, len_ref, loc_ref, ncopy_ref, zero0_ref, nzero_ref, nreal_ref,
                     hx_ref, rows_t_ref, xb_ref, srt_ref, zero_ref, sem):
    t = pl.program_id(0)
    tm = hx_ref.shape[0]
    rows_t = rows_t_ref[0]
    slot = t % 2

    def copy_from(s):
        def copy(seg_row, loc_row, rows):
            return pltpu.make_async_copy(srt_ref.at[s, pl.ds(loc_row, rows)],
                                         xb_ref.at[pl.ds(seg_row, rows)], sem.at[s])
        return copy

    _start_copies(jnp.maximum(t - 1, 0), dst0_ref, len_ref, loc_ref, copy_from(1 - slot), t > 0)
    part = SORTED_ROWS // SORT_PARTS
    for c in range(SORT_PARTS):
        prow = (lax.broadcasted_iota(jnp.int32, (part, tm), 0) + c * part).astype(F32)
        onehot = jnp.zeros((part, tm), F32)
        for k in range(TOP_K):
            onehot = jnp.where(prow == rows_t[k:k + 1, :], 1.0, onehot)
        srt_ref[slot, c * part:(c + 1) * part] = _dot(onehot.astype(BF16), hx_ref[...])

    @pl.when(t > 0)
    def _():
        _wait_copies(t - 1, ncopy_ref, copy_from(1 - slot))

    @pl.when(t == pl.num_programs(0) - 1)
    def _():
        _start_copies(t, dst0_ref, len_ref, loc_ref, copy_from(slot), True)
        _wait_copies(t, ncopy_ref, copy_from(slot))
        zero_ref[...] = jnp.zeros_like(zero_ref)
        for op in ("start", "wait"):
            for e in range(N_EXPERTS):
                def body(c, carry):
                    row0 = pl.multiple_of(zero0_ref[e] + c * SUBLANES, SUBLANES)
                    cp = pltpu.make_async_copy(zero_ref.at[pl.ds(0, SUBLANES)],
                                               xb_ref.at[pl.ds(row0, SUBLANES)], sem.at[0])
                    cp.start() if op == "start" else cp.wait()
                    return carry

                lax.fori_loop(0, nzero_ref[e], body, 0)

            def tail(b, carry):
                row0 = pl.multiple_of(b * FFN_BLOCK, FFN_BLOCK)
                cp = pltpu.make_async_copy(zero_ref, xb_ref.at[pl.ds(row0, FFN_BLOCK)], sem.at[0])
                cp.start() if op == "start" else cp.wait()
                return carry

            lax.fori_loop(nreal_ref[0], xb_ref.shape[0] // FFN_BLOCK, tail, 0)


def _dispatch(hx, meta_t, tbl):
    n, width = hx.shape
    tm = TOKEN_TILE
    grid_spec = pltpu.PrefetchScalarGridSpec(
        num_scalar_prefetch=7,
        grid=(n // tm,),
        in_specs=[pl.BlockSpec((tm, width), lambda i, *_: (i, 0)),
                  pl.BlockSpec((1, SUBLANES, tm), lambda i, *_: (i, 0, 0))],
        out_specs=pl.BlockSpec(memory_space=pl.ANY),
        scratch_shapes=[pltpu.VMEM((2, SORTED_ROWS, width), F32),
                        pltpu.VMEM((FFN_BLOCK, width), F32),
                        pltpu.SemaphoreType.DMA((2,))],
    )
    return pl.pallas_call(
        _dispatch_kernel,
        grid_spec=grid_spec,
        out_shape=jax.ShapeDtypeStruct((tbl["cap"], width), F32),
        compiler_params=_params("arbitrary"),
        name="dispatch",
    )(tbl["dst0"], tbl["tile_len"], tbl["loc"], tbl["ncopy"], tbl["zero0"], tbl["nzero"], tbl["n_real"],
      hx, meta_t)


def _ffn_kernel(exp_ref, nreal_ref, x_ref, wgu_ref, bgu_ref, wd_ref, bd_ref, y_ref, wgu_bf, wd_bf):
    b = pl.program_id(0)
    d_ff, d = wd_ref.shape[1], wd_ref.shape[2]

    @pl.when(b >= nreal_ref[0])
    def _():
        y_ref[...] = jnp.zeros_like(y_ref)

    @pl.when((b == 0) | (exp_ref[b] != exp_ref[jnp.maximum(b - 1, 0)]))
    def _():
        wgu_bf[...] = wgu_ref[0].astype(BF16)
        wd_bf[...] = wd_ref[0].astype(BF16)

    @pl.when(b < nreal_ref[0])
    def _():
        x = x_ref[...]
        ext = x[:, d:]
        expert = exp_ref[b].astype(F32)
        gate = jnp.zeros((x.shape[0], 1), F32)
        for k in range(TOP_K):
            g0 = EXT_GATE + GATE_PARTS * k
            g = ext[:, g0:g0 + 1] + ext[:, g0 + 1:g0 + 2] + ext[:, g0 + 2:g0 + 3]
            gate = gate + jnp.where(ext[:, EXT_EXPERT + k:EXT_EXPERT + k + 1] == expert, g, 0.0)
        gu = _dot(x[:, :d].astype(BF16), wgu_bf[...]) + bgu_ref[0]
        g = jnp.minimum(gu[:, :d_ff], SWIGLU_LIMIT)
        u = jnp.clip(gu[:, d_ff:], -SWIGLU_LIMIT, SWIGLU_LIMIT)
        act = (u + 1.0) * (g * jax.nn.sigmoid(SWIGLU_ALPHA * g))
        y_ref[...] = gate * (_dot(act.astype(BF16), wd_bf[...]) + bd_ref[0])


def _ffn(xb, tbl, w_gate_up, b_gate_up, w_down, b_down):
    cap, width = xb.shape
    e, d, two_ff = w_gate_up.shape
    d_ff = two_ff // 2
    grid_spec = pltpu.PrefetchScalarGridSpec(
        num_scalar_prefetch=2,
        grid=(cap // FFN_BLOCK,),
        in_specs=[pl.BlockSpec((FFN_BLOCK, width), lambda b, ex, nr: (b, 0)),
                  pl.BlockSpec((1, d, two_ff), lambda b, ex, nr: (ex[b], 0, 0)),
                  pl.BlockSpec((1, 1, two_ff), lambda b, ex, nr: (ex[b], 0, 0)),
                  pl.BlockSpec((1, d_ff, d), lambda b, ex, nr: (ex[b], 0, 0)),
                  pl.BlockSpec((1, 1, d), lambda b, ex, nr: (ex[b], 0, 0))],
        out_specs=pl.BlockSpec((FFN_BLOCK, d), lambda b, ex, nr: (b, 0)),
        scratch_shapes=[pltpu.VMEM((d, two_ff), BF16), pltpu.VMEM((d_ff, d), BF16)],
    )
    return pl.pallas_call(
        _ffn_kernel,
        grid_spec=grid_spec,
        out_shape=jax.ShapeDtypeStruct((cap, d), F32),
        compiler_params=_params("arbitrary"),
        name="expert_ffn",
    )(tbl["blk_exp"], tbl["n_real"], xb, w_gate_up,
      b_gate_up.reshape(e, 1, two_ff), w_down, b_down.reshape(e, 1, d))


def _combine_kernel(dst0_ref, len_ref, loc_ref, ncopy_ref, meta_ref, h_ref, y_ref, o_ref, srt_ref, sem):
    t = pl.program_id(0)
    tm = h_ref.shape[0]

    @pl.when(t == 0)
    def _():
        srt_ref[...] = jnp.zeros_like(srt_ref)

    slot = t % 2

    def copy_into(s):
        def copy(seg_row, loc_row, rows):
            return pltpu.make_async_copy(y_ref.at[pl.ds(seg_row, rows)],
                                         srt_ref.at[s, pl.ds(loc_row, rows)], sem.at[s])
        return copy

    @pl.when(t == 0)
    def _():
        _start_copies(t, dst0_ref, len_ref, loc_ref, copy_into(slot), True)

    _wait_copies(t, ncopy_ref, copy_into(slot))
    last = pl.num_programs(0) - 1

    meta = meta_ref[...]
    rows = [meta[:, k:k + 1] for k in range(TOP_K)]
    part = SORTED_ROWS // SORT_PARTS

    def onehot_part(c):
        pcol = (lax.broadcasted_iota(jnp.int32, (tm, part), 1) + c * part).astype(F32)
        onehot = jnp.zeros((tm, part), F32)
        for k in range(TOP_K):
            onehot = jnp.where(pcol == rows[k], 1.0, onehot)
        return onehot.astype(BF16)

    _start_copies(jnp.minimum(t + 1, last), dst0_ref, len_ref, loc_ref, copy_into(1 - slot), t < last)
    onehots = [onehot_part(c) for c in range(SORT_PARTS)]
    out = h_ref[...]
    for c in range(SORT_PARTS):
        out = out + _dot(onehots[c], srt_ref[slot, c * part:(c + 1) * part].astype(BF16))
    o_ref[...] = out


def _combine(meta, tbl, h1, yb):
    n, d = h1.shape
    tm = TOKEN_TILE
    grid_spec = pltpu.PrefetchScalarGridSpec(
        num_scalar_prefetch=4,
        grid=(n // tm,),
        in_specs=[pl.BlockSpec((tm, LANES), lambda i, *_: (i, 0)),
                  pl.BlockSpec((tm, d), lambda i, *_: (i, 0)),
                  pl.BlockSpec(memory_space=pl.ANY)],
        out_specs=pl.BlockSpec((tm, d), lambda i, *_: (i, 0)),
        scratch_shapes=[pltpu.VMEM((2, SORTED_ROWS, d), F32), pltpu.SemaphoreType.DMA((2,))],
    )
    return pl.pallas_call(
        _combine_kernel,
        grid_spec=grid_spec,
        out_shape=jax.ShapeDtypeStruct((n, d), F32),
        compiler_params=_params("arbitrary"),
        name="combine",
    )(tbl["dst0"], tbl["tile_len"], tbl["loc"], tbl["ncopy"], meta, h1, yb)


def kernel(x, attn_norm_w, w_in, q_norm_w, k_norm_w, sinks, w_proj_swa, w_proj_sb, w_out,
           ffn_norm_w, router_w, router_b, w_gate_up, b_gate_up, w_down, b_down):
    batch, seq, d = x.shape
    n = batch * seq
    h = x.reshape(n, d)
    for layer in range(attn_norm_w.shape[0]):
        q_a, kv_a, q_b, k_b, v_b, g_a, g_b = _in_proj(h, attn_norm_w[layer], w_in[layer].astype(BF16))
        y_a = _swa(q_a, kv_a, q_norm_w[layer], k_norm_w[layer], sinks[layer], batch, seq)
        y_b = _sb(q_b, k_b, v_b, batch, seq)
        h1 = _merge(y_a, y_b, g_a, g_b, h, w_proj_swa[layer].astype(BF16),
                    w_proj_sb[layer].astype(BF16), w_out[layer].astype(BF16))
        hx, meta, meta_t, tile_run, counts = _router(h1, ffn_norm_w[layer], router_w[layer], router_b[layer])
        tbl = _routing_tables(tile_run, counts, n)
        xb = _dispatch(hx, meta_t, tbl)
        yb = _ffn(xb, tbl, w_gate_up[layer], b_gate_up[layer], w_down[layer], b_down[layer])
        h = _combine(meta, tbl, h1, yb)
    return h.reshape(batch, seq, d)
```

```python
import jax
import jax.numpy as jnp
from jax import lax
from jax.experimental import pallas as pl
from jax.experimental.pallas import tpu as pltpu

HEAD_DIM = 64
SWA_HEADS = 8
SWA_KV_HEADS = 2
SWA_GROUP = SWA_HEADS // SWA_KV_HEADS
BLOCK = 128
SB_HEADS = 8
SWA_WIDTH = SWA_HEADS * HEAD_DIM
SWA_KV_WIDTH = SWA_KV_HEADS * HEAD_DIM
SB_WIDTH = SB_HEADS * HEAD_DIM
N_EXPERTS = 32
TOP_K = 4
SWIGLU_LIMIT = 7.0
SWIGLU_ALPHA = 1.702
NORM_EPS = 1e-5
ATTN_SCALE = HEAD_DIM ** -0.5
SIGN_BIT = -2 ** 31
LOG2E = 1.4426950408889634

LANES = 128
SUBLANES = 8
DENSE_TILE = 1024
TOKEN_TILE = 512
FFN_BLOCK = 512
SORTED_ROWS = -(-(TOKEN_TILE * TOP_K + N_EXPERTS * (SUBLANES - 1)) // LANES) * LANES
SWA_BLOCKS = 8
SORT_PARTS = 3
SB_Q_ROWS = 1024
SB_K_ROWS = 256
VMEM_LIMIT = 56 * 1024 * 1024

F32 = jnp.float32
BF16 = jnp.bfloat16
NEG_BIG = -1e30


def _dot(a, b):
    return jnp.dot(a, b, preferred_element_type=F32)


def _dot_nt(a, b):
    return lax.dot_general(a, b, (((1,), (1,)), ((), ())), preferred_element_type=F32)


def _sigmoid(x):
    return jax.nn.sigmoid(x)


def _rms(x, w):
    return x * lax.rsqrt(jnp.mean(x * x, axis=-1, keepdims=True) + NORM_EPS) * w


def _params(*sem):
    return pltpu.CompilerParams(dimension_semantics=sem, vmem_limit_bytes=VMEM_LIMIT)


def _in_proj_kernel(x_ref, nw_ref, w_ref, qa_ref, kva_ref, qb_ref, kb_ref, vb_ref, ga_ref, gb_ref):
    xn = _rms(x_ref[...], nw_ref[...]).astype(BF16)
    off = 0
    for ref, gate in ((qa_ref, False), (kva_ref, False), (qb_ref, False), (kb_ref, False),
                      (vb_ref, False), (ga_ref, True), (gb_ref, True)):
        width = ref.shape[1]
        y = _dot(xn, w_ref[:, off:off + width])
        if gate:
            y = _sigmoid(y)
        if ref is qb_ref:
            y = y * (ATTN_SCALE * LOG2E)
        ref[...] = y.astype(ref.dtype)
        off += width


def _in_proj(x2, norm_w, w_in):
    n, d = x2.shape
    widths = (SWA_WIDTH, 2 * SWA_KV_WIDTH, SB_WIDTH, SB_WIDTH, SB_WIDTH, d, d)
    tm = DENSE_TILE
    return pl.pallas_call(
        _in_proj_kernel,
        grid=(n // tm,),
        in_specs=[pl.BlockSpec((tm, d), lambda i: (i, 0)),
                  pl.BlockSpec((1, d), lambda i: (0, 0)),
                  pl.BlockSpec(w_in.shape, lambda i: (0, 0))],
        out_specs=[pl.BlockSpec((tm, w), lambda i: (i, 0)) for w in widths],
        out_shape=[jax.ShapeDtypeStruct((n, w), BF16) for w in widths],
        compiler_params=_params("parallel"),
        name="in_proj",
    )(x2, norm_w.reshape(1, d), w_in)


def _swa_kernel(q_ref, kvc_ref, kvp_ref, qnw_ref, knw_ref, sink_ref, o_ref):
    i = pl.program_id(1)
    half = HEAD_DIM

    def group_mean_sq(x):
        w = x.shape[1]
        same_head = (lax.broadcasted_iota(jnp.int32, (w, w), 0) // half
                     == lax.broadcasted_iota(jnp.int32, (w, w), 1) // half)
        avg = jnp.where(same_head, 1.0 / half, 0.0).astype(BF16)
        hi, lo = _split_bf16(x * x)
        return _dot(hi, avg) + _dot(lo, avg)

    q = q_ref[...].astype(F32)
    kv = jnp.concatenate([kvp_ref[...], kvc_ref[...]], axis=0)
    k = kv[:, :SWA_KV_WIDTH].astype(F32)
    v = kv[:, SWA_KV_WIDTH:].astype(F32)
    qn = (q * lax.rsqrt(group_mean_sq(q) + NORM_EPS) * qnw_ref[...] * (ATTN_SCALE * LOG2E)).astype(BF16)
    kn = k * lax.rsqrt(group_mean_sq(k) + NORM_EPS) * knw_ref[...]
    lane = lax.broadcasted_iota(jnp.int32, kn.shape, 1)
    low = lane < half
    k_swap = pltpu.roll(kn, half, 1)
    v_swap = pltpu.roll(v, half, 1).astype(BF16)
    v_same = v.astype(BF16)
    keys = [[jnp.where(low, kn, 0.0).astype(BF16), jnp.where(low, 0.0, k_swap).astype(BF16)],
            [jnp.where(low, k_swap, 0.0).astype(BF16), jnp.where(low, 0.0, kn).astype(BF16)]]
    vals = [[v_same, v_swap], [v_swap, v_same]]

    row = lax.broadcasted_iota(jnp.int32, (BLOCK, 2 * BLOCK), 0)
    col = lax.broadcasted_iota(jnp.int32, (BLOCK, 2 * BLOCK), 1)
    rel = row + BLOCK - col
    in_window = (rel >= 0) & (rel < BLOCK)
    rel = rel.astype(F32)
    out_lane_low = lax.broadcasted_iota(jnp.int32, (BLOCK, LANES), 1) < half
    for sub in range(SWA_BLOCKS):
        rows = slice(sub * BLOCK, (sub + 1) * BLOCK)
        band = slice(sub * BLOCK, (sub + 2) * BLOCK)
        valid = in_window & ((col >= BLOCK) | (i > 0)) if sub == 0 else in_window
        for blk in range(SWA_HEADS // 2):
            qb = qn[rows, blk * LANES:(blk + 1) * LANES]
            halves = []
            for par in range(2):
                h = 2 * blk + par
                g = h // SWA_GROUP
                slope = 2.0 ** (-8.0 * (h + 1) / SWA_HEADS) * LOG2E
                s = jnp.where(valid, _dot_nt(qb, keys[g][par][band]) - slope * rel, NEG_BIG)
                sink = sink_ref[h] * LOG2E
                m = jnp.maximum(jnp.max(s, axis=-1, keepdims=True), sink)
                p = jnp.exp2(s - m)
                den = jnp.sum(p, axis=-1, keepdims=True) + jnp.exp2(sink - m)
                halves.append(_dot(p.astype(BF16), vals[g][par][band]) / den)
            o_ref[rows, blk * LANES:(blk + 1) * LANES] = jnp.where(
                out_lane_low, halves[0], halves[1]).astype(o_ref.dtype)


def _swa(q_a, kv_a, q_norm_w, k_norm_w, sinks, batch, seq):
    n = q_a.shape[0]
    rows = SWA_BLOCKS * BLOCK
    nb = seq // rows
    return pl.pallas_call(
        _swa_kernel,
        grid=(batch, nb),
        in_specs=[pl.BlockSpec((rows, SWA_WIDTH), lambda b, i: (b * nb + i, 0)),
                  pl.BlockSpec((rows, 2 * SWA_KV_WIDTH), lambda b, i: (b * nb + i, 0)),
                  pl.BlockSpec((BLOCK, 2 * SWA_KV_WIDTH),
                               lambda b, i: (jnp.maximum((b * nb + i) * SWA_BLOCKS - 1, 0), 0)),
                  pl.BlockSpec((1, SWA_WIDTH), lambda b, i: (0, 0)),
                  pl.BlockSpec((1, SWA_KV_WIDTH), lambda b, i: (0, 0)),
                  pl.BlockSpec(memory_space=pltpu.SMEM)],
        out_specs=pl.BlockSpec((rows, SWA_WIDTH), lambda b, i: (b * nb + i, 0)),
        out_shape=jax.ShapeDtypeStruct((n, SWA_WIDTH), BF16),
        compiler_params=_params("parallel", "parallel"),
        name="swa",
    )(q_a, kv_a, kv_a, jnp.tile(q_norm_w, SWA_HEADS).reshape(1, SWA_WIDTH),
      jnp.tile(k_norm_w, SWA_KV_HEADS).reshape(1, SWA_KV_WIDTH), sinks)


def _split_bf16(x):
    hi = x.astype(BF16)
    lo = (x - hi.astype(F32)).astype(BF16)
    return hi, lo


def _sb_kernel(q_ref, k_ref, v_ref, o_ref):
    seq = q_ref.shape[0]
    tq, tk = SB_Q_ROWS, SB_K_ROWS
    nq = seq // tq
    k_per_q = tq // tk
    lane = lax.broadcasted_iota(jnp.int32, (tq, LANES), 1)
    row = lax.broadcasted_iota(jnp.int32, (tq, tk), 0)
    col = lax.broadcasted_iota(jnp.int32, (tq, tk), 1)
    neg_later = jnp.where(lax.broadcasted_iota(jnp.int32, (tk, tk), 0)
                          > lax.broadcasted_iota(jnp.int32, (tk, tk), 1), -1.0, 0.0).astype(BF16)

    def softplus(z):
        neg_abs = lax.bitcast_convert_type(lax.bitcast_convert_type(z, jnp.int32) | SIGN_BIT, F32)
        return jnp.maximum(z, 0.0) + jnp.log(1.0 + jnp.exp2(neg_abs)) * LOG2E

    def neg_suffix_sum(sp):
        return _dot(sp.astype(BF16), neg_later)

    def add_from_row(x, r0, delta):
        return x + delta if r0 == 0 else jnp.concatenate([x[:r0], x[r0:] + delta], axis=0)

    def k_span(k0s, qms, state, masks, first_rows):
        pairs = [(i, p) for p in range(2) for i in range(len(k0s))]
        kbs = [k_ref[pl.ds(k0, tk), :] for k0 in k0s]
        vbs = [v_ref[pl.ds(k0, tk), :] for k0 in k0s]
        accs, runs = list(state[:2]), list(state[2:])
        zs, sps, cs, ws, firsts = {}, {}, {}, {}, {}

        def stage(s, i, p):
            r0 = first_rows[i]
            if s == 0:
                zs[(i, p)] = _dot_nt(qms[p][r0:], kbs[i])
            elif s == 1:
                z = zs[(i, p)]
                sp = softplus(z)
                log_beta = z - sp
                zs[(i, p)] = log_beta
                firsts[(i, p)] = (z[:, 0:1] if masks[i] is None
                                  else jnp.where(masks[i][:, 0:1], z[:, 0:1], log_beta[:, 0:1]))
                sps[(i, p)] = sp if masks[i] is None else jnp.where(masks[i], sp, 0.0)
            elif s == 2:
                cs[(i, p)] = neg_suffix_sum(sps[(i, p)])
            elif s == 3:
                expo = zs[(i, p)] + cs[(i, p)] + runs[p][r0:]
                w = jnp.exp2(expo)
                ws[(i, p)] = w if masks[i] is None else jnp.where(masks[i], w, 0.0)
                new_run = expo[:, 0:1] - firsts[(i, p)]
                runs[p] = new_run if r0 == 0 else jnp.concatenate([runs[p][:r0], new_run], axis=0)
            else:
                accs[p] = add_from_row(accs[p], r0, _dot(ws[(i, p)].astype(BF16), vbs[i]))

        n_stages = 5
        for step in range(len(pairs) - 1 + n_stages):
            for n, (i, p) in enumerate(pairs):
                s = step - n
                if 0 <= s < n_stages:
                    stage(s, i, p)
        return (*accs, *runs)

    def q_block(qi, carry):
        q0 = pl.multiple_of(qi * tq, tq)
        q2 = q_ref[pl.ds(q0, tq), :]
        qms = (jnp.where(lane < HEAD_DIM, q2, 0.0).astype(BF16),
               jnp.where(lane >= HEAD_DIM, q2, 0.0).astype(BF16))
        state = (jnp.zeros((tq, LANES), F32), jnp.zeros((tq, LANES), F32),
                 jnp.zeros((tq, 1), F32), jnp.zeros((tq, 1), F32))
        order = list(reversed(range(k_per_q)))
        state = k_span([q0 + d * tk for d in order], qms, state,
                       [((col + d * tk) < row)[d * tk:] for d in order], [d * tk for d in order])

        def earlier(jj, state):
            k0s = [pl.multiple_of((qi - 1 - jj) * tq + d * tk, tk) for d in order]
            return k_span(k0s, qms, state, [None] * k_per_q, [0] * k_per_q)

        state = lax.fori_loop(0, qi, earlier, state)
        o_ref[pl.ds(q0, tq), :] = jnp.where(lane < HEAD_DIM, state[0], state[1]).astype(o_ref.dtype)
        return carry

    lax.fori_loop(0, nq, q_block, 0)


def _sb(q_b, k_b, v_b, batch, seq):
    n = q_b.shape[0]
    spec = pl.BlockSpec((seq, LANES), lambda b, p: (b, p))
    return pl.pallas_call(
        _sb_kernel,
        grid=(batch, SB_WIDTH // LANES),
        in_specs=[spec, spec, spec],
        out_specs=spec,
        out_shape=jax.ShapeDtypeStruct((n, SB_WIDTH), BF16),
        compiler_params=_params("parallel", "parallel"),
        name="stick_breaking",
    )(q_b, k_b, v_b)


def _merge_kernel(ya_ref, yb_ref, ga_ref, gb_ref, x_ref, pa_ref, pb_ref, wo_ref, h_ref):
    m = (ga_ref[...].astype(F32) * _dot(ya_ref[...], pa_ref[...])
         + gb_ref[...].astype(F32) * _dot(yb_ref[...], pb_ref[...]))
    h_ref[...] = x_ref[...] + _dot(m.astype(BF16), wo_ref[...])


def _merge(y_a, y_b, g_a, g_b, x2, p_a, p_b, w_o):
    n, d = x2.shape
    tm = DENSE_TILE
    row = lambda w: pl.BlockSpec((tm, w), lambda i: (i, 0))
    full = lambda a: pl.BlockSpec(a.shape, lambda i: (0, 0))
    return pl.pallas_call(
        _merge_kernel,
        grid=(n // tm,),
        in_specs=[row(SWA_WIDTH), row(SB_WIDTH), row(d), row(d), row(d), full(p_a), full(p_b), full(w_o)],
        out_specs=row(d),
        out_shape=jax.ShapeDtypeStruct((n, d), F32),
        compiler_params=_params("parallel"),
        name="merge",
    )(y_a, y_b, g_a, g_b, x2, p_a, p_b, w_o)


EXT_EXPERT = 0
EXT_GATE = TOP_K
GATE_PARTS = 3


def _router_kernel(h_ref, nw_ref, rw_ref, rb_ref, hx_ref, meta_ref, meta_t_ref, trun_ref, cnt_ref, run_ref):
    tm, d = h_ref.shape

    @pl.when(pl.program_id(0) == 0)
    def _():
        run_ref[...] = jnp.zeros_like(run_ref)

    hn = _rms(h_ref[...], nw_ref[...])
    hn_hi, hn_lo = _split_bf16(hn)
    both = _dot(hn_hi, rw_ref[...])
    logits = (both[:, :N_EXPERTS] + both[:, N_EXPERTS:] + _dot(hn_lo, rw_ref[:, :N_EXPERTS])
              + rb_ref[...])
    lane = lax.broadcasted_iota(jnp.int32, (tm, N_EXPERTS), 1).astype(F32)
    work = logits
    tops, idxs, hots = [], [], []
    for _ in range(TOP_K):
        m = jnp.max(work, axis=-1, keepdims=True)
        idx = jnp.min(jnp.where(work == m, lane, float(N_EXPERTS)), axis=-1, keepdims=True)
        hot = lane == idx
        work = jnp.where(hot, -jnp.inf, work)
        tops.append(m)
        idxs.append(idx)
        hots.append(hot)
    exps = [jnp.exp(t - tops[0]) for t in tops]
    den = exps[0] + exps[1] + exps[2] + exps[3]
    chosen = jnp.where(hots[0] | hots[1] | hots[2] | hots[3], 1.0, 0.0)
    r = lax.broadcasted_iota(jnp.int32, (tm, tm), 0)
    c = lax.broadcasted_iota(jnp.int32, (tm, tm), 1)
    before = jnp.where(c < r, 1.0, 0.0).astype(BF16)
    run = run_ref[...]
    tile_cnt = jnp.sum(chosen, axis=0, keepdims=True)
    units = jnp.ceil(tile_cnt / SUBLANES)
    earlier = jnp.where(lax.broadcasted_iota(jnp.int32, (N_EXPERTS, N_EXPERTS), 0)
                        < lax.broadcasted_iota(jnp.int32, (N_EXPERTS, N_EXPERTS), 1), 1.0, 0.0)
    group_row = SUBLANES * _dot(jnp.broadcast_to(units, (SUBLANES, N_EXPERTS)).astype(BF16),
                                earlier.astype(BF16))[0:1]
    place = _dot(before, chosen.astype(BF16)) + group_row
    mlane = lax.broadcasted_iota(jnp.int32, (tm, LANES), 1)
    meta = jnp.full((tm, LANES), -1.0, F32)
    ext = jnp.zeros((tm, LANES), F32)
    for k in range(TOP_K):
        row_k = jnp.sum(jnp.where(hots[k], place, 0.0), axis=-1, keepdims=True)
        meta = jnp.where(mlane == k, row_k, meta)
        ext = jnp.where(mlane == EXT_EXPERT + k, idxs[k], ext)
        rest = exps[k] / den
        for part in range(GATE_PARTS):
            piece = rest.astype(BF16).astype(F32)
            ext = jnp.where(mlane == EXT_GATE + GATE_PARTS * k + part, piece, ext)
            rest = rest - piece
    meta_ref[...] = meta
    meta_t_ref[0] = meta.T[:SUBLANES]
    hx_ref[:, :d] = hn_hi
    hx_ref[:, d:] = ext.astype(BF16)
    trun_ref[0] = run
    run = run + units * SUBLANES
    run_ref[...] = run
    cnt_ref[...] = run


def _router(h1, norm_w, router_w, router_b):
    n, d = h1.shape
    tm = TOKEN_TILE
    tiles = n // tm
    return pl.pallas_call(
        _router_kernel,
        grid=(tiles,),
        in_specs=[pl.BlockSpec((tm, d), lambda i: (i, 0)),
                  pl.BlockSpec((1, d), lambda i: (0, 0)),
                  pl.BlockSpec((d, 2 * N_EXPERTS), lambda i: (0, 0)),
                  pl.BlockSpec((1, N_EXPERTS), lambda i: (0, 0))],
        out_specs=[pl.BlockSpec((tm, d + LANES), lambda i: (i, 0)),
                   pl.BlockSpec((tm, LANES), lambda i: (i, 0)),
                   pl.BlockSpec((1, SUBLANES, tm), lambda i: (i, 0, 0)),
                   pl.BlockSpec((1, 1, N_EXPERTS), lambda i: (i, 0, 0)),
                   pl.BlockSpec((1, N_EXPERTS), lambda i: (0, 0))],
        out_shape=[jax.ShapeDtypeStruct((n, d + LANES), BF16),
                   jax.ShapeDtypeStruct((n, LANES), F32),
                   jax.ShapeDtypeStruct((tiles, SUBLANES, tm), F32),
                   jax.ShapeDtypeStruct((tiles, 1, N_EXPERTS), F32),
                   jax.ShapeDtypeStruct((1, N_EXPERTS), F32)],
        scratch_shapes=[pltpu.VMEM((1, N_EXPERTS), F32)],
        compiler_params=_params("arbitrary"),
        name="router",
    )(h1, norm_w.reshape(1, d), jnp.concatenate(_split_bf16(router_w), axis=1),
      router_b.reshape(1, N_EXPERTS))


def _routing_tables(tile_run, counts, n):
    tiles = tile_run.shape[0]
    counts = counts.reshape(N_EXPERTS).astype(jnp.int32)
    run = tile_run.reshape(tiles, N_EXPERTS).astype(jnp.int32)
    tile_len = jnp.concatenate([run[1:], counts[None]], axis=0) - run
    loc = jnp.cumsum(tile_len, axis=1) - tile_len
    padded = (counts + FFN_BLOCK - 1) // FFN_BLOCK * FFN_BLOCK
    seg_end = jnp.cumsum(padded)
    seg_start = seg_end - padded
    dst0 = seg_start[None, :] + run
    zero0 = seg_start + counts
    nzero = (padded - counts) // SUBLANES
    n_real = seg_end[-1] // FFN_BLOCK
    max_blocks = (n * TOP_K + tiles * N_EXPERTS * (SUBLANES - 1)
                  + N_EXPERTS * (FFN_BLOCK - 1)) // FFN_BLOCK
    blk_exp = jnp.minimum(jnp.sum(jnp.arange(max_blocks, dtype=jnp.int32)[:, None] * FFN_BLOCK
                                  >= seg_end[None, :], axis=1), N_EXPERTS - 1).astype(jnp.int32)
    ncopy = jnp.stack([jnp.sum((tile_len // rows) % 2, axis=1) for rows in _copy_sizes()],
                      axis=1).astype(jnp.int32)
    return dict(dst0=dst0.reshape(-1), tile_len=tile_len.reshape(-1), loc=loc.reshape(-1),
                ncopy=ncopy.reshape(-1), zero0=zero0, nzero=nzero, blk_exp=blk_exp,
                n_real=n_real.reshape(1).astype(jnp.int32), cap=max_blocks * FFN_BLOCK)


def _copy_sizes():
    sizes, rows = [], TOKEN_TILE
    while rows >= SUBLANES:
        sizes.append(rows)
        rows //= 2
    return sizes


def _wait_copies(t, ncopy_ref, make_copy):
    sizes = _copy_sizes()
    for c, rows in enumerate(sizes):
        def body(_, carry, rows=rows):
            make_copy(0, 0, rows).wait()
            return carry

        lax.fori_loop(0, ncopy_ref[t * len(sizes) + c], body, 0)


def _start_copies(t, tbl_ref, len_ref, loc_ref, make_copy, enabled):
    for e in range(N_EXPERTS):
        idx = t * N_EXPERTS + e
        seg_row = tbl_ref[idx]
        loc_row = loc_ref[idx]
        length = len_ref[idx]
        done = 0
        for rows in _copy_sizes():
            @pl.when(((length & rows) != 0) & enabled)
            def _(done=done, rows=rows):
                make_copy(pl.multiple_of(seg_row + done, SUBLANES),
                          pl.multiple_of(loc_row + done, SUBLANES), rows).start(priority=e % 2)
            done = done + (length & rows)


def _dispatch_kernel(dst0_ref, len_ref, loc_ref, ncopy_ref, zero0_ref, nzero_ref, nreal_ref,
                     hx_ref, rows_t_ref, xb_ref, srt_ref, zero_ref, sem):
    t = pl.program_id(0)
    tm = hx_ref.shape[0]
    rows_t = rows_t_ref[0]
    slot = t % 2

    def copy_from(s):
        def copy(seg_row, loc_row, rows):
            return pltpu.make_async_copy(srt_ref.at[s, pl.ds(loc_row, rows)],
                                         xb_ref.at[pl.ds(seg_row, rows)], sem.at[s])
        return copy

    _start_copies(jnp.maximum(t - 1, 0), dst0_ref, len_ref, loc_ref, copy_from(1 - slot), t > 0)
    part = SORTED_ROWS // SORT_PARTS
    for c in range(SORT_PARTS):
        prow = (lax.broadcasted_iota(jnp.int32, (part, tm), 0) + c * part).astype(F32)
        onehot = jnp.zeros((part, tm), F32)
        for k in range(TOP_K):
            onehot = jnp.where(prow == rows_t[k:k + 1, :], 1.0, onehot)
        srt_ref[slot, c * part:(c + 1) * part] = _dot(onehot.astype(BF16), hx_ref[...])

    @pl.when(t > 0)
    def _():
        _wait_copies(t - 1, ncopy_ref, copy_from(1 - slot))

    @pl.when(t == pl.num_programs(0) - 1)
    def _():
        _start_copies(t, dst0_ref, len_ref, loc_ref, copy_from(slot), True)
        _wait_copies(t, ncopy_ref, copy_from(slot))
        zero_ref[...] = jnp.zeros_like(zero_ref)
        for op in ("start", "wait"):
            for e in range(N_EXPERTS):
                def body(c, carry):
                    row0 = pl.multiple_of(zero0_ref[e] + c * SUBLANES, SUBLANES)
                    cp = pltpu.make_async_copy(zero_ref.at[pl.ds(0, SUBLANES)],
                                               xb_ref.at[pl.ds(row0, SUBLANES)], sem.at[0])
                    cp.start() if op == "start" else cp.wait()
                    return carry

                lax.fori_loop(0, nzero_ref[e], body, 0)

            def tail(b, carry):
                row0 = pl.multiple_of(b * FFN_BLOCK, FFN_BLOCK)
                cp = pltpu.make_async_copy(zero_ref, xb_ref.at[pl.ds(row0, FFN_BLOCK)], sem.at[0])
                cp.start() if op == "start" else cp.wait()
                return carry

            lax.fori_loop(nreal_ref[0], xb_ref.shape[0] // FFN_BLOCK, tail, 0)


def _dispatch(hx, meta_t, tbl):
    n, width = hx.shape
    tm = TOKEN_TILE
    grid_spec = pltpu.PrefetchScalarGridSpec(
        num_scalar_prefetch=7,
        grid=(n // tm,),
        in_specs=[pl.BlockSpec((tm, width), lambda i, *_: (i, 0)),
                  pl.BlockSpec((1, SUBLANES, tm), lambda i, *_: (i, 0, 0))],
        out_specs=pl.BlockSpec(memory_space=pl.ANY),
        scratch_shapes=[pltpu.VMEM((2, SORTED_ROWS, width), F32),
                        pltpu.VMEM((FFN_BLOCK, width), F32),
                        pltpu.SemaphoreType.DMA((2,))],
    )
    return pl.pallas_call(
        _dispatch_kernel,
        grid_spec=grid_spec,
        out_shape=jax.ShapeDtypeStruct((tbl["cap"], width), F32),
        compiler_params=_params("arbitrary"),
        name="dispatch",
    )(tbl["dst0"], tbl["tile_len"], tbl["loc"], tbl["ncopy"], tbl["zero0"], tbl["nzero"], tbl["n_real"],
      hx, meta_t)


def _ffn_kernel(exp_ref, nreal_ref, x_ref, wgu_ref, bgu_ref, wd_ref, bd_ref, y_ref, wgu_bf, wd_bf):
    b = pl.program_id(0)
    d_ff, d = wd_ref.shape[1], wd_ref.shape[2]

    @pl.when(b >= nreal_ref[0])
    def _():
        y_ref[...] = jnp.zeros_like(y_ref)

    @pl.when((b == 0) | (exp_ref[b] != exp_ref[jnp.maximum(b - 1, 0)]))
    def _():
        wgu_bf[...] = wgu_ref[0].astype(BF16)
        wd_bf[...] = wd_ref[0].astype(BF16)

    @pl.when(b < nreal_ref[0])
    def _():
        x = x_ref[...]
        ext = x[:, d:]
        expert = exp_ref[b].astype(F32)
        gate = jnp.zeros((x.shape[0], 1), F32)
        for k in range(TOP_K):
            g0 = EXT_GATE + GATE_PARTS * k
            g = ext[:, g0:g0 + 1] + ext[:, g0 + 1:g0 + 2] + ext[:, g0 + 2:g0 + 3]
            gate = gate + jnp.where(ext[:, EXT_EXPERT + k:EXT_EXPERT + k + 1] == expert, g, 0.0)
        gu = _dot(x[:, :d].astype(BF16), wgu_bf[...]) + bgu_ref[0]
        g = jnp.minimum(gu[:, :d_ff], SWIGLU_LIMIT)
        u = jnp.clip(gu[:, d_ff:], -SWIGLU_LIMIT, SWIGLU_LIMIT)
        act = (u + 1.0) * (g * jax.nn.sigmoid(SWIGLU_ALPHA * g))
        y_ref[...] = gate * (_dot(act.astype(BF16), wd_bf[...]) + bd_ref[0])


def _ffn(xb, tbl, w_gate_up, b_gate_up, w_down, b_down):
    cap, width = xb.shape
    e, d, two_ff = w_gate_up.shape
    d_ff = two_ff // 2
    grid_spec = pltpu.PrefetchScalarGridSpec(
        num_scalar_prefetch=2,
        grid=(cap // FFN_BLOCK,),
        in_specs=[pl.BlockSpec((FFN_BLOCK, width), lambda b, ex, nr: (b, 0)),
                  pl.BlockSpec((1, d, two_ff), lambda b, ex, nr: (ex[b], 0, 0)),
                  pl.BlockSpec((1, 1, two_ff), lambda b, ex, nr: (ex[b], 0, 0)),
                  pl.BlockSpec((1, d_ff, d), lambda b, ex, nr: (ex[b], 0, 0)),
                  pl.BlockSpec((1, 1, d), lambda b, ex, nr: (ex[b], 0, 0))],
        out_specs=pl.BlockSpec((FFN_BLOCK, d), lambda b, ex, nr: (b, 0)),
        scratch_shapes=[pltpu.VMEM((d, two_ff), BF16), pltpu.VMEM((d_ff, d), BF16)],
    )
    return pl.pallas_call(
        _ffn_kernel,
        grid_spec=grid_spec,
        out_shape=jax.ShapeDtypeStruct((cap, d), F32),
        compiler_params=_params("arbitrary"),
        name="expert_ffn",
    )(tbl["blk_exp"], tbl["n_real"], xb, w_gate_up,
      b_gate_up.reshape(e, 1, two_ff), w_down, b_down.reshape(e, 1, d))


def _combine_kernel(dst0_ref, len_ref, loc_ref, ncopy_ref, meta_ref, h_ref, y_ref, o_ref, srt_ref, sem):
    t = pl.program_id(0)
    tm = h_ref.shape[0]

    @pl.when(t == 0)
    def _():
        srt_ref[...] = jnp.zeros_like(srt_ref)

    slot = t % 2

    def copy_into(s):
        def copy(seg_row, loc_row, rows):
            return pltpu.make_async_copy(y_ref.at[pl.ds(seg_row, rows)],
                                         srt_ref.at[s, pl.ds(loc_row, rows)], sem.at[s])
        return copy

    @pl.when(t == 0)
    def _():
        _start_copies(t, dst0_ref, len_ref, loc_ref, copy_into(slot), True)

    _wait_copies(t, ncopy_ref, copy_into(slot))
    last = pl.num_programs(0) - 1

    meta = meta_ref[...]
    rows = [meta[:, k:k + 1] for k in range(TOP_K)]
    part = SORTED_ROWS // SORT_PARTS

    def onehot_part(c):
        pcol = (lax.broadcasted_iota(jnp.int32, (tm, part), 1) + c * part).astype(F32)
        onehot = jnp.zeros((tm, part), F32)
        for k in range(TOP_K):
            onehot = jnp.where(pcol == rows[k], 1.0, onehot)
        return onehot.astype(BF16)

    _start_copies(jnp.minimum(t + 1, last), dst0_ref, len_ref, loc_ref, copy_into(1 - slot), t < last)
    onehots = [onehot_part(c) for c in range(SORT_PARTS)]
    out = h_ref[...]
    for c in range(SORT_PARTS):
        out = out + _dot(onehots[c], srt_ref[slot, c * part:(c + 1) * part].astype(BF16))
    o_ref[...] = out


def _combine(meta, tbl, h1, yb):
    n, d = h1.shape
    tm = TOKEN_TILE
    grid_spec = pltpu.PrefetchScalarGridSpec(
        num_scalar_prefetch=4,
        grid=(n // tm,),
        in_specs=[pl.BlockSpec((tm, LANES), lambda i, *_: (i, 0)),
                  pl.BlockSpec((tm, d), lambda i, *_: (i, 0)),
                  pl.BlockSpec(memory_space=pl.ANY)],
        out_specs=pl.BlockSpec((tm, d), lambda i, *_: (i, 0)),
        scratch_shapes=[pltpu.VMEM((2, SORTED_ROWS, d), F32), pltpu.SemaphoreType.DMA((2,))],
    )
    return pl.pallas_call(
        _combine_kernel,
        grid_spec=grid_spec,
        out_shape=jax.ShapeDtypeStruct((n, d), F32),
        compiler_params=_params("arbitrary"),
        name="combine",
    )(tbl["dst0"], tbl["tile_len"], tbl["loc"], tbl["ncopy"], meta, h1, yb)


def kernel(x, attn_norm_w, w_in, q_norm_w, k_norm_w, sinks, w_proj_swa, w_proj_sb, w_out,
           ffn_norm_w, router_w, router_b, w_gate_up, b_gate_up, w_down, b_down):
    batch, seq, d = x.shape
    n = batch * seq
    h = x.reshape(n, d)
    for layer in range(attn_norm_w.shape[0]):
        q_a, kv_a, q_b, k_b, v_b, g_a, g_b = _in_proj(h, attn_norm_w[layer], w_in[layer].astype(BF16))
        y_a = _swa(q_a, kv_a, q_norm_w[layer], k_norm_w[layer], sinks[layer], batch, seq)
        y_b = _sb(q_b, k_b, v_b, batch, seq)
        h1 = _merge(y_a, y_b, g_a, g_b, h, w_proj_swa[layer].astype(BF16),
                    w_proj_sb[layer].astype(BF16), w_out[layer].astype(BF16))
        hx, meta, meta_t, tile_run, counts = _router(h1, ffn_norm_w[layer], router_w[layer], router_b[layer])
        tbl = _routing_tables(tile_run, counts, n)
        xb = _dispatch(hx, meta_t, tbl)
        yb = _ffn(xb, tbl, w_gate_up[layer], b_gate_up[layer], w_down[layer], b_down[layer])
        h = _combine(meta, tbl, h1, yb)
    return h.reshape(batch, seq, d)
```
